```python
import math
import jax, jax.numpy as jnp
from jax import lax
import numpy as np

D_MODEL = 1024
BATCH = 8
SEQ = 4096
DEPTH = 2

GRID_W = 64
CTX_LEN = 256
NORM_EPS = 1e-6
N_HEADS = 8
N_KV_HEADS = 2
HEAD_DIM = 128
ATT_WIDTH = N_HEADS * HEAD_DIM
KV_WIDTH = N_KV_HEADS * HEAD_DIM
ROPE_THETA = 10000.0
Q_BLOCK = 128
ATT_SCALE = HEAD_DIM ** -0.5
HY_WIDTH = D_MODEL // 2
HY_ORDER = 2
HY_SHORT = 3
HY_FILTER_HIDDEN = 64
HY_BANDS = 16
HY_PE_DIM = 1 + 2 * HY_BANDS
HY_FAST_DECAY = 0.3
HY_SLOW_DECAY = 1.5
HY_DECAY_TARGET = 1e-2
N_EXPERTS = 64
N_GROUPS = 8
EXPERTS_PER_GROUP = N_EXPERTS // N_GROUPS
TOP_K = 2
EXPERT_DIM = 512
MOE_BLOCK = 256
IN_WIDTH = ATT_WIDTH + 2 * KV_WIDTH + 3 * HY_WIDTH + 2 * D_MODEL
SPLITS = (ATT_WIDTH, ATT_WIDTH + KV_WIDTH, ATT_WIDTH + 2 * KV_WIDTH,
          ATT_WIDTH + 2 * KV_WIDTH + 3 * HY_WIDTH)

kernel_name = 'hybrid_attn_hyena_moe_diffusion_trunk'


def rms_norm(x, w):
    xf = x.astype(jnp.float32)
    y = xf * lax.rsqrt(jnp.mean(xf * xf, axis=-1, keepdims=True) + NORM_EPS)
    return (y * w.astype(jnp.float32)).astype(x.dtype)


def modulate(h, shift, scale):
    return h * (1 + scale) + shift


def head_norm(t, w, n_heads):
    B, L = t.shape[:2]
    return rms_norm(t.reshape(B, L, n_heads, HEAD_DIM), w)


def axial_rope(L):
    rows = L // GRID_W
    row = jnp.broadcast_to(jnp.arange(rows)[:, None], (rows, GRID_W)).reshape(-1).astype(jnp.float32)
    col = jnp.broadcast_to(jnp.arange(GRID_W)[None, :], (rows, GRID_W)).reshape(-1).astype(jnp.float32)
    n = HEAD_DIM // 4
    inv = ROPE_THETA ** (-jnp.arange(n, dtype=jnp.float32) / n)
    ang = jnp.concatenate([row[:, None] * inv, col[:, None] * inv], axis=-1)
    return jnp.cos(ang), jnp.sin(ang)


def apply_rope(x, cos, sin):
    x1, x2 = jnp.split(x, 2, axis=-1)
    c = cos[None, :, None, :].astype(x.dtype)
    s = sin[None, :, None, :].astype(x.dtype)
    return jnp.concatenate([x1 * c - x2 * s, x1 * s + x2 * c], axis=-1)


def gqa_attention(q, k, v, block):
    B, L = q.shape[:2]
    G = N_HEADS // N_KV_HEADS
    nb = L // block
    qb = jnp.moveaxis(q.reshape(B, nb, block, N_KV_HEADS, G, HEAD_DIM), 1, 0)

    def attend(qblk):
        s = jnp.einsum('bqhgd,bkhd->bhgqk', qblk, k).astype(jnp.float32) * ATT_SCALE
        p = jax.nn.softmax(s, axis=-1).astype(v.dtype)
        return jnp.einsum('bhgqk,bkhd->bqhgd', p, v)

    o = lax.map(attend, qb)
    return jnp.moveaxis(o, 0, 1).reshape(B, L, ATT_WIDTH)


def hyena_filters(L, pe_w1, pe_b1, freq1, pe_w2, pe_b2, freq2, pe_w3):
    t01 = jnp.linspace(0.0, 1.0, L, dtype=jnp.float32)[:, None]
    pos = jnp.arange(L, dtype=jnp.float32)[:, None]
    bands = jnp.linspace(1e-4, HY_BANDS - 1, HY_BANDS, dtype=jnp.float32)[None, :]
    f = 2.0 * math.pi * pos * bands / L
    z = jnp.concatenate([t01, jnp.cos(f), -jnp.sin(f)], axis=-1).astype(pe_w1.dtype)
    h = jnp.sin(freq1 * (z @ pe_w1 + pe_b1))
    h = jnp.sin(freq2 * (h @ pe_w2 + pe_b2))
    h = h @ pe_w3
    max_decay = math.log(HY_DECAY_TARGET) / HY_FAST_DECAY
    min_decay = math.log(HY_DECAY_TARGET) / HY_SLOW_DECAY
    deltas = jnp.abs(jnp.linspace(min_decay, max_decay, HY_WIDTH, dtype=jnp.float32))
    window = jnp.exp(-t01 * deltas[None, :]).astype(h.dtype)
    return h.reshape(L, HY_ORDER, 2, HY_WIDTH) * window[:, None, None, :]


def bidir_long_conv(u, h_fwd, h_bwd):
    L, C = h_fwd.shape
    k = jnp.concatenate([h_fwd, jnp.zeros((1, C), h_fwd.dtype), h_bwd[:0:-1]], axis=0).astype(jnp.float32)
    U = jnp.fft.rfft(u.astype(jnp.float32), n=2 * L, axis=1)
    K = jnp.fft.rfft(k, n=2 * L, axis=0)
    y = jnp.fft.irfft(U * K[None], n=2 * L, axis=1)[:, :L]
    return y.astype(u.dtype)


def short_conv(u, w, b):
    up = jnp.pad(u, ((0, 0), (1, 1), (0, 0)))
    return up[:, :-2] * w[0] + up[:, 1:-1] * w[1] + up[:, 2:] * w[2] + b


def hyena_operator(u_proj, filt, conv_w, conv_b, skip):
    u = short_conv(u_proj, conv_w, conv_b)
    x1, x2, v = jnp.split(u, 3, axis=-1)
    z = x1 * (bidir_long_conv(v, filt[:, 0, 0], filt[:, 0, 1]) + skip[0] * v)
    return x2 * (bidir_long_conv(z, filt[:, 1, 0], filt[:, 1, 1]) + skip[1] * z)


def merge_branches(y_att, y_hy, g, w_att_proj, w_hy_proj, w_out):
    g_att, g_hy = jnp.split(jax.nn.sigmoid(g), 2, axis=-1)
    return (g_att * (y_att @ w_att_proj) + g_hy * (y_hy @ w_hy_proj)) @ w_out


def route(h, router_w, router_b):
    T = h.shape[0]
    scores = jax.nn.sigmoid((h @ router_w).astype(jnp.float32))
    biased = (scores + router_b.astype(jnp.float32)).reshape(T, N_GROUPS, EXPERTS_PER_GROUP)
    grp_score = jnp.sum(lax.top_k(biased, 2)[0], axis=-1)
    g_sel = jnp.argmax(grp_score, axis=-1).astype(jnp.int32)
    in_grp = jnp.take_along_axis(biased, g_sel[:, None, None], axis=1)[:, 0]
    _, local = lax.top_k(in_grp, TOP_K)
    idx = g_sel[:, None] * EXPERTS_PER_GROUP + local.astype(jnp.int32)
    w = jnp.take_along_axis(scores, idx, axis=1)
    return idx, w / jnp.sum(w, axis=-1, keepdims=True)


def moe_ffn(h, router_w, router_b, w1, w3, w2):
    T, D = h.shape
    idx, gate = route(h, router_w, router_b)
    A = T * TOP_K
    flat_e = idx.reshape(-1)
    flat_tok = jnp.repeat(jnp.arange(T, dtype=jnp.int32), TOP_K)
    flat_gate = gate.reshape(-1)
    order = jnp.argsort(flat_e)
    e_sorted = flat_e[order]
    counts = jnp.bincount(flat_e, length=N_EXPERTS)
    padded = (counts + MOE_BLOCK - 1) // MOE_BLOCK * MOE_BLOCK
    pad_end = jnp.cumsum(padded)
    pad_start = pad_end - padded
    start = jnp.cumsum(counts) - counts
    dest = pad_start[e_sorted] + (jnp.arange(A, dtype=jnp.int32) - start[e_sorted])
    n_blocks = -(-A // MOE_BLOCK) + N_EXPERTS
    n_slots = n_blocks * MOE_BLOCK
    slot_tok = jnp.full((n_slots,), T, jnp.int32).at[dest].set(flat_tok[order])
    slot_gate = jnp.zeros((n_slots,), jnp.float32).at[dest].set(flat_gate[order])
    block_e = jnp.minimum(jnp.searchsorted(pad_end, jnp.arange(n_blocks) * MOE_BLOCK, side='right'),
                          N_EXPERTS - 1)
    h_pad = jnp.concatenate([h, jnp.zeros((1, D), h.dtype)], axis=0)
    xb = h_pad[slot_tok].reshape(n_blocks, MOE_BLOCK, D)

    def expert_block(args):
        xblk, e = args
        return (jax.nn.silu(xblk @ w1[e]) * (xblk @ w3[e])) @ w2[e]

    yb = lax.map(expert_block, (xb, block_e))
    y = yb.reshape(n_slots, D) * slot_gate[:, None].astype(h.dtype)
    return jnp.zeros((T + 1, D), h.dtype).at[slot_tok].add(y)[:T]


def setup_inputs(seed: int = 0) -> dict:
    key = jax.random.key(seed)
    ks = iter(jax.random.split(key, 32))

    def nrm(shape, scale):
        return jax.random.normal(next(ks), shape, jnp.float32) * scale

    D = D_MODEL
    H = HY_FILTER_HIDDEN
    return {
        'x': nrm((BATCH, SEQ, D), 1.0),
        'c': nrm((BATCH, D), 1.0),
        'ctx': nrm((BATCH, CTX_LEN, D), 1.0),
        'c_ctx': nrm((D,), 1.0),
        'w_ada': nrm((DEPTH, D, 6 * D), D ** -0.5),
        'b_ada': nrm((DEPTH, 6 * D), 0.02),
        'norm1_w': 1.0 + nrm((DEPTH, D), 0.05),
        'norm2_w': 1.0 + nrm((DEPTH, D), 0.05),
        'w_in': nrm((DEPTH, D, IN_WIDTH), D ** -0.5),
        'q_norm_w': 1.0 + nrm((DEPTH, HEAD_DIM), 0.05),
        'k_norm_w': 1.0 + nrm((DEPTH, HEAD_DIM), 0.05),
        'hy_conv_w': nrm((DEPTH, HY_SHORT, 3 * HY_WIDTH), HY_SHORT ** -0.5),
        'hy_conv_b': nrm((DEPTH, 3 * HY_WIDTH), 0.02),
        'hy_pe_w1': nrm((DEPTH, HY_PE_DIM, H), HY_PE_DIM ** -0.5),
        'hy_pe_b1': nrm((DEPTH, H), 0.1),
        'hy_freq1': 1.0 + nrm((DEPTH, H), 0.1),
        'hy_pe_w2': nrm((DEPTH, H, H), H ** -0.5),
        'hy_pe_b2': nrm((DEPTH, H), 0.1),
        'hy_freq2': 1.0 + nrm((DEPTH, H), 0.1),
        'hy_pe_w3': nrm((DEPTH, H, HY_ORDER * 2 * HY_WIDTH), 0.01),
        'hy_skip': nrm((DEPTH, HY_ORDER, HY_WIDTH), 0.5),
        'w_att_proj': nrm((DEPTH, ATT_WIDTH, D), ATT_WIDTH ** -0.5),
        'w_hy_proj': nrm((DEPTH, HY_WIDTH, D), HY_WIDTH ** -0.5),
        'w_out': nrm((DEPTH, D, D), D ** -0.5),
        'router_w': nrm((D, N_EXPERTS), D ** -0.5),
        'router_b': nrm((N_EXPERTS,), 0.01),
        'exp_w1': nrm((DEPTH, N_EXPERTS, D, EXPERT_DIM), D ** -0.5),
        'exp_w3': nrm((DEPTH, N_EXPERTS, D, EXPERT_DIM), D ** -0.5),
        'exp_w2': nrm((DEPTH, N_EXPERTS, EXPERT_DIM, D), EXPERT_DIM ** -0.5),
        'final_norm_w': 1.0 + nrm((D,), 0.05),
    }


def reference(x, c, ctx, c_ctx, w_ada, b_ada, norm1_w, norm2_w, w_in, q_norm_w, k_norm_w,
              hy_conv_w, hy_conv_b, hy_pe_w1, hy_pe_b1, hy_freq1, hy_pe_w2, hy_pe_b2, hy_freq2,
              hy_pe_w3, hy_skip, w_att_proj, w_hy_proj, w_out, router_w, router_b,
              exp_w1, exp_w3, exp_w2, final_norm_w):
    B, S, D = x.shape
    C = ctx.shape[1]
    cos, sin = axial_rope(S)
    for i in range(DEPTH):
        last = i == DEPTH - 1
        mods = jnp.split(jax.nn.silu(c) @ w_ada[i] + b_ada[i], 6, axis=-1)
        sh1, sc1, g1, sh2, sc2, g2 = [m[:, None, :] for m in mods]
        csh1, csc1, cg1, csh2, csc2, cg2 = jnp.split(jax.nn.silu(c_ctx) @ w_ada[i] + b_ada[i], 6, axis=-1)

        hx = modulate(rms_norm(x, norm1_w[i]), sh1, sc1)
        hc = modulate(rms_norm(ctx, norm1_w[i]), csh1, csc1)
        qx, kx, vx, ux, gx = jnp.split(hx @ w_in[i], SPLITS, axis=-1)
        qx = apply_rope(head_norm(qx, q_norm_w[i], N_HEADS), cos, sin)
        kx = apply_rope(head_norm(kx, k_norm_w[i], N_KV_HEADS), cos, sin)
        vx = vx.reshape(B, S, N_KV_HEADS, HEAD_DIM)
        if last:
            kc, vc = jnp.split(hc @ w_in[i][:, ATT_WIDTH:ATT_WIDTH + 2 * KV_WIDTH], 2, axis=-1)
        else:
            qc, kc, vc, uc, gc = jnp.split(hc @ w_in[i], SPLITS, axis=-1)
        kc = head_norm(kc, k_norm_w[i], N_KV_HEADS)
        vc = vc.reshape(B, C, N_KV_HEADS, HEAD_DIM)

        k_all = jnp.concatenate([kx, kc], axis=1)
        v_all = jnp.concatenate([vx, vc], axis=1)
        ya_x = gqa_attention(qx, k_all, v_all, Q_BLOCK)
        filt_x = hyena_filters(S, hy_pe_w1[i], hy_pe_b1[i], hy_freq1[i], hy_pe_w2[i], hy_pe_b2[i],
                               hy_freq2[i], hy_pe_w3[i])
        yh_x = hyena_operator(ux, filt_x, hy_conv_w[i], hy_conv_b[i], hy_skip[i])
        x = x + g1 * merge_branches(ya_x, yh_x, gx, w_att_proj[i], w_hy_proj[i], w_out[i])
        hx2 = modulate(rms_norm(x, norm2_w[i]), sh2, sc2)

        if last:
            f = moe_ffn(hx2.reshape(B * S, D), router_w, router_b, exp_w1[i], exp_w3[i], exp_w2[i])
            x = x + g2 * f.reshape(B, S, D)
        else:
            qc = head_norm(qc, q_norm_w[i], N_HEADS)
            ya_c = gqa_attention(qc, kc, vc, C)
            filt_c = hyena_filters(C, hy_pe_w1[i], hy_pe_b1[i], hy_freq1[i], hy_pe_w2[i], hy_pe_b2[i],
                                   hy_freq2[i], hy_pe_w3[i])
            yh_c = hyena_operator(uc, filt_c, hy_conv_w[i], hy_conv_b[i], hy_skip[i])
            ctx = ctx + cg1 * merge_branches(ya_c, yh_c, gc, w_att_proj[i], w_hy_proj[i], w_out[i])
            hc2 = modulate(rms_norm(ctx, norm2_w[i]), csh2, csc2)
            tokens = jnp.concatenate([hx2.reshape(B * S, D), hc2.reshape(B * C, D)], axis=0)
            f = moe_ffn(tokens, router_w, router_b, exp_w1[i], exp_w3[i], exp_w2[i])
            x = x + g2 * f[:B * S].reshape(B, S, D)
            ctx = ctx + cg2 * f[B * S:].reshape(B, C, D)
    return rms_norm(x, final_norm_w)
```

```python
import functools
import math

import numpy as np
import jax
import jax.numpy as jnp
from jax import lax
from jax.experimental import pallas as pl
from jax.experimental.pallas import tpu as pltpu

F32 = jnp.float32
BF16 = jnp.bfloat16

D_MODEL = 1024
DEPTH = 2
GRID_W = 64
NORM_EPS = 1e-6
N_HEADS = 8
N_KV_HEADS = 2
HEAD_DIM = 128
ATT_WIDTH = N_HEADS * HEAD_DIM
KV_WIDTH = N_KV_HEADS * HEAD_DIM
ROPE_THETA = 10000.0
ATT_SCALE = HEAD_DIM ** -0.5
HY_WIDTH = D_MODEL // 2
HY_ORDER = 2
HY_FILTER_HIDDEN = 64
HY_BANDS = 16
HY_PE_DIM = 1 + 2 * HY_BANDS
HY_PE_PAD = 128
HY_FAST_DECAY = 0.3
HY_SLOW_DECAY = 1.5
HY_DECAY_TARGET = 1e-2
N_EXPERTS = 64
N_GROUPS = 8
EXPERTS_PER_GROUP = N_EXPERTS // N_GROUPS
TOP_K = 2
EXPERT_DIM = 512
IN_WIDTH = ATT_WIDTH + 2 * KV_WIDTH + 3 * HY_WIDTH + 2 * D_MODEL
COL_Q = 0
COL_K = ATT_WIDTH
COL_V = ATT_WIDTH + KV_WIDTH
COL_U = ATT_WIDTH + 2 * KV_WIDTH
COL_G = COL_U + 3 * HY_WIDTH

MXU_COLS = 256
ROW_TILE = 512
ATT_Q_TILE = 256
DFT_ROWS = 64
MOE_ROWS = 256
VMEM_LIMIT = 56 * 1024 * 1024


def _dot(a, b):
    return jnp.dot(a, b, preferred_element_type=F32)


def _split(a):
    hi = a.astype(BF16)
    lo = (a - hi.astype(F32)).astype(BF16)
    return hi, lo


def _dot3(a, b):
    ah, al = _split(a)
    bh, bl = _split(b)
    return _dot(ah, bh) + (_dot(al, bh) + _dot(ah, bl))


def _rms(t):
    return t * lax.rsqrt(jnp.mean(t * t, axis=-1, keepdims=True) + NORM_EPS)


def _params(*sem):
    return pltpu.CompilerParams(dimension_semantics=sem, vmem_limit_bytes=VMEM_LIMIT)


def _adaln_kernel(c_ref, w_ref, b_ref, o_ref):
    c = c_ref[...]
    o_ref[...] = _dot3(c * jax.nn.sigmoid(c), w_ref[...]) + b_ref[...]


def _adaln(cs, w_ada, b_ada):
    depth, d, n = w_ada.shape
    rows = cs.shape[0]
    tn = 1536
    return pl.pallas_call(
        _adaln_kernel,
        grid=(depth, n // tn),
        in_specs=[pl.BlockSpec((rows, d), lambda l, j: (0, 0)),
                  pl.BlockSpec((None, d, tn), lambda l, j: (l, 0, j)),
                  pl.BlockSpec((None, 1, tn), lambda l, j: (l, 0, j))],
        out_specs=pl.BlockSpec((None, rows, tn), lambda l, j: (l, 0, j)),
        out_shape=jax.ShapeDtypeStruct((depth, rows, n), F32),
        compiler_params=_params("parallel", "parallel"),
        name="adaln",
    )(cs, w_ada, b_ada.reshape(depth, 1, n))


def _inproj_kernel(x_ref, sh_ref, sc_ref, nw_ref, w_ref, qn_ref, kn_ref, cos_ref, sin_ref,
                   *outs, sections):
    h = _rms(x_ref[...]) * nw_ref[...]
    h = h * (1.0 + sc_ref[...]) + sh_ref[...]
    hb = h.astype(BF16)
    cos = cos_ref[...]
    sin = sin_ref[...]
    o = dict(zip(sections, outs))

    def head_cols(col0, n_cols, norm_w, scale, out_ref):
        for c in range(n_cols // MXU_COLS):
            acc = _dot(hb, w_ref[:, col0 + c * MXU_COLS:col0 + (c + 1) * MXU_COLS])
            for j in range(MXU_COLS // HEAD_DIM):
                t = _rms(acc[:, j * HEAD_DIM:(j + 1) * HEAD_DIM]) * norm_w
                t = t * cos + pltpu.roll(t, HEAD_DIM // 2, 1) * sin
                lo = c * MXU_COLS + j * HEAD_DIM
                out_ref[:, lo:lo + HEAD_DIM] = (t * scale).astype(out_ref.dtype)

    if "q" in o:
        head_cols(COL_Q, ATT_WIDTH, qn_ref[...], ATT_SCALE, o["q"])
    if "k" in o:
        head_cols(COL_K, KV_WIDTH, kn_ref[...], 1.0, o["k"])
    if "v" in o:
        o["v"][...] = _dot(hb, w_ref[:, COL_V:COL_V + KV_WIDTH]).astype(BF16)
    if "u" in o:
        for c in range(3):
            o["u"][:, c * HY_WIDTH:(c + 1) * HY_WIDTH] = _dot(
                hb, w_ref[:, COL_U + c * HY_WIDTH:COL_U + (c + 1) * HY_WIDTH])
    if "g" in o:
        for c in range(4):
            acc = _dot(hb, w_ref[:, COL_G + c * 512:COL_G + (c + 1) * 512])
            o["g"][:, c * 512:(c + 1) * 512] = jax.nn.sigmoid(acc).astype(BF16)


_SECTION_SHAPES = {"q": (ATT_WIDTH, BF16), "k": (KV_WIDTH, BF16), "v": (KV_WIDTH, BF16),
                   "u": (3 * HY_WIDTH, F32), "g": (2 * D_MODEL, BF16)}


def _inproj(x2d, seq_len, sh, sc, nw, w_bf16, qn, kn, cos2, sin2, sections):
    rows, d = x2d.shape
    tm = min(ROW_TILE, seq_len)
    tiles_per_seq = seq_len // tm
    nb = sh.shape[0]
    mod_idx = (lambda i: (i // tiles_per_seq, 0, 0)) if nb > 1 else (lambda i: (0, 0, 0))
    const2 = lambda i: (0, 0)
    out_shape = [jax.ShapeDtypeStruct((rows, _SECTION_SHAPES[s][0]), _SECTION_SHAPES[s][1])
                 for s in sections]
    out_specs = [pl.BlockSpec((tm, _SECTION_SHAPES[s][0]), lambda i: (i, 0)) for s in sections]
    return pl.pallas_call(
        functools.partial(_inproj_kernel, sections=tuple(sections)),
        grid=(rows // tm,),
        in_specs=[pl.BlockSpec((tm, d), lambda i: (i, 0)),
                  pl.BlockSpec((None, 1, d), mod_idx),
                  pl.BlockSpec((None, 1, d), mod_idx),
                  pl.BlockSpec((1, d), const2),
                  pl.BlockSpec((d, IN_WIDTH), const2),
                  pl.BlockSpec((1, HEAD_DIM), const2),
                  pl.BlockSpec((1, HEAD_DIM), const2),
                  pl.BlockSpec((tm, HEAD_DIM), lambda i: (i % tiles_per_seq, 0)),
                  pl.BlockSpec((tm, HEAD_DIM), lambda i: (i % tiles_per_seq, 0))],
        out_specs=out_specs,
        out_shape=out_shape,
        compiler_params=_params("parallel"),
        name="inproj",
    )(x2d, sh, sc, nw, w_bf16, qn, kn, cos2, sin2)


def _attn_kernel(q_ref, *refs, n_kv_sets):
    kv = refs[:2 * n_kv_sets]
    o_ref = refs[2 * n_kv_sets]
    ks = [kv[2 * s][...] for s in range(n_kv_sets)]
    vs = [kv[2 * s + 1][...] for s in range(n_kv_sets)]
    nt = (((1,), (1,)), ((), ()))
    for g in range(N_HEADS // N_KV_HEADS):
        q = q_ref[:, g * HEAD_DIM:(g + 1) * HEAD_DIM]
        ss = [lax.dot_general(q, k, nt, preferred_element_type=F32) for k in ks]
        m = jnp.max(ss[0], axis=1, keepdims=True)
        for s in ss[1:]:
            m = jnp.maximum(m, jnp.max(s, axis=1, keepdims=True))
        l = None
        acc = None
        for s, v in zip(ss, vs):
            p = jnp.exp(s - m)
            ls = jnp.sum(p, axis=1, keepdims=True)
            a = _dot(p.astype(BF16), v)
            l = ls if l is None else l + ls
            acc = a if acc is None else acc + a
        o_ref[:, g * HEAD_DIM:(g + 1) * HEAD_DIM] = (acc * (1.0 / l)).astype(o_ref.dtype)


def _attention(q, kv_sets, batch, seq_len):
    tq = min(ATT_Q_TILE, seq_len)
    nq = seq_len // tq
    grp = (N_HEADS // N_KV_HEADS) * HEAD_DIM
    in_specs = [pl.BlockSpec((tq, grp), lambda b, j, i: (b * nq + i, j))]
    args = [q]
    for k, v, lk in kv_sets:
        in_specs += [pl.BlockSpec((lk, HEAD_DIM), lambda b, j, i: (b, j))] * 2
        args += [k, v]
    return pl.pallas_call(
        functools.partial(_attn_kernel, n_kv_sets=len(kv_sets)),
        grid=(batch, N_KV_HEADS, nq),
        in_specs=in_specs,
        out_specs=pl.BlockSpec((tq, grp), lambda b, j, i: (b * nq + i, j)),
        out_shape=jax.ShapeDtypeStruct(q.shape, BF16),
        compiler_params=_params("parallel", "parallel", "arbitrary"),
        name="attention",
    )(*args)


def _hy_filter_kernel(z_ref, t_ref, w1_ref, b1_ref, f1_ref, w2_ref, b2_ref, f2_ref, w3_ref,
                      dl_ref, sp_ref, sm_ref, nyq_ref):
    i = pl.program_id(0)
    tl = z_ref.shape[0]
    h = jnp.sin(f1_ref[...] * (_dot3(z_ref[...], w1_ref[...]) + b1_ref[...]))
    h = jnp.sin(f2_ref[...] * (_dot3(h, w2_ref[...]) + b2_ref[...]))
    h = _dot3(h, w3_ref[...])
    window = jnp.exp(-t_ref[...] * dl_ref[...])
    row = i * tl + lax.broadcasted_iota(jnp.int32, (tl, HY_WIDTH), 0)
    alt = (1 - 2 * (row & 1)).astype(F32)

    @pl.when(i == 0)
    def _():
        nyq_ref[...] = jnp.zeros_like(nyq_ref)

    for o in range(HY_ORDER):
        hf = h[:, (2 * o) * HY_WIDTH:(2 * o + 1) * HY_WIDTH] * window
        hb = h[:, (2 * o + 1) * HY_WIDTH:(2 * o + 2) * HY_WIDTH] * window
        hb = jnp.where(row == 0, 0.0, hb)
        plus = hf + hb
        sp_ref[:, o * HY_WIDTH:(o + 1) * HY_WIDTH] = plus.astype(BF16)
        sm_ref[:, o * HY_WIDTH:(o + 1) * HY_WIDTH] = (hf - hb).astype(BF16)
        nyq_ref[:, o * HY_WIDTH:(o + 1) * HY_WIDTH] += jnp.sum(alt * plus, axis=0, keepdims=True)


def _hy_filter(seq_len, pe_w1, pe_b1, freq1, pe_w2, pe_b2, freq2, pe_w3):
    t01 = np.linspace(0.0, 1.0, seq_len)[:, None]
    pos = np.arange(seq_len, dtype=np.float64)[:, None]
    bands = np.linspace(1e-4, HY_BANDS - 1, HY_BANDS)[None, :]
    f = 2.0 * math.pi * pos * bands / seq_len
    z = np.zeros((seq_len, HY_PE_PAD), np.float32)
    z[:, :HY_PE_DIM] = np.concatenate([t01, np.cos(f), -np.sin(f)], axis=-1)
    max_decay = math.log(HY_DECAY_TARGET) / HY_FAST_DECAY
    min_decay = math.log(HY_DECAY_TARGET) / HY_SLOW_DECAY
    deltas = np.abs(np.linspace(min_decay, max_decay, HY_WIDTH))[None, :].astype(np.float32)
    w1p = jnp.zeros((HY_PE_PAD, HY_FILTER_HIDDEN), F32).at[:HY_PE_DIM].set(pe_w1)
    tl = min(512, seq_len)
    hid = HY_FILTER_HIDDEN
    n_out = HY_ORDER * HY_WIDTH
    c2 = lambda i: (0, 0)
    return pl.pallas_call(
        _hy_filter_kernel,
        grid=(seq_len // tl,),
        in_specs=[pl.BlockSpec((tl, HY_PE_PAD), lambda i: (i, 0)),
                  pl.BlockSpec((tl, 1), lambda i: (i, 0)),
                  pl.BlockSpec((HY_PE_PAD, hid), c2), pl.BlockSpec((1, hid), c2),
                  pl.BlockSpec((1, hid), c2), pl.BlockSpec((hid, hid), c2),
                  pl.BlockSpec((1, hid), c2), pl.BlockSpec((1, hid), c2),
                  pl.BlockSpec((hid, 2 * n_out), c2), pl.BlockSpec((1, HY_WIDTH), c2)],
        out_specs=[pl.BlockSpec((tl, n_out), lambda i: (i, 0)),
                   pl.BlockSpec((tl, n_out), lambda i: (i, 0)),
                   pl.BlockSpec((1, n_out), c2)],
        out_shape=[jax.ShapeDtypeStruct((seq_len, n_out), BF16),
                   jax.ShapeDtypeStruct((seq_len, n_out), BF16),
                   jax.ShapeDtypeStruct((1, n_out), F32)],
        compiler_params=_params("arbitrary"),
        name="hy_filter",
    )(jnp.asarray(z), jnp.asarray(t01.astype(np.float32)), w1p, pe_b1.reshape(1, hid),
      freq1.reshape(1, hid), pe_w2, pe_b2.reshape(1, hid), freq2.reshape(1, hid), pe_w3,
      jnp.asarray(deltas))


def _dft_kernel(ca_ref, sa_ref, cb_ref, sb_ref, alt_ref, wc_ref, cm_ref, sm_ref, ci_ref, si_ref):
    i = pl.program_id(0)
    ca = ca_ref[...]
    sa = sa_ref[...]
    cb = cb_ref[...]
    sb = sb_ref[...]
    c = ca * cb - sa * sb
    s = sa * cb + ca * sb
    rows = lax.broadcasted_iota(jnp.int32, c.shape, 0)
    cols = lax.broadcasted_iota(jnp.int32, c.shape, 1)
    wc = wc_ref[...]
    cm_ref[...] = c.astype(BF16)
    sm_ref[...] = jnp.where((rows == 0) & (i == 0), alt_ref[...], s).astype(BF16)
    alt_t = (1 - 2 * (rows & 1)).astype(F32)
    ci_ref[...] = (c * wc).astype(BF16)
    si_ref[...] = (jnp.where(cols == 0, alt_t, s) * wc).astype(BF16)


def _dft_matrices(seq_len):
    n = 2 * seq_len
    idx = np.arange(seq_len, dtype=np.int64)[None, :]
    r1 = np.arange(seq_len // DFT_ROWS, dtype=np.int64)[:, None] * DFT_ROWS
    r0 = np.arange(DFT_ROWS, dtype=np.int64)[:, None]
    ang_a = ((r1 * idx) % n).astype(np.float64) * (2.0 * math.pi / n)
    ang_b = ((r0 * idx) % n).astype(np.float64) * (2.0 * math.pi / n)
    tab = lambda a: jnp.asarray(a.astype(np.float32))
    ca = tab(np.cos(ang_a)).reshape(-1, 1, seq_len)
    sa = tab(np.sin(ang_a)).reshape(-1, 1, seq_len)
    alt = tab(1.0 - 2.0 * (idx % 2))
    wc = tab(np.where(idx == 0, 1.0, 2.0) / n)
    row_blk = pl.BlockSpec((None, 1, seq_len), lambda i: (i, 0, 0))
    full = lambda r: pl.BlockSpec((r, seq_len), lambda i: (0, 0))
    out_blk = pl.BlockSpec((DFT_ROWS, seq_len), lambda i: (i, 0))
    return pl.pallas_call(
        _dft_kernel,
        grid=(seq_len // DFT_ROWS,),
        in_specs=[row_blk, row_blk, full(DFT_ROWS), full(DFT_ROWS), full(1), full(1)],
        out_specs=[out_blk] * 4,
        out_shape=[jax.ShapeDtypeStruct((seq_len, seq_len), BF16)] * 4,
        compiler_params=_params("parallel"),
        name="dft_matrices",
    )(ca, sa, tab(np.cos(ang_b)), tab(np.sin(ang_b)), alt, wc)


def _hy_spec_kernel(cm_ref, sm_ref, sp_ref, smn_ref, nyq_ref, kp_ref, kq_ref):
    i = pl.program_id(1)
    kp_ref[...] = _dot(cm_ref[...], sp_ref[...])
    kq = _dot(sm_ref[...], smn_ref[...])
    rows = lax.broadcasted_iota(jnp.int32, kq.shape, 0)
    kq_ref[...] = jnp.where((rows == 0) & (i == 0), nyq_ref[...], kq)


def _hy_spectrum(cm, sm, splus, sminus, nyq):
    seq_len = cm.shape[0]
    tf = min(512, seq_len)
    mat = pl.BlockSpec((tf, seq_len), lambda o, i: (i, 0))
    sig = pl.BlockSpec((seq_len, HY_WIDTH), lambda o, i: (0, o))
    out = pl.BlockSpec((tf, HY_WIDTH), lambda o, i: (i, o))
    return pl.pallas_call(
        _hy_spec_kernel,
        grid=(HY_ORDER, seq_len // tf),
        in_specs=[mat, mat, sig, sig, pl.BlockSpec((1, HY_WIDTH), lambda o, i: (0, o))],
        out_specs=[out, out],
        out_shape=[jax.ShapeDtypeStruct((seq_len, HY_ORDER * HY_WIDTH), F32)] * 2,
        compiler_params=_params("parallel", "arbitrary"),
        name="hy_spectrum",
    )(cm, sm, splus, sminus, nyq)


def _hy_sconv_kernel(u_ref, w_ref, b_ref, uc_ref, vb_ref, *, v_first_block):
    j = pl.program_id(1)
    x = u_ref[...]
    n = x.shape[0]
    rows = lax.broadcasted_iota(jnp.int32, x.shape, 0)
    prev = jnp.where(rows == 0, 0.0, pltpu.roll(x, 1, 0))
    nxt = jnp.where(rows == n - 1, 0.0, pltpu.roll(x, n - 1, 0))
    y = prev * w_ref[0:1, :] + x * w_ref[1:2, :] + nxt * w_ref[2:3, :] + b_ref[...]
    uc_ref[...] = y

    @pl.when(j >= v_first_block)
    def _():
        vb_ref[...] = y.astype(BF16)


def _hy_sconv(u, batch, seq_len, conv_w, conv_b):
    ch = u.shape[1]
    tc = 256
    v0 = (2 * HY_WIDTH) // tc
    u3 = u.reshape(batch, seq_len, ch)
    uc, vb = pl.pallas_call(
        functools.partial(_hy_sconv_kernel, v_first_block=v0),
        grid=(batch, ch // tc),
        in_specs=[pl.BlockSpec((None, seq_len, tc), lambda b, j: (b, 0, j)),
                  pl.BlockSpec((3, tc), lambda b, j: (0, j)),
                  pl.BlockSpec((1, tc), lambda b, j: (0, j))],
        out_specs=[pl.BlockSpec((None, seq_len, tc), lambda b, j: (b, 0, j)),
                   pl.BlockSpec((None, seq_len, tc), lambda b, j: (b, 0, jnp.maximum(j - v0, 0)))],
        out_shape=[jax.ShapeDtypeStruct((batch, seq_len, ch), F32),
                   jax.ShapeDtypeStruct((batch, seq_len, HY_WIDTH), BF16)],
        compiler_params=_params("parallel", "arbitrary"),
        name="hy_sconv",
    )(u3, conv_w, conv_b.reshape(1, ch))
    return uc, vb


def _hy_fwd_kernel(cm_ref, sm_ref, v_ref, kp_ref, kq_ref, yp_ref, yq_ref):
    i = pl.program_id(0)
    v = v_ref[...]
    vp = _dot(cm_ref[...], v)
    vq = _dot(sm_ref[...], v)
    kp = kp_ref[...]
    kq = kq_ref[...]
    rows = lax.broadcasted_iota(jnp.int32, vp.shape, 0)
    edge = (rows == 0) & (i == 0)
    t = vq * kq
    yp_ref[...] = (vp * kp - jnp.where(edge, 0.0, t)).astype(BF16)
    yq_ref[...] = jnp.where(edge, t, vp * kq + vq * kp).astype(BF16)


def _hy_forward(cm, sm, vb, kp, kq, order):
    batch, seq_len, _ = vb.shape
    tf = min(512, seq_len)
    mat = pl.BlockSpec((tf, seq_len), lambda i, b: (i, 0))
    spec = pl.BlockSpec((tf, HY_WIDTH), lambda i, b: (i, order))
    out = pl.BlockSpec((None, tf, HY_WIDTH), lambda i, b: (b, i, 0))
    return pl.pallas_call(
        _hy_fwd_kernel,
        grid=(seq_len // tf, batch),
        in_specs=[mat, mat, pl.BlockSpec((None, seq_len, HY_WIDTH), lambda i, b: (b, 0, 0)),
                  spec, spec],
        out_specs=[out, out],
        out_shape=[jax.ShapeDtypeStruct((batch, seq_len, HY_WIDTH), BF16)] * 2,
        compiler_params=_params("parallel", "arbitrary"),
        name="hy_forward",
    )(cm, sm, vb, kp, kq)


def _hy_inv_kernel(ci_ref, si_ref, yp_ref, yq_ref, a_ref, c_ref, skip_ref, *outs):
    y = _dot(ci_ref[...], yp_ref[...]) + _dot(si_ref[...], yq_ref[...])
    z = a_ref[...] * (y + skip_ref[...] * c_ref[...])
    for o_ref in outs:
        o_ref[...] = z.astype(o_ref.dtype)


def _hy_inverse(ci, si, yp, yq, a_arr, a_blk, c_arr, c_blk, skip, out_dtypes):
    batch, seq_len, _ = yp.shape
    tt = min(512, seq_len)
    mat = pl.BlockSpec((tt, seq_len), lambda i, b: (i, 0))
    sig = pl.BlockSpec((None, seq_len, HY_WIDTH), lambda i, b: (b, 0, 0))
    out = pl.BlockSpec((None, tt, HY_WIDTH), lambda i, b: (b, i, 0))
    return pl.pallas_call(
        _hy_inv_kernel,
        grid=(seq_len // tt, batch),
        in_specs=[mat, mat, sig, sig,
                  pl.BlockSpec((None, tt, HY_WIDTH), lambda i, b: (b, i, a_blk)),
                  pl.BlockSpec((None, tt, HY_WIDTH), lambda i, b: (b, i, c_blk)),
                  pl.BlockSpec((1, HY_WIDTH), lambda i, b: (0, 0))],
        out_specs=[out] * len(out_dtypes),
        out_shape=[jax.ShapeDtypeStruct((batch, seq_len, HY_WIDTH), dt) for dt in out_dtypes],
        compiler_params=_params("parallel", "arbitrary"),
        name="hy_inverse",
    )(ci, si, yp, yq, a_arr, c_arr, skip)


def _hyena(u, batch, seq_len, dft, kp, kq, conv_w, conv_b, skip):
    cm, sm, ci, si = dft
    uc, vb = _hy_sconv(u, batch, seq_len, conv_w, conv_b)
    yp, yq = _hy_forward(cm, sm, vb, kp, kq, 0)
    z, zb = _hy_inverse(ci, si, yp, yq, uc, 0, uc, 2, skip[0:1], (F32, BF16))
    yp, yq = _hy_forward(cm, sm, zb, kp, kq, 1)
    (y,) = _hy_inverse(ci, si, yp, yq, uc, 1, z, 0, skip[1:2], (BF16,))
    return y.reshape(batch * seq_len, HY_WIDTH)


def _merge_kernel(ya_ref, yh_ref, g_ref, x_ref, g1_ref, sh_ref, sc_ref, nw_ref,
                  wa_ref, wh_ref, wo_ref, rh_ref, rl_ref, *rest):
    xo_ref, hx_ref, lg_ref = rest[-3:]
    a = _dot(ya_ref[...], wa_ref[...])
    h = _dot(yh_ref[...], wh_ref[...])
    m = g_ref[:, :D_MODEL].astype(F32) * a + g_ref[:, D_MODEL:].astype(F32) * h
    xn = x_ref[...] + g1_ref[...] * _dot(m.astype(BF16), wo_ref[...])
    xo_ref[...] = xn
    hx = _rms(xn) * nw_ref[...]
    hx = hx * (1.0 + sc_ref[...]) + sh_ref[...]
    hx_ref[...] = hx
    hh, hl = _split(hx)
    lg_ref[...] = _dot(hh, rh_ref[...]) + (_dot(hl, rh_ref[...]) + _dot(hh, rl_ref[...]))


def _merge(ya, yh, g, x2d, seq_len, g1, sh2, sc2, nw, wa, wh, wo, r_hi, r_lo,
           total_rows, row_offset, prev=None):
    rows, d = x2d.shape
    tm = min(ROW_TILE, seq_len)
    tiles_per_seq = seq_len // tm
    off = row_offset // tm
    nb = g1.shape[0]
    mod_idx = (lambda i: (i // tiles_per_seq, 0, 0)) if nb > 1 else (lambda i: (0, 0, 0))
    c2 = lambda i: (0, 0)
    row = lambda w: pl.BlockSpec((tm, w), lambda i: (i, 0))
    in_specs = [row(ATT_WIDTH), row(HY_WIDTH), row(2 * d), row(d),
                pl.BlockSpec((None, 1, d), mod_idx), pl.BlockSpec((None, 1, d), mod_idx),
                pl.BlockSpec((None, 1, d), mod_idx), pl.BlockSpec((1, d), c2),
                pl.BlockSpec((ATT_WIDTH, d), c2), pl.BlockSpec((HY_WIDTH, d), c2),
                pl.BlockSpec((d, d), c2), pl.BlockSpec((d, N_EXPERTS), c2),
                pl.BlockSpec((d, N_EXPERTS), c2)]
    args = [ya, yh, g, x2d, g1, sh2, sc2, nw, wa, wh, wo, r_hi, r_lo]
    aliases = {}
    if prev is not None:
        in_specs += [pl.BlockSpec(memory_space=pl.ANY)] * 2
        aliases = {len(args): 1, len(args) + 1: 2}
        args += list(prev)
    return pl.pallas_call(
        _merge_kernel,
        grid=(rows // tm,),
        in_specs=in_specs,
        out_specs=[row(d), pl.BlockSpec((tm, d), lambda i: (i + off, 0)),
                   pl.BlockSpec((tm, N_EXPERTS), lambda i: (i + off, 0))],
        out_shape=[jax.ShapeDtypeStruct((rows, d), F32),
                   jax.ShapeDtypeStruct((total_rows, d), F32),
                   jax.ShapeDtypeStruct((total_rows, N_EXPERTS), F32)],
        input_output_aliases=aliases,
        compiler_params=_params("arbitrary"),
        name="merge",
    )(*args)


def _route(logits, router_b):
    t = logits.shape[0]
    scores = jax.nn.sigmoid(logits)
    biased = (scores + router_b.astype(F32)).reshape(t, N_GROUPS, EXPERTS_PER_GROUP)
    grp_score = jnp.sum(lax.top_k(biased, 2)[0], axis=-1)
    g_sel = jnp.argmax(grp_score, axis=-1).astype(jnp.int32)
    in_grp = jnp.take_along_axis(biased, g_sel[:, None, None], axis=1)[:, 0]
    _, local = lax.top_k(in_grp, TOP_K)
    idx = g_sel[:, None] * EXPERTS_PER_GROUP + local.astype(jnp.int32)
    w = jnp.take_along_axis(scores, idx, axis=1)
    return idx, w / jnp.sum(w, axis=-1, keepdims=True)


def _dispatch(idx, gate):
    t = idx.shape[0]
    a = t * TOP_K
    n_blocks = -(-a // MOE_ROWS) + N_EXPERTS
    n_slots = n_blocks * MOE_ROWS
    flat_e = idx.reshape(-1)
    onehot = (flat_e[:, None] == jnp.arange(N_EXPERTS, dtype=jnp.int32)[None, :]).astype(jnp.int32)
    csum = jnp.cumsum(onehot, axis=0)
    rank = jnp.take_along_axis(csum, flat_e[:, None], axis=1)[:, 0] - 1
    counts = csum[-1]
    padded = (counts + MOE_ROWS - 1) // MOE_ROWS * MOE_ROWS
    pad_end = jnp.cumsum(padded)
    pad_start = pad_end - padded
    dest = (pad_start[flat_e] + rank).astype(jnp.int32)
    flat_tok = jnp.repeat(jnp.arange(t, dtype=jnp.int32), TOP_K)
    slot_tok = jnp.zeros((n_slots,), jnp.int32).at[dest].set(flat_tok)
    slot_gate = jnp.zeros((n_slots,), F32).at[dest].set(gate.reshape(-1))
    block_e = jnp.minimum(jnp.searchsorted(pad_end, jnp.arange(n_blocks) * MOE_ROWS, side="right"),
                          N_EXPERTS - 1).astype(jnp.int32)
    n_used = (pad_end[-1] // MOE_ROWS).astype(jnp.int32).reshape(1)
    return slot_tok, slot_gate, block_e, n_used, dest, n_blocks


def _row_copy(src_hbm, dst_hbm, sem, src_row, dst_row):
    return pltpu.make_async_copy(src_hbm.at[pl.ds(src_row, 1)], dst_hbm.at[pl.ds(dst_row, 1)], sem)


def _gather_kernel(tok_ref, nused_ref, h_hbm, o_hbm, sems):
    i = pl.program_id(0)
    last = pl.num_programs(0) - 1
    n_used = nused_ref[0]

    def wait_block(blk):
        pltpu.make_async_copy(h_hbm.at[pl.ds(0, MOE_ROWS)],
                              o_hbm.at[pl.ds(blk * MOE_ROWS, MOE_ROWS)],
                              sems.at[blk % 2]).wait()

    @pl.when(i < n_used)
    def _():
        def issue(r, carry):
            slot = i * MOE_ROWS + r
            _row_copy(h_hbm, o_hbm, sems.at[i % 2], tok_ref[slot], slot).start()
            return carry
        lax.fori_loop(0, MOE_ROWS, issue, 0)

    @pl.when((i >= 1) & (i - 1 < n_used))
    def _():
        wait_block(i - 1)

    @pl.when((i == last) & (i < n_used))
    def _():
        wait_block(i)


def _gather_rows(slot_tok, n_used, h, n_blocks):
    d = h.shape[1]
    return pl.pallas_call(
        _gather_kernel,
        grid_spec=pltpu.PrefetchScalarGridSpec(
            num_scalar_prefetch=2,
            grid=(n_blocks,),
            in_specs=[pl.BlockSpec(memory_space=pl.ANY)],
            out_specs=pl.BlockSpec(memory_space=pl.ANY),
            scratch_shapes=[pltpu.SemaphoreType.DMA((2,))]),
        out_shape=jax.ShapeDtypeStruct((n_blocks * MOE_ROWS, d), h.dtype),
        compiler_params=pltpu.CompilerParams(dimension_semantics=("arbitrary",),
                                             has_side_effects=True),
        name="moe_gather",
    )(slot_tok, n_used, h)


def _expert_kernel(be_ref, nused_ref, x_ref, gate_ref, w1_ref, w3_ref, w2_ref, y_ref):
    @pl.when(pl.program_id(0) < nused_ref[0])
    def _():
        x = x_ref[...].astype(BF16)
        a = _dot(x, w1_ref[...].astype(BF16))
        b = _dot(x, w3_ref[...].astype(BF16))
        h = (a * jax.nn.sigmoid(a)) * b
        y_ref[...] = _dot(h.astype(BF16), w2_ref[...].astype(BF16)) * gate_ref[...]


def _experts(block_e, n_used, xs, slot_gate, w1, w3, w2):
    n_slots, d = xs.shape
    n_blocks = n_slots // MOE_ROWS
    e_dim = w1.shape[2]
    blk = lambda i, be, nu: (jnp.minimum(i, nu[0] - 1), 0)
    wsel = lambda i, be, nu: (be[jnp.minimum(i, nu[0] - 1)], 0, 0)
    return pl.pallas_call(
        _expert_kernel,
        grid_spec=pltpu.PrefetchScalarGridSpec(
            num_scalar_prefetch=2,
            grid=(n_blocks,),
            in_specs=[pl.BlockSpec((MOE_ROWS, d), blk),
                      pl.BlockSpec((MOE_ROWS, 1), blk),
                      pl.BlockSpec((None, d, e_dim), wsel),
                      pl.BlockSpec((None, d, e_dim), wsel),
                      pl.BlockSpec((None, e_dim, d), wsel)],
            out_specs=pl.BlockSpec((MOE_ROWS, d), blk)),
        out_shape=jax.ShapeDtypeStruct((n_slots, d), F32),
        compiler_params=_params("arbitrary"),
        name="moe_experts",
    )(block_e, n_used, xs, slot_gate.reshape(n_slots, 1), w1, w3, w2)


def _combine_kernel(pos_ref, x_ref, g2_ref, fw_ref, y_hbm, o_ref, buf, sems, *, tok_offset,
                    final_norm):
    i = pl.program_id(0)
    n = pl.num_programs(0)
    tm = x_ref.shape[0]

    def issue(step, slot):
        def body(r, carry):
            base = (tok_offset + step * tm + r) * TOP_K
            for k in range(TOP_K):
                pltpu.make_async_copy(y_hbm.at[pl.ds(pos_ref[base + k], 1)],
                                      buf.at[slot, k, pl.ds(r, 1)], sems.at[slot]).start()
            return carry
        lax.fori_loop(0, tm, body, 0)

    @pl.when(i == 0)
    def _():
        issue(0, 0)

    @pl.when(i + 1 < n)
    def _():
        issue(i + 1, (i + 1) % 2)

    slot = i % 2
    pltpu.make_async_copy(buf.at[slot], buf.at[slot], sems.at[slot]).wait()
    f = buf[slot, 0] + buf[slot, 1]
    out = x_ref[...] + g2_ref[...] * f
    if final_norm:
        out = _rms(out) * fw_ref[...]
    o_ref[...] = out


def _combine(pos, x2d, seq_len, g2, fw, y, tok_offset, final_norm):
    rows, d = x2d.shape
    tm = min(256, seq_len)
    tiles_per_seq = seq_len // tm
    nb = g2.shape[0]
    mod_idx = ((lambda i, p: (i // tiles_per_seq, 0, 0)) if nb > 1 else (lambda i, p: (0, 0, 0)))
    return pl.pallas_call(
        functools.partial(_combine_kernel, tok_offset=tok_offset, final_norm=final_norm),
        grid_spec=pltpu.PrefetchScalarGridSpec(
            num_scalar_prefetch=1,
            grid=(rows // tm,),
            in_specs=[pl.BlockSpec((tm, d), lambda i, p: (i, 0)),
                      pl.BlockSpec((None, 1, d), mod_idx),
                      pl.BlockSpec((1, d), lambda i, p: (0, 0)),
                      pl.BlockSpec(memory_space=pl.ANY)],
            out_specs=pl.BlockSpec((tm, d), lambda i, p: (i, 0)),
            scratch_shapes=[pltpu.VMEM((2, TOP_K, tm, d), F32),
                            pltpu.SemaphoreType.DMA((2,))]),
        out_shape=jax.ShapeDtypeStruct((rows, d), F32),
        compiler_params=_params("arbitrary"),
        name="moe_combine",
    )(pos, x2d, g2, fw, y)


def _rope_tables(seq_len):
    rows = seq_len // GRID_W
    row = np.repeat(np.arange(rows), GRID_W).astype(np.float64)
    col = np.tile(np.arange(GRID_W), rows).astype(np.float64)
    n = HEAD_DIM // 4
    inv = (ROPE_THETA ** (-np.arange(n, dtype=np.float32) / n)).astype(np.float64)
    ang = np.concatenate([row[:, None] * inv, col[:, None] * inv], axis=-1)
    cos = np.cos(ang.astype(np.float32).astype(np.float64))
    sin = np.sin(ang.astype(np.float32).astype(np.float64))
    cos2 = np.concatenate([cos, cos], axis=-1).astype(np.float32)
    sin2 = np.concatenate([-sin, sin], axis=-1).astype(np.float32)
    return jnp.asarray(cos2), jnp.asarray(sin2)


def kernel(x, c, ctx, c_ctx, w_ada, b_ada, norm1_w, norm2_w, w_in, q_norm_w, k_norm_w,
           hy_conv_w, hy_conv_b, hy_pe_w1, hy_pe_b1, hy_freq1, hy_pe_w2, hy_pe_b2, hy_freq2,
           hy_pe_w3, hy_skip, w_att_proj, w_hy_proj, w_out, router_w, router_b,
           exp_w1, exp_w3, exp_w2, final_norm_w):
    B, S, D = x.shape
    C = ctx.shape[1]
    depth = w_ada.shape[0]
    n_lat = B * S
    n_ctx = B * C

    cos2, sin2 = _rope_tables(S)
    cos_id = jnp.ones((C, HEAD_DIM), F32)
    sin_id = jnp.zeros((C, HEAD_DIM), F32)
    dft_x = _dft_matrices(S)
    dft_c = _dft_matrices(C)

    mod_rows = 16
    cs = jnp.zeros((mod_rows, D), F32).at[:B].set(c).at[B].set(c_ctx)
    mods = _adaln(cs, w_ada, b_ada)

    r_hi = router_w.astype(BF16)
    r_lo = (router_w - r_hi.astype(F32)).astype(BF16)
    fw = final_norm_w.reshape(1, D)

    x2d = x.reshape(n_lat, D)
    c2d = ctx.reshape(n_ctx, D)
    for i in range(depth):
        last = i == depth - 1
        m_lat = [mods[i, :B, j * D:(j + 1) * D].reshape(B, 1, D) for j in range(6)]
        m_ctx = [mods[i, B:B + 1, j * D:(j + 1) * D].reshape(1, 1, D) for j in range(6)]
        wb = w_in[i].astype(BF16)
        n1 = norm1_w[i].reshape(1, D)
        n2 = norm2_w[i].reshape(1, D)
        qn = q_norm_w[i].reshape(1, HEAD_DIM)
        kn = k_norm_w[i].reshape(1, HEAD_DIM)
        wa = w_att_proj[i].astype(BF16)
        wh = w_hy_proj[i].astype(BF16)
        wo = w_out[i].astype(BF16)

        q, k, v, u, g = _inproj(x2d, S, m_lat[0], m_lat[1], n1, wb, qn, kn, cos2, sin2,
                                ("q", "k", "v", "u", "g"))
        if last:
            kc, vc = _inproj(c2d, C, m_ctx[0], m_ctx[1], n1, wb, qn, kn, cos_id, sin_id, ("k", "v"))
        else:
            qc, kc, vc, uc, gc = _inproj(c2d, C, m_ctx[0], m_ctx[1], n1, wb, qn, kn, cos_id, sin_id,
                                         ("q", "k", "v", "u", "g"))
        ya = _attention(q, [(k, v, S), (kc, vc, C)], B, S)

        filt = (hy_pe_w1[i], hy_pe_b1[i], hy_freq1[i], hy_pe_w2[i], hy_pe_b2[i], hy_freq2[i],
                hy_pe_w3[i])
        sp, sm, nyq = _hy_filter(S, *filt)
        kp, kq = _hy_spectrum(dft_x[0], dft_x[1], sp, sm, nyq)
        yh = _hyena(u, B, S, dft_x, kp, kq, hy_conv_w[i], hy_conv_b[i], hy_skip[i])

        n_tok = n_lat if last else n_lat + n_ctx
        x2d, hx, logits = _merge(ya, yh, g, x2d, S, m_lat[2], m_lat[3], m_lat[4], n2,
                                 wa, wh, wo, r_hi, r_lo, n_tok, 0)
        if not last:
            ya_c = _attention(qc, [(kc, vc, C)], B, C)
            sp, sm, nyq = _hy_filter(C, *filt)
            kp, kq = _hy_spectrum(dft_c[0], dft_c[1], sp, sm, nyq)
            yh_c = _hyena(uc, B, C, dft_c, kp, kq, hy_conv_w[i], hy_conv_b[i], hy_skip[i])
            c2d, hx, logits = _merge(ya_c, yh_c, gc, c2d, C, m_ctx[2], m_ctx[3], m_ctx[4], n2,
                                     wa, wh, wo, r_hi, r_lo, n_tok, n_lat, prev=(hx, logits))

        idx, gate = _route(logits, router_b)
        slot_tok, slot_gate, block_e, n_used, dest, n_blocks = _dispatch(idx, gate)
        xs = _gather_rows(slot_tok, n_used, hx, n_blocks)
        y = _experts(block_e, n_used, xs, slot_gate, exp_w1[i], exp_w3[i], exp_w2[i])
        x2d = _combine(dest, x2d, S, m_lat[5], fw, y, 0, last)
        if not last:
            c2d = _combine(dest, c2d, C, m_ctx[5], fw, y, n_lat, False)
    return x2d.reshape(B, S, D)
```

```python
import functools
import math

import numpy as np
import jax
import jax.numpy as jnp
from jax import lax
from jax.experimental import pallas as pl
from jax.experimental.pallas import tpu as pltpu

F32 = jnp.float32
BF16 = jnp.bfloat16

D_MODEL = 1024
DEPTH = 2
GRID_W = 64
NORM_EPS = 1e-6
N_HEADS = 8
N_KV_HEADS = 2
HEAD_DIM = 128
ATT_WIDTH = N_HEADS * HEAD_DIM
KV_WIDTH = N_KV_HEADS * HEAD_DIM
ROPE_THETA = 10000.0
ATT_SCALE = HEAD_DIM ** -0.5
HY_WIDTH = D_MODEL // 2
HY_ORDER = 2
HY_FILTER_HIDDEN = 64
HY_BANDS = 16
HY_PE_DIM = 1 + 2 * HY_BANDS
HY_PE_PAD = 128
HY_FAST_DECAY = 0.3
HY_SLOW_DECAY = 1.5
HY_DECAY_TARGET = 1e-2
N_EXPERTS = 64
N_GROUPS = 8
EXPERTS_PER_GROUP = N_EXPERTS // N_GROUPS
TOP_K = 2
EXPERT_DIM = 512
IN_WIDTH = ATT_WIDTH + 2 * KV_WIDTH + 3 * HY_WIDTH + 2 * D_MODEL
COL_Q = 0
COL_K = ATT_WIDTH
COL_V = ATT_WIDTH + KV_WIDTH
COL_U = ATT_WIDTH + 2 * KV_WIDTH
COL_G = COL_U + 3 * HY_WIDTH

MXU_COLS = 256
SUBLANES = 8
ROW_TILE = 512
ATT_Q_TILE = 256
DFT_ROWS = 64
MOE_ROWS = 256
VMEM_LIMIT = 56 * 1024 * 1024


def _dot(a, b):
    return jnp.dot(a, b, preferred_element_type=F32)


def _split(a):
    hi = a.astype(BF16)
    lo = (a - hi.astype(F32)).astype(BF16)
    return hi, lo


def _dot3(a, b):
    ah, al = _split(a)
    bh, bl = _split(b)
    return _dot(ah, bh) + (_dot(al, bh) + _dot(ah, bl))


def _rms(t):
    return t * lax.rsqrt(jnp.mean(t * t, axis=-1, keepdims=True) + NORM_EPS)


def _params(*sem):
    return pltpu.CompilerParams(dimension_semantics=sem, vmem_limit_bytes=VMEM_LIMIT)


def _adaln_kernel(c_ref, w_ref, b_ref, o_ref):
    c = c_ref[...]
    o_ref[...] = _dot3(c * jax.nn.sigmoid(c), w_ref[...]) + b_ref[...]


def _adaln(cs, w_ada, b_ada):
    depth, d, n = w_ada.shape
    rows = cs.shape[0]
    tn = 1536
    return pl.pallas_call(
        _adaln_kernel,
        grid=(depth, n // tn),
        in_specs=[pl.BlockSpec((rows, d), lambda l, j: (0, 0)),
                  pl.BlockSpec((None, d, tn), lambda l, j: (l, 0, j)),
                  pl.BlockSpec((None, 1, tn), lambda l, j: (l, 0, j))],
        out_specs=pl.BlockSpec((None, rows, tn), lambda l, j: (l, 0, j)),
        out_shape=jax.ShapeDtypeStruct((depth, rows, n), F32),
        compiler_params=_params("parallel", "parallel"),
        name="adaln",
    )(cs, w_ada, b_ada.reshape(depth, 1, n))


def _inproj_kernel(x_ref, sh_ref, sc_ref, nw_ref, w_ref, qn_ref, kn_ref, cos_ref, sin_ref,
                   *outs, sections):
    h = _rms(x_ref[...]) * nw_ref[...]
    h = h * (1.0 + sc_ref[...]) + sh_ref[...]
    hb = h.astype(BF16)
    cos = cos_ref[...]
    sin = sin_ref[...]
    o = dict(zip(sections, outs))

    def head_cols(col0, n_cols, norm_w, scale, out_ref):
        for c in range(n_cols // MXU_COLS):
            acc = _dot(hb, w_ref[:, col0 + c * MXU_COLS:col0 + (c + 1) * MXU_COLS])
            for j in range(MXU_COLS // HEAD_DIM):
                t = _rms(acc[:, j * HEAD_DIM:(j + 1) * HEAD_DIM]) * norm_w
                t = t * cos + pltpu.roll(t, HEAD_DIM // 2, 1) * sin
                lo = c * MXU_COLS + j * HEAD_DIM
                out_ref[:, lo:lo + HEAD_DIM] = (t * scale).astype(out_ref.dtype)

    if "q" in o:
        head_cols(COL_Q, ATT_WIDTH, qn_ref[...], ATT_SCALE, o["q"])
    if "k" in o:
        head_cols(COL_K, KV_WIDTH, kn_ref[...], 1.0, o["k"])
    if "v" in o:
        o["v"][...] = _dot(hb, w_ref[:, COL_V:COL_V + KV_WIDTH]).astype(BF16)
    if "u" in o:
        for c in range(3):
            o["u"][:, c * HY_WIDTH:(c + 1) * HY_WIDTH] = _dot(
                hb, w_ref[:, COL_U + c * HY_WIDTH:COL_U + (c + 1) * HY_WIDTH])
    if "g" in o:
        for c in range(4):
            acc = _dot(hb, w_ref[:, COL_G + c * 512:COL_G + (c + 1) * 512])
            o["g"][:, c * 512:(c + 1) * 512] = jax.nn.sigmoid(acc).astype(BF16)


_SECTION_SHAPES = {"q": (ATT_WIDTH, BF16), "k": (KV_WIDTH, BF16), "v": (KV_WIDTH, BF16),
                   "u": (3 * HY_WIDTH, F32), "g": (2 * D_MODEL, BF16)}


def _inproj(x2d, seq_len, sh, sc, nw, w_bf16, qn, kn, cos2, sin2, sections):
    rows, d = x2d.shape
    tm = min(ROW_TILE, seq_len)
    tiles_per_seq = seq_len // tm
    nb = sh.shape[0]
    mod_idx = (lambda i: (i // tiles_per_seq, 0, 0)) if nb > 1 else (lambda i: (0, 0, 0))
    const2 = lambda i: (0, 0)
    out_shape = [jax.ShapeDtypeStruct((rows, _SECTION_SHAPES[s][0]), _SECTION_SHAPES[s][1])
                 for s in sections]
    out_specs = [pl.BlockSpec((tm, _SECTION_SHAPES[s][0]), lambda i: (i, 0)) for s in sections]
    return pl.pallas_call(
        functools.partial(_inproj_kernel, sections=tuple(sections)),
        grid=(rows // tm,),
        in_specs=[pl.BlockSpec((tm, d), lambda i: (i, 0)),
                  pl.BlockSpec((None, 1, d), mod_idx),
                  pl.BlockSpec((None, 1, d), mod_idx),
                  pl.BlockSpec((1, d), const2),
                  pl.BlockSpec((d, IN_WIDTH), const2),
                  pl.BlockSpec((1, HEAD_DIM), const2),
                  pl.BlockSpec((1, HEAD_DIM), const2),
                  pl.BlockSpec((tm, HEAD_DIM), lambda i: (i % tiles_per_seq, 0)),
                  pl.BlockSpec((tm, HEAD_DIM), lambda i: (i % tiles_per_seq, 0))],
        out_specs=out_specs,
        out_shape=out_shape,
        compiler_params=_params("parallel"),
        name="inproj",
    )(x2d, sh, sc, nw, w_bf16, qn, kn, cos2, sin2)


def _attn_kernel(q_ref, *refs, n_kv_sets):
    kv = refs[:2 * n_kv_sets]
    o_ref = refs[2 * n_kv_sets]
    ks = [kv[2 * s][...] for s in range(n_kv_sets)]
    vs = [kv[2 * s + 1][...] for s in range(n_kv_sets)]
    nt = (((1,), (1,)), ((), ()))
    for g in range(N_HEADS // N_KV_HEADS):
        q = q_ref[:, g * HEAD_DIM:(g + 1) * HEAD_DIM]
        ss = [lax.dot_general(q, k, nt, preferred_element_type=F32) for k in ks]
        m = jnp.max(ss[0], axis=1, keepdims=True)
        for s in ss[1:]:
            m = jnp.maximum(m, jnp.max(s, axis=1, keepdims=True))
        l = None
        acc = None
        for s, v in zip(ss, vs):
            p = jnp.exp(s - m)
            ls = jnp.sum(p, axis=1, keepdims=True)
            a = _dot(p.astype(BF16), v)
            l = ls if l is None else l + ls
            acc = a if acc is None else acc + a
        o_ref[:, g * HEAD_DIM:(g + 1) * HEAD_DIM] = (acc * (1.0 / l)).astype(o_ref.dtype)


def _attention(q, kv_sets, batch, seq_len):
    tq = min(ATT_Q_TILE, seq_len)
    nq = seq_len // tq
    grp = (N_HEADS // N_KV_HEADS) * HEAD_DIM
    in_specs = [pl.BlockSpec((tq, grp), lambda b, j, i: (b * nq + i, j))]
    args = [q]
    for k, v, lk in kv_sets:
        in_specs += [pl.BlockSpec((lk, HEAD_DIM), lambda b, j, i: (b, j))] * 2
        args += [k, v]
    return pl.pallas_call(
        functools.partial(_attn_kernel, n_kv_sets=len(kv_sets)),
        grid=(batch, N_KV_HEADS, nq),
        in_specs=in_specs,
        out_specs=pl.BlockSpec((tq, grp), lambda b, j, i: (b * nq + i, j)),
        out_shape=jax.ShapeDtypeStruct(q.shape, BF16),
        compiler_params=_params("parallel", "parallel", "arbitrary"),
        name="attention",
    )(*args)


def _hy_filter_kernel(z_ref, t_ref, w1_ref, b1_ref, f1_ref, w2_ref, b2_ref, f2_ref, w3_ref,
                      dl_ref, sp_ref, sm_ref, nyq_ref):
    i = pl.program_id(0)
    tl = z_ref.shape[0]
    h = jnp.sin(f1_ref[...] * (_dot3(z_ref[...], w1_ref[...]) + b1_ref[...]))
    h = jnp.sin(f2_ref[...] * (_dot3(h, w2_ref[...]) + b2_ref[...]))
    h = _dot3(h, w3_ref[...])
    window = jnp.exp(-t_ref[...] * dl_ref[...])
    row = i * tl + lax.broadcasted_iota(jnp.int32, (tl, HY_WIDTH), 0)
    alt = (1 - 2 * (row & 1)).astype(F32)

    @pl.when(i == 0)
    def _():
        nyq_ref[...] = jnp.zeros_like(nyq_ref)

    for o in range(HY_ORDER):
        hf = h[:, (2 * o) * HY_WIDTH:(2 * o + 1) * HY_WIDTH] * window
        hb = h[:, (2 * o + 1) * HY_WIDTH:(2 * o + 2) * HY_WIDTH] * window
        hb = jnp.where(row == 0, 0.0, hb)
        plus = hf + hb
        sp_ref[:, o * HY_WIDTH:(o + 1) * HY_WIDTH] = plus.astype(BF16)
        sm_ref[:, o * HY_WIDTH:(o + 1) * HY_WIDTH] = (hf - hb).astype(BF16)
        nyq_ref[:, o * HY_WIDTH:(o + 1) * HY_WIDTH] += jnp.sum(alt * plus, axis=0, keepdims=True)


def _hy_filter(seq_len, pe_w1, pe_b1, freq1, pe_w2, pe_b2, freq2, pe_w3):
    t01 = np.linspace(0.0, 1.0, seq_len)[:, None]
    pos = np.arange(seq_len, dtype=np.float64)[:, None]
    bands = np.linspace(1e-4, HY_BANDS - 1, HY_BANDS)[None, :]
    f = 2.0 * math.pi * pos * bands / seq_len
    z = np.zeros((seq_len, HY_PE_PAD), np.float32)
    z[:, :HY_PE_DIM] = np.concatenate([t01, np.cos(f), -np.sin(f)], axis=-1)
    max_decay = math.log(HY_DECAY_TARGET) / HY_FAST_DECAY
    min_decay = math.log(HY_DECAY_TARGET) / HY_SLOW_DECAY
    deltas = np.abs(np.linspace(min_decay, max_decay, HY_WIDTH))[None, :].astype(np.float32)
    w1p = jnp.zeros((HY_PE_PAD, HY_FILTER_HIDDEN), F32).at[:HY_PE_DIM].set(pe_w1)
    tl = min(512, seq_len)
    hid = HY_FILTER_HIDDEN
    n_out = HY_ORDER * HY_WIDTH
    c2 = lambda i: (0, 0)
    return pl.pallas_call(
        _hy_filter_kernel,
        grid=(seq_len // tl,),
        in_specs=[pl.BlockSpec((tl, HY_PE_PAD), lambda i: (i, 0)),
                  pl.BlockSpec((tl, 1), lambda i: (i, 0)),
                  pl.BlockSpec((HY_PE_PAD, hid), c2), pl.BlockSpec((1, hid), c2),
                  pl.BlockSpec((1, hid), c2), pl.BlockSpec((hid, hid), c2),
                  pl.BlockSpec((1, hid), c2), pl.BlockSpec((1, hid), c2),
                  pl.BlockSpec((hid, 2 * n_out), c2), pl.BlockSpec((1, HY_WIDTH), c2)],
        out_specs=[pl.BlockSpec((tl, n_out), lambda i: (i, 0)),
                   pl.BlockSpec((tl, n_out), lambda i: (i, 0)),
                   pl.BlockSpec((1, n_out), c2)],
        out_shape=[jax.ShapeDtypeStruct((seq_len, n_out), BF16),
                   jax.ShapeDtypeStruct((seq_len, n_out), BF16),
                   jax.ShapeDtypeStruct((1, n_out), F32)],
        compiler_params=_params("arbitrary"),
        name="hy_filter",
    )(jnp.asarray(z), jnp.asarray(t01.astype(np.float32)), w1p, pe_b1.reshape(1, hid),
      freq1.reshape(1, hid), pe_w2, pe_b2.reshape(1, hid), freq2.reshape(1, hid), pe_w3,
      jnp.asarray(deltas))


def _dft_kernel(ca_ref, sa_ref, cb_ref, sb_ref, alt_ref, wc_ref, cm_ref, sm_ref, ci_ref, si_ref):
    i = pl.program_id(0)
    ca = ca_ref[...]
    sa = sa_ref[...]
    cb = cb_ref[...]
    sb = sb_ref[...]
    c = ca * cb - sa * sb
    s = sa * cb + ca * sb
    rows = lax.broadcasted_iota(jnp.int32, c.shape, 0)
    cols = lax.broadcasted_iota(jnp.int32, c.shape, 1)
    wc = wc_ref[...]
    cm_ref[...] = c.astype(BF16)
    sm_ref[...] = jnp.where((rows == 0) & (i == 0), alt_ref[...], s).astype(BF16)
    alt_t = (1 - 2 * (rows & 1)).astype(F32)
    ci_ref[...] = (c * wc).astype(BF16)
    si_ref[...] = (jnp.where(cols == 0, alt_t, s) * wc).astype(BF16)


def _dft_matrices(seq_len):
    n = 2 * seq_len
    idx = np.arange(seq_len, dtype=np.int64)[None, :]
    r1 = np.arange(seq_len // DFT_ROWS, dtype=np.int64)[:, None] * DFT_ROWS
    r0 = np.arange(DFT_ROWS, dtype=np.int64)[:, None]
    ang_a = ((r1 * idx) % n).astype(np.float64) * (2.0 * math.pi / n)
    ang_b = ((r0 * idx) % n).astype(np.float64) * (2.0 * math.pi / n)
    tab = lambda a: jnp.asarray(a.astype(np.float32))
    ca = tab(np.cos(ang_a)).reshape(-1, 1, seq_len)
    sa = tab(np.sin(ang_a)).reshape(-1, 1, seq_len)
    alt = tab(1.0 - 2.0 * (idx % 2))
    wc = tab(np.where(idx == 0, 1.0, 2.0) / n)
    row_blk = pl.BlockSpec((None, 1, seq_len), lambda i: (i, 0, 0))
    full = lambda r: pl.BlockSpec((r, seq_len), lambda i: (0, 0))
    out_blk = pl.BlockSpec((DFT_ROWS, seq_len), lambda i: (i, 0))
    return pl.pallas_call(
        _dft_kernel,
        grid=(seq_len // DFT_ROWS,),
        in_specs=[row_blk, row_blk, full(DFT_ROWS), full(DFT_ROWS), full(1), full(1)],
        out_specs=[out_blk] * 4,
        out_shape=[jax.ShapeDtypeStruct((seq_len, seq_len), BF16)] * 4,
        compiler_params=_params("parallel"),
        name="dft_matrices",
    )(ca, sa, tab(np.cos(ang_b)), tab(np.sin(ang_b)), alt, wc)


def _hy_spec_kernel(cm_ref, sm_ref, sp_ref, smn_ref, nyq_ref, kp_ref, kq_ref):
    i = pl.program_id(1)
    kp_ref[...] = _dot(cm_ref[...], sp_ref[...])
    kq = _dot(sm_ref[...], smn_ref[...])
    rows = lax.broadcasted_iota(jnp.int32, kq.shape, 0)
    kq_ref[...] = jnp.where((rows == 0) & (i == 0), nyq_ref[...], kq)


def _hy_spectrum(cm, sm, splus, sminus, nyq):
    seq_len = cm.shape[0]
    tf = min(512, seq_len)
    mat = pl.BlockSpec((tf, seq_len), lambda o, i: (i, 0))
    sig = pl.BlockSpec((seq_len, HY_WIDTH), lambda o, i: (0, o))
    out = pl.BlockSpec((tf, HY_WIDTH), lambda o, i: (i, o))
    return pl.pallas_call(
        _hy_spec_kernel,
        grid=(HY_ORDER, seq_len // tf),
        in_specs=[mat, mat, sig, sig, pl.BlockSpec((1, HY_WIDTH), lambda o, i: (0, o))],
        out_specs=[out, out],
        out_shape=[jax.ShapeDtypeStruct((seq_len, HY_ORDER * HY_WIDTH), F32)] * 2,
        compiler_params=_params("parallel", "arbitrary"),
        name="hy_spectrum",
    )(cm, sm, splus, sminus, nyq)


def _hy_sconv_kernel(u_ref, w_ref, b_ref, uc_ref, vb_ref, *, v_first_block):
    j = pl.program_id(1)
    x = u_ref[...]
    n = x.shape[0]
    rows = lax.broadcasted_iota(jnp.int32, x.shape, 0)
    prev = jnp.where(rows == 0, 0.0, pltpu.roll(x, 1, 0))
    nxt = jnp.where(rows == n - 1, 0.0, pltpu.roll(x, n - 1, 0))
    y = prev * w_ref[0:1, :] + x * w_ref[1:2, :] + nxt * w_ref[2:3, :] + b_ref[...]
    uc_ref[...] = y

    @pl.when(j >= v_first_block)
    def _():
        vb_ref[...] = y.astype(BF16)


def _hy_sconv(u, batch, seq_len, conv_w, conv_b):
    ch = u.shape[1]
    tc = 256
    v0 = (2 * HY_WIDTH) // tc
    u3 = u.reshape(batch, seq_len, ch)
    uc, vb = pl.pallas_call(
        functools.partial(_hy_sconv_kernel, v_first_block=v0),
        grid=(batch, ch // tc),
        in_specs=[pl.BlockSpec((None, seq_len, tc), lambda b, j: (b, 0, j)),
                  pl.BlockSpec((3, tc), lambda b, j: (0, j)),
                  pl.BlockSpec((1, tc), lambda b, j: (0, j))],
        out_specs=[pl.BlockSpec((None, seq_len, tc), lambda b, j: (b, 0, j)),
                   pl.BlockSpec((None, seq_len, tc), lambda b, j: (b, 0, jnp.maximum(j - v0, 0)))],
        out_shape=[jax.ShapeDtypeStruct((batch, seq_len, ch), F32),
                   jax.ShapeDtypeStruct((batch, seq_len, HY_WIDTH), BF16)],
        compiler_params=_params("parallel", "arbitrary"),
        name="hy_sconv",
    )(u3, conv_w, conv_b.reshape(1, ch))
    return uc, vb


def _hy_fwd_kernel(cm_ref, sm_ref, v_ref, kp_ref, kq_ref, yp_ref, yq_ref):
    i = pl.program_id(0)
    v = v_ref[...]
    vp = _dot(cm_ref[...], v)
    vq = _dot(sm_ref[...], v)
    kp = kp_ref[...]
    kq = kq_ref[...]
    rows = lax.broadcasted_iota(jnp.int32, vp.shape, 0)
    edge = (rows == 0) & (i == 0)
    t = vq * kq
    yp_ref[...] = (vp * kp - jnp.where(edge, 0.0, t)).astype(BF16)
    yq_ref[...] = jnp.where(edge, t, vp * kq + vq * kp).astype(BF16)


def _hy_forward(cm, sm, vb, kp, kq, order):
    batch, seq_len, _ = vb.shape
    tf = min(512, seq_len)
    mat = pl.BlockSpec((tf, seq_len), lambda i, b: (i, 0))
    spec = pl.BlockSpec((tf, HY_WIDTH), lambda i, b: (i, order))
    out = pl.BlockSpec((None, tf, HY_WIDTH), lambda i, b: (b, i, 0))
    return pl.pallas_call(
        _hy_fwd_kernel,
        grid=(seq_len // tf, batch),
        in_specs=[mat, mat, pl.BlockSpec((None, seq_len, HY_WIDTH), lambda i, b: (b, 0, 0)),
                  spec, spec],
        out_specs=[out, out],
        out_shape=[jax.ShapeDtypeStruct((batch, seq_len, HY_WIDTH), BF16)] * 2,
        compiler_params=_params("parallel", "arbitrary"),
        name="hy_forward",
    )(cm, sm, vb, kp, kq)


def _hy_inv_kernel(ci_ref, si_ref, yp_ref, yq_ref, a_ref, c_ref, skip_ref, *outs):
    y = _dot(ci_ref[...], yp_ref[...]) + _dot(si_ref[...], yq_ref[...])
    z = a_ref[...] * (y + skip_ref[...] * c_ref[...])
    for o_ref in outs:
        o_ref[...] = z.astype(o_ref.dtype)


def _hy_inverse(ci, si, yp, yq, a_arr, a_blk, c_arr, c_blk, skip, out_dtypes):
    batch, seq_len, _ = yp.shape
    tt = min(512, seq_len)
    mat = pl.BlockSpec((tt, seq_len), lambda i, b: (i, 0))
    sig = pl.BlockSpec((None, seq_len, HY_WIDTH), lambda i, b: (b, 0, 0))
    out = pl.BlockSpec((None, tt, HY_WIDTH), lambda i, b: (b, i, 0))
    return pl.pallas_call(
        _hy_inv_kernel,
        grid=(seq_len // tt, batch),
        in_specs=[mat, mat, sig, sig,
                  pl.BlockSpec((None, tt, HY_WIDTH), lambda i, b: (b, i, a_blk)),
                  pl.BlockSpec((None, tt, HY_WIDTH), lambda i, b: (b, i, c_blk)),
                  pl.BlockSpec((1, HY_WIDTH), lambda i, b: (0, 0))],
        out_specs=[out] * len(out_dtypes),
        out_shape=[jax.ShapeDtypeStruct((batch, seq_len, HY_WIDTH), dt) for dt in out_dtypes],
        compiler_params=_params("parallel", "arbitrary"),
        name="hy_inverse",
    )(ci, si, yp, yq, a_arr, c_arr, skip)


def _hyena(u, batch, seq_len, dft, kp, kq, conv_w, conv_b, skip):
    cm, sm, ci, si = dft
    uc, vb = _hy_sconv(u, batch, seq_len, conv_w, conv_b)
    yp, yq = _hy_forward(cm, sm, vb, kp, kq, 0)
    z, zb = _hy_inverse(ci, si, yp, yq, uc, 0, uc, 2, skip[0:1], (F32, BF16))
    yp, yq = _hy_forward(cm, sm, zb, kp, kq, 1)
    (y,) = _hy_inverse(ci, si, yp, yq, uc, 1, z, 0, skip[1:2], (BF16,))
    return y.reshape(batch * seq_len, HY_WIDTH)


def _merge_kernel(ya_ref, yh_ref, g_ref, x_ref, g1_ref, sh_ref, sc_ref, nw_ref,
                  wa_ref, wh_ref, wo_ref, rh_ref, rl_ref, *rest):
    xo_ref, hx_ref, lg_ref = rest[-3:]
    a = _dot(ya_ref[...], wa_ref[...])
    h = _dot(yh_ref[...], wh_ref[...])
    m = g_ref[:, :D_MODEL].astype(F32) * a + g_ref[:, D_MODEL:].astype(F32) * h
    xn = x_ref[...] + g1_ref[...] * _dot(m.astype(BF16), wo_ref[...])
    xo_ref[...] = xn
    hx = _rms(xn) * nw_ref[...]
    hx = hx * (1.0 + sc_ref[...]) + sh_ref[...]
    hx_ref[...] = hx
    hh, hl = _split(hx)
    nt = (((1,), (1,)), ((), ()))
    dn = lambda a, b: lax.dot_general(a, b, nt, preferred_element_type=F32)
    lg_ref[...] = dn(rh_ref[...], hh) + (dn(rh_ref[...], hl) + dn(rl_ref[...], hh))


def _merge(ya, yh, g, x2d, seq_len, g1, sh2, sc2, nw, wa, wh, wo, r_hi, r_lo,
           total_rows, row_offset, prev=None):
    rows, d = x2d.shape
    tm = min(ROW_TILE, seq_len)
    tiles_per_seq = seq_len // tm
    off = row_offset // tm
    nb = g1.shape[0]
    mod_idx = (lambda i: (i // tiles_per_seq, 0, 0)) if nb > 1 else (lambda i: (0, 0, 0))
    c2 = lambda i: (0, 0)
    row = lambda w: pl.BlockSpec((tm, w), lambda i: (i, 0))
    in_specs = [row(ATT_WIDTH), row(HY_WIDTH), row(2 * d), row(d),
                pl.BlockSpec((None, 1, d), mod_idx), pl.BlockSpec((None, 1, d), mod_idx),
                pl.BlockSpec((None, 1, d), mod_idx), pl.BlockSpec((1, d), c2),
                pl.BlockSpec((ATT_WIDTH, d), c2), pl.BlockSpec((HY_WIDTH, d), c2),
                pl.BlockSpec((d, d), c2), pl.BlockSpec((N_EXPERTS, d), c2),
                pl.BlockSpec((N_EXPERTS, d), c2)]
    args = [ya, yh, g, x2d, g1, sh2, sc2, nw, wa, wh, wo, r_hi, r_lo]
    aliases = {}
    if prev is not None:
        in_specs += [pl.BlockSpec(memory_space=pl.ANY)] * 2
        aliases = {len(args): 1, len(args) + 1: 2}
        args += list(prev)
    return pl.pallas_call(
        _merge_kernel,
        grid=(rows // tm,),
        in_specs=in_specs,
        out_specs=[row(d), pl.BlockSpec((tm, d), lambda i: (i + off, 0)),
                   pl.BlockSpec((N_EXPERTS, tm), lambda i: (0, i + off))],
        out_shape=[jax.ShapeDtypeStruct((rows, d), F32),
                   jax.ShapeDtypeStruct((total_rows, d), F32),
                   jax.ShapeDtypeStruct((N_EXPERTS, total_rows), F32)],
        input_output_aliases=aliases,
        compiler_params=_params("arbitrary"),
        name="merge",
    )(*args)


def _route_kernel(lg_ref, bias_ref, idx_ref, gate_ref, rank_ref, cnt_ref, run_ref):
    step = pl.program_id(0)
    tn = lg_ref.shape[2]
    shape = (N_GROUPS, EXPERTS_PER_GROUP, tn)
    s = jax.nn.sigmoid(lg_ref[...])
    b = s + bias_ref[...]
    mem = lax.broadcasted_iota(jnp.int32, shape, 1).astype(F32)
    grp = lax.broadcasted_iota(jnp.int32, (N_GROUPS, 1, tn), 0).astype(F32)
    big = float(N_EXPERTS)

    m1 = jnp.max(b, axis=1, keepdims=True)
    i1 = jnp.min(jnp.where(b == m1, mem, big), axis=1, keepdims=True)
    b2 = jnp.where(mem == i1, -jnp.inf, b)
    m2 = jnp.max(b2, axis=1, keepdims=True)
    i2 = jnp.min(jnp.where(b2 == m2, mem, big), axis=1, keepdims=True)
    gs = m1 + m2
    gmax = jnp.max(gs, axis=0, keepdims=True)
    gsel = jnp.min(jnp.where(gs == gmax, grp, big), axis=0, keepdims=True)
    selg = grp == gsel

    @pl.when(step == 0)
    def _():
        run_ref[...] = jnp.zeros_like(run_ref)

    tri = (lax.broadcasted_iota(jnp.int32, (tn, tn), 0)
           < lax.broadcasted_iota(jnp.int32, (tn, tn), 1)).astype(BF16)
    run = run_ref[...]
    ws = []
    for k, ik in enumerate((i1, i2)):
        hit = selg & (mem == ik)
        ws.append(jnp.sum(jnp.where(hit, s, 0.0), axis=(0, 1), keepdims=True))
        e_loc = jnp.sum(jnp.where(selg, ik, 0.0), axis=0, keepdims=True)
        idx_ref[k:k + 1, :] = (gsel * EXPERTS_PER_GROUP + e_loc).reshape(1, tn).astype(jnp.int32)
        oh = jnp.where(hit, 1.0, 0.0).reshape(N_EXPERTS, tn)
        before = run + _dot(oh.astype(BF16), tri)
        rank_ref[k:k + 1, :] = jnp.sum(oh * before, axis=0, keepdims=True).astype(jnp.int32)
        run = run + jnp.sum(oh, axis=1, keepdims=True)
    run_ref[...] = run
    cnt_ref[...] = run
    tot = ws[0] + ws[1]
    for k in range(TOP_K):
        gate_ref[k:k + 1, :] = (ws[k] / tot).reshape(1, tn)


def _route(logits_t, router_b):
    n_tok = logits_t.shape[1]
    tn = 512
    lg3 = logits_t.reshape(N_GROUPS, EXPERTS_PER_GROUP, n_tok)
    row2 = pl.BlockSpec((TOP_K, tn), lambda i: (0, i))
    return pl.pallas_call(
        _route_kernel,
        grid=(n_tok // tn,),
        in_specs=[pl.BlockSpec((N_GROUPS, EXPERTS_PER_GROUP, tn), lambda i: (0, 0, i)),
                  pl.BlockSpec((N_GROUPS, EXPERTS_PER_GROUP, 1), lambda i: (0, 0, 0))],
        out_specs=[row2, row2, row2, pl.BlockSpec((N_EXPERTS, 1), lambda i: (0, 0))],
        out_shape=[jax.ShapeDtypeStruct((TOP_K, n_tok), jnp.int32),
                   jax.ShapeDtypeStruct((TOP_K, n_tok), F32),
                   jax.ShapeDtypeStruct((TOP_K, n_tok), jnp.int32),
                   jax.ShapeDtypeStruct((N_EXPERTS, 1), F32)],
        scratch_shapes=[pltpu.VMEM((N_EXPERTS, 1), F32)],
        compiler_params=_params("arbitrary"),
        name="moe_route",
    )(lg3, router_b.astype(F32).reshape(N_GROUPS, EXPERTS_PER_GROUP, 1))


def _slot_kernel(idx_ref, rank_ref, start_ref, dest_ref):
    tn = idx_ref.shape[1]
    e = lax.broadcasted_iota(jnp.int32, (N_EXPERTS, tn), 0)
    for k in range(TOP_K):
        base = jnp.sum(jnp.where(e == idx_ref[k:k + 1, :], start_ref[...], 0.0), axis=0,
                       keepdims=True)
        dest_ref[k:k + 1, :] = base.astype(jnp.int32) + rank_ref[k:k + 1, :]


def _slots(idx, rank, pad_start):
    n_tok = idx.shape[1]
    tn = 512
    row2 = pl.BlockSpec((TOP_K, tn), lambda i: (0, i))
    return pl.pallas_call(
        _slot_kernel,
        grid=(n_tok // tn,),
        in_specs=[row2, row2, pl.BlockSpec((N_EXPERTS, 1), lambda i: (0, 0))],
        out_specs=row2,
        out_shape=jax.ShapeDtypeStruct((TOP_K, n_tok), jnp.int32),
        compiler_params=_params("parallel"),
        name="moe_slots",
    )(idx, rank, pad_start.astype(F32).reshape(N_EXPERTS, 1))


def _layout(counts, n_blocks):
    counts = counts.reshape(N_EXPERTS).astype(jnp.int32)
    padded = (counts + MOE_ROWS - 1) // MOE_ROWS * MOE_ROWS
    pad_end = jnp.cumsum(padded)
    pad_start = pad_end - padded
    block_e = jnp.minimum(jnp.searchsorted(pad_end, jnp.arange(n_blocks) * MOE_ROWS, side="right"),
                          N_EXPERTS - 1).astype(jnp.int32)
    n_used = (pad_end[-1] // MOE_ROWS).astype(jnp.int32).reshape(1)
    return pad_start, pad_start + counts, pad_end, block_e, n_used


_PAD_CHUNKS = tuple(2 ** p for p in range(int(math.log2(MOE_ROWS)) - 1, int(math.log2(SUBLANES)) - 1, -1))


def _scatter_kernel(dest_ref, lo_ref, hi_ref, x_ref, o_hbm, zeros, sem, zsem, *, n_tok):
    i = pl.program_id(0)
    tm = x_ref.shape[0]

    def pad_copies(act):
        def per_expert(e, carry):
            lo = lo_ref[e]
            head = (-lo) & (SUBLANES - 1)

            def single(j, c2):
                act(pltpu.make_async_copy(zeros.at[pl.ds(0, 1)], o_hbm.at[pl.ds(lo + j, 1)], zsem))
                return c2
            lax.fori_loop(0, head, single, 0)
            off = lo + head
            n = hi_ref[e] - off
            for c in _PAD_CHUNKS:
                @pl.when((n & c) != 0)
                def _():
                    act(pltpu.make_async_copy(zeros.at[pl.ds(0, c)],
                                              o_hbm.at[pl.ds(pl.multiple_of(off, SUBLANES), c)], zsem))
                off = off + (n & c)
            return carry
        lax.fori_loop(0, N_EXPERTS, per_expert, 0)

    @pl.when(i == 0)
    def _():
        zeros[...] = jnp.zeros_like(zeros)
        pad_copies(lambda cp: cp.start())

    def issue(r, carry):
        for k in range(TOP_K):
            pltpu.make_async_copy(x_ref.at[pl.ds(r, 1)],
                                  o_hbm.at[pl.ds(dest_ref[k * n_tok + i * tm + r], 1)], sem).start()
        return carry
    lax.fori_loop(0, tm, issue, 0, unroll=4)
    for k in range(TOP_K):
        pltpu.make_async_copy(x_ref, o_hbm.at[pl.ds(0, tm)], sem).wait()

    @pl.when(i == 0)
    def _():
        pad_copies(lambda cp: cp.wait())


def _scatter_rows(dest_flat, pad_lo, pad_hi, h, n_blocks):
    n_tok, d = h.shape
    tm = 256
    return pl.pallas_call(
        functools.partial(_scatter_kernel, n_tok=n_tok),
        grid_spec=pltpu.PrefetchScalarGridSpec(
            num_scalar_prefetch=3,
            grid=(n_tok // tm,),
            in_specs=[pl.BlockSpec((tm, d), lambda i, *_: (i, 0))],
            out_specs=pl.BlockSpec(memory_space=pl.ANY),
            scratch_shapes=[pltpu.VMEM((MOE_ROWS // 2, d), h.dtype),
                            pltpu.SemaphoreType.DMA(()), pltpu.SemaphoreType.DMA(())]),
        out_shape=jax.ShapeDtypeStruct((n_blocks * MOE_ROWS, d), h.dtype),
        compiler_params=pltpu.CompilerParams(dimension_semantics=("arbitrary",),
                                             vmem_limit_bytes=VMEM_LIMIT, has_side_effects=True),
        name="moe_scatter",
    )(dest_flat, pad_lo, pad_hi, h)


def _expert_kernel(be_ref, nused_ref, x_ref, w1_ref, w3_ref, w2_ref, y_ref):
    @pl.when(pl.program_id(0) < nused_ref[0])
    def _():
        x = x_ref[...].astype(BF16)
        a = _dot(x, w1_ref[...].astype(BF16))
        b = _dot(x, w3_ref[...].astype(BF16))
        h = (a * jax.nn.sigmoid(a)) * b
        y_ref[...] = _dot(h.astype(BF16), w2_ref[...].astype(BF16))


def _experts(block_e, n_used, xs, w1, w3, w2):
    n_slots, d = xs.shape
    n_blocks = n_slots // MOE_ROWS
    e_dim = w1.shape[2]
    blk = lambda i, be, nu: (jnp.minimum(i, nu[0] - 1), 0)
    wsel = lambda i, be, nu: (be[jnp.minimum(i, nu[0] - 1)], 0, 0)
    return pl.pallas_call(
        _expert_kernel,
        grid_spec=pltpu.PrefetchScalarGridSpec(
            num_scalar_prefetch=2,
            grid=(n_blocks,),
            in_specs=[pl.BlockSpec((MOE_ROWS, d), blk),
                      pl.BlockSpec((None, d, e_dim), wsel),
                      pl.BlockSpec((None, d, e_dim), wsel),
                      pl.BlockSpec((None, e_dim, d), wsel)],
            out_specs=pl.BlockSpec((MOE_ROWS, d), blk)),
        out_shape=jax.ShapeDtypeStruct((n_slots, d), F32),
        compiler_params=_params("arbitrary"),
        name="moe_experts",
    )(block_e, n_used, xs, w1, w3, w2)


def _combine_kernel(pos_ref, x_ref, gate_ref, g2_ref, fw_ref, y_hbm, o_ref, buf, sems, *,
                    tok_offset, n_tok, final_norm):
    i = pl.program_id(0)
    n = pl.num_programs(0)
    tm = x_ref.shape[0]

    def issue(step, slot):
        def body(r, carry):
            tok = tok_offset + step * tm + r
            for k in range(TOP_K):
                pltpu.make_async_copy(y_hbm.at[pl.ds(pos_ref[k * n_tok + tok], 1)],
                                      buf.at[slot, k, pl.ds(r, 1)], sems.at[slot]).start()
            return carry
        lax.fori_loop(0, tm, body, 0, unroll=4)

    @pl.when(i == 0)
    def _():
        issue(0, 0)

    @pl.when(i + 1 < n)
    def _():
        issue(i + 1, (i + 1) % 2)

    slot = i % 2
    pltpu.make_async_copy(buf.at[slot], buf.at[slot], sems.at[slot]).wait()
    gate = gate_ref[...]
    f = buf[slot, 0] * gate[:, 0:1] + buf[slot, 1] * gate[:, 1:2]
    out = x_ref[...] + g2_ref[...] * f
    if final_norm:
        out = _rms(out) * fw_ref[...]
    o_ref[...] = out


def _combine(pos, x2d, seq_len, gate, g2, fw, y, tok_offset, final_norm):
    rows, d = x2d.shape
    n_tok = gate.shape[0]
    tm = min(256, seq_len)
    tiles_per_seq = seq_len // tm
    off = tok_offset // tm
    nb = g2.shape[0]
    mod_idx = ((lambda i, p: (i // tiles_per_seq, 0, 0)) if nb > 1 else (lambda i, p: (0, 0, 0)))
    return pl.pallas_call(
        functools.partial(_combine_kernel, tok_offset=tok_offset, n_tok=n_tok,
                          final_norm=final_norm),
        grid_spec=pltpu.PrefetchScalarGridSpec(
            num_scalar_prefetch=1,
            grid=(rows // tm,),
            in_specs=[pl.BlockSpec((tm, d), lambda i, p: (i, 0)),
                      pl.BlockSpec((tm, TOP_K), lambda i, p: (i + off, 0)),
                      pl.BlockSpec((None, 1, d), mod_idx),
                      pl.BlockSpec((1, d), lambda i, p: (0, 0)),
                      pl.BlockSpec(memory_space=pl.ANY)],
            out_specs=pl.BlockSpec((tm, d), lambda i, p: (i, 0)),
            scratch_shapes=[pltpu.VMEM((2, TOP_K, tm, d), F32),
                            pltpu.SemaphoreType.DMA((2,))]),
        out_shape=jax.ShapeDtypeStruct((rows, d), F32),
        compiler_params=_params("arbitrary"),
        name="moe_combine",
    )(pos, x2d, gate, g2, fw, y)


def _rope_tables(seq_len):
    rows = seq_len // GRID_W
    row = np.repeat(np.arange(rows), GRID_W).astype(np.float64)
    col = np.tile(np.arange(GRID_W), rows).astype(np.float64)
    n = HEAD_DIM // 4
    inv = (ROPE_THETA ** (-np.arange(n, dtype=np.float32) / n)).astype(np.float64)
    ang = np.concatenate([row[:, None] * inv, col[:, None] * inv], axis=-1)
    cos = np.cos(ang.astype(np.float32).astype(np.float64))
    sin = np.sin(ang.astype(np.float32).astype(np.float64))
    cos2 = np.concatenate([cos, cos], axis=-1).astype(np.float32)
    sin2 = np.concatenate([-sin, sin], axis=-1).astype(np.float32)
    return jnp.asarray(cos2), jnp.asarray(sin2)


def kernel(x, c, ctx, c_ctx, w_ada, b_ada, norm1_w, norm2_w, w_in, q_norm_w, k_norm_w,
           hy_conv_w, hy_conv_b, hy_pe_w1, hy_pe_b1, hy_freq1, hy_pe_w2, hy_pe_b2, hy_freq2,
           hy_pe_w3, hy_skip, w_att_proj, w_hy_proj, w_out, router_w, router_b,
           exp_w1, exp_w3, exp_w2, final_norm_w):
    B, S, D = x.shape
    C = ctx.shape[1]
    depth = w_ada.shape[0]
    n_lat = B * S
    n_ctx = B * C

    cos2, sin2 = _rope_tables(S)
    cos_id = jnp.ones((C, HEAD_DIM), F32)
    sin_id = jnp.zeros((C, HEAD_DIM), F32)
    dft_x = _dft_matrices(S)
    dft_c = _dft_matrices(C)

    mod_rows = 16
    cs = jnp.zeros((mod_rows, D), F32).at[:B].set(c).at[B].set(c_ctx)
    mods = _adaln(cs, w_ada, b_ada)

    r_hi = router_w.T.astype(BF16)
    r_lo = (router_w.T - r_hi.astype(F32)).astype(BF16)
    fw = final_norm_w.reshape(1, D)

    x2d = x.reshape(n_lat, D)
    c2d = ctx.reshape(n_ctx, D)
    for i in range(depth):
        last = i == depth - 1
        m_lat = [mods[i, :B, j * D:(j + 1) * D].reshape(B, 1, D) for j in range(6)]
        m_ctx = [mods[i, B:B + 1, j * D:(j + 1) * D].reshape(1, 1, D) for j in range(6)]
        wb = w_in[i].astype(BF16)
        n1 = norm1_w[i].reshape(1, D)
        n2 = norm2_w[i].reshape(1, D)
        qn = q_norm_w[i].reshape(1, HEAD_DIM)
        kn = k_norm_w[i].reshape(1, HEAD_DIM)
        wa = w_att_proj[i].astype(BF16)
        wh = w_hy_proj[i].astype(BF16)
        wo = w_out[i].astype(BF16)

        q, k, v, u, g = _inproj(x2d, S, m_lat[0], m_lat[1], n1, wb, qn, kn, cos2, sin2,
                                ("q", "k", "v", "u", "g"))
        if last:
            kc, vc = _inproj(c2d, C, m_ctx[0], m_ctx[1], n1, wb, qn, kn, cos_id, sin_id, ("k", "v"))
        else:
            qc, kc, vc, uc, gc = _inproj(c2d, C, m_ctx[0], m_ctx[1], n1, wb, qn, kn, cos_id, sin_id,
                                         ("q", "k", "v", "u", "g"))
        ya = _attention(q, [(k, v, S), (kc, vc, C)], B, S)

        filt = (hy_pe_w1[i], hy_pe_b1[i], hy_freq1[i], hy_pe_w2[i], hy_pe_b2[i], hy_freq2[i],
                hy_pe_w3[i])
        sp, sm, nyq = _hy_filter(S, *filt)
        kp, kq = _hy_spectrum(dft_x[0], dft_x[1], sp, sm, nyq)
        yh = _hyena(u, B, S, dft_x, kp, kq, hy_conv_w[i], hy_conv_b[i], hy_skip[i])

        n_tok = n_lat if last else n_lat + n_ctx
        x2d, hx, logits = _merge(ya, yh, g, x2d, S, m_lat[2], m_lat[3], m_lat[4], n2,
                                 wa, wh, wo, r_hi, r_lo, n_tok, 0)
        if not last:
            ya_c = _attention(qc, [(kc, vc, C)], B, C)
            sp, sm, nyq = _hy_filter(C, *filt)
            kp, kq = _hy_spectrum(dft_c[0], dft_c[1], sp, sm, nyq)
            yh_c = _hyena(uc, B, C, dft_c, kp, kq, hy_conv_w[i], hy_conv_b[i], hy_skip[i])
            c2d, hx, logits = _merge(ya_c, yh_c, gc, c2d, C, m_ctx[2], m_ctx[3], m_ctx[4], n2,
                                     wa, wh, wo, r_hi, r_lo, n_tok, n_lat, prev=(hx, logits))

        idx, gate, rank, counts = _route(logits, router_b)
        n_blocks = -(-(n_tok * TOP_K) // MOE_ROWS) + N_EXPERTS
        pad_start, pad_lo, pad_hi, block_e, n_used = _layout(counts, n_blocks)
        dest = _slots(idx, rank, pad_start).reshape(TOP_K * n_tok)
        xs = _scatter_rows(dest, pad_lo, pad_hi, hx, n_blocks)
        y = _experts(block_e, n_used, xs, exp_w1[i], exp_w3[i], exp_w2[i])
        gate_t = gate.T
        x2d = _combine(dest, x2d, S, gate_t, m_lat[5], fw, y, 0, last)
        if not last:
            c2d = _combine(dest, c2d, C, gate_t, m_ctx[5], fw, y, n_lat, False)
    return x2d.reshape(B, S, D)
```

```python
import functools
import math

import numpy as np
import jax
import jax.numpy as jnp
from jax import lax
from jax.experimental import pallas as pl
from jax.experimental.pallas import tpu as pltpu

F32 = jnp.float32
BF16 = jnp.bfloat16

D_MODEL = 1024
DEPTH = 2
GRID_W = 64
NORM_EPS = 1e-6
N_HEADS = 8
N_KV_HEADS = 2
HEAD_DIM = 128
ATT_WIDTH = N_HEADS * HEAD_DIM
KV_WIDTH = N_KV_HEADS * HEAD_DIM
ROPE_THETA = 10000.0
ATT_SCALE = HEAD_DIM ** -0.5
LOG2_E = math.log2(math.e)
HY_WIDTH = D_MODEL // 2
HY_ORDER = 2
HY_FILTER_HIDDEN = 64
HY_BANDS = 16
HY_PE_DIM = 1 + 2 * HY_BANDS
HY_PE_PAD = 128
HY_FAST_DECAY = 0.3
HY_SLOW_DECAY = 1.5
HY_DECAY_TARGET = 1e-2
N_EXPERTS = 64
N_GROUPS = 8
EXPERTS_PER_GROUP = N_EXPERTS // N_GROUPS
TOP_K = 2
EXPERT_DIM = 512
IN_WIDTH = ATT_WIDTH + 2 * KV_WIDTH + 3 * HY_WIDTH + 2 * D_MODEL
COL_Q = 0
COL_K = ATT_WIDTH
COL_V = ATT_WIDTH + KV_WIDTH
COL_U = ATT_WIDTH + 2 * KV_WIDTH
COL_G = COL_U + 3 * HY_WIDTH

MXU_COLS = 256
SUBLANES = 8
ROW_TILE = 512
ATT_Q_TILE = 512
ATT_KEY_CHUNK = 512
DFT_ROWS = 64
MOE_ROWS = 256
VMEM_LIMIT = 56 * 1024 * 1024


def _dot(a, b):
    return jnp.dot(a, b, preferred_element_type=F32)


def _split(a):
    hi = a.astype(BF16)
    lo = (a - hi.astype(F32)).astype(BF16)
    return hi, lo


def _dot3(a, b):
    ah, al = _split(a)
    bh, bl = _split(b)
    return _dot(ah, bh) + (_dot(al, bh) + _dot(ah, bl))


def _rms(t):
    return t * lax.rsqrt(jnp.mean(t * t, axis=-1, keepdims=True) + NORM_EPS)


def _params(*sem):
    return pltpu.CompilerParams(dimension_semantics=sem, vmem_limit_bytes=VMEM_LIMIT)


def _adaln_kernel(c_ref, w_ref, b_ref, o_ref):
    c = c_ref[...]
    o_ref[...] = _dot3(c * jax.nn.sigmoid(c), w_ref[...]) + b_ref[...]


def _adaln(cs, w_ada, b_ada):
    depth, d, n = w_ada.shape
    rows = cs.shape[0]
    tn = 1536
    return pl.pallas_call(
        _adaln_kernel,
        grid=(depth, n // tn),
        in_specs=[pl.BlockSpec((rows, d), lambda l, j: (0, 0)),
                  pl.BlockSpec((None, d, tn), lambda l, j: (l, 0, j)),
                  pl.BlockSpec((None, 1, tn), lambda l, j: (l, 0, j))],
        out_specs=pl.BlockSpec((None, rows, tn), lambda l, j: (l, 0, j)),
        out_shape=jax.ShapeDtypeStruct((depth, rows, n), F32),
        compiler_params=_params("parallel", "parallel"),
        name="adaln",
    )(cs, w_ada, b_ada.reshape(depth, 1, n))


def _inproj_kernel(x_ref, sh_ref, sc_ref, nw_ref, w_ref, qn_ref, kn_ref, cos_ref, sin_ref,
                   *outs, sections):
    h = _rms(x_ref[...]) * nw_ref[...]
    h = h * (1.0 + sc_ref[...]) + sh_ref[...]
    hb = h.astype(BF16)
    cos = cos_ref[...]
    sin = sin_ref[...]
    o = dict(zip(sections, outs))

    def head_cols(col0, n_cols, norm_w, scale, out_ref):
        for c in range(n_cols // MXU_COLS):
            acc = _dot(hb, w_ref[:, col0 + c * MXU_COLS:col0 + (c + 1) * MXU_COLS])
            for j in range(MXU_COLS // HEAD_DIM):
                t = _rms(acc[:, j * HEAD_DIM:(j + 1) * HEAD_DIM]) * norm_w
                t = t * cos + pltpu.roll(t, HEAD_DIM // 2, 1) * sin
                lo = c * MXU_COLS + j * HEAD_DIM
                out_ref[:, lo:lo + HEAD_DIM] = (t * scale).astype(out_ref.dtype)

    if "q" in o:
        head_cols(COL_Q, ATT_WIDTH, qn_ref[...], ATT_SCALE * LOG2_E, o["q"])
    if "k" in o:
        head_cols(COL_K, KV_WIDTH, kn_ref[...], 1.0, o["k"])
    if "v" in o:
        acc = _dot(hb, w_ref[:, COL_V:COL_V + KV_WIDTH]).astype(BF16)
        ones = jnp.ones((acc.shape[0], HEAD_DIM), BF16)
        for j in range(N_KV_HEADS):
            o["v"][:, 2 * j * HEAD_DIM:(2 * j + 1) * HEAD_DIM] = acc[:, j * HEAD_DIM:(j + 1) * HEAD_DIM]
            o["v"][:, (2 * j + 1) * HEAD_DIM:(2 * j + 2) * HEAD_DIM] = ones
    if "u" in o:
        for c in range(3):
            o["u"][:, c * HY_WIDTH:(c + 1) * HY_WIDTH] = _dot(
                hb, w_ref[:, COL_U + c * HY_WIDTH:COL_U + (c + 1) * HY_WIDTH])
    if "g" in o:
        for c in range(4):
            acc = _dot(hb, w_ref[:, COL_G + c * 512:COL_G + (c + 1) * 512])
            o["g"][:, c * 512:(c + 1) * 512] = jax.nn.sigmoid(acc).astype(BF16)


_SECTION_SHAPES = {"q": (ATT_WIDTH, BF16), "k": (KV_WIDTH, BF16), "v": (2 * KV_WIDTH, BF16),
                   "u": (3 * HY_WIDTH, F32), "g": (2 * D_MODEL, BF16)}


def _inproj(x2d, seq_len, sh, sc, nw, w_bf16, qn, kn, cos2, sin2, sections):
    rows, d = x2d.shape
    tm = min(ROW_TILE, seq_len)
    tiles_per_seq = seq_len // tm
    nb = sh.shape[0]
    mod_idx = (lambda i: (i // tiles_per_seq, 0, 0)) if nb > 1 else (lambda i: (0, 0, 0))
    const2 = lambda i: (0, 0)
    out_shape = [jax.ShapeDtypeStruct((rows, _SECTION_SHAPES[s][0]), _SECTION_SHAPES[s][1])
                 for s in sections]
    out_specs = [pl.BlockSpec((tm, _SECTION_SHAPES[s][0]), lambda i: (i, 0)) for s in sections]
    return pl.pallas_call(
        functools.partial(_inproj_kernel, sections=tuple(sections)),
        grid=(rows // tm,),
        in_specs=[pl.BlockSpec((tm, d), lambda i: (i, 0)),
                  pl.BlockSpec((None, 1, d), mod_idx),
                  pl.BlockSpec((None, 1, d), mod_idx),
                  pl.BlockSpec((1, d), const2),
                  pl.BlockSpec((d, IN_WIDTH), const2),
                  pl.BlockSpec((1, HEAD_DIM), const2),
                  pl.BlockSpec((1, HEAD_DIM), const2),
                  pl.BlockSpec((tm, HEAD_DIM), lambda i: (i % tiles_per_seq, 0)),
                  pl.BlockSpec((tm, HEAD_DIM), lambda i: (i % tiles_per_seq, 0))],
        out_specs=out_specs,
        out_shape=out_shape,
        compiler_params=_params("parallel"),
        name="inproj",
    )(x2d, sh, sc, nw, w_bf16, qn, kn, cos2, sin2)


def _attn_kernel(q_ref, *refs, n_kv_sets):
    kv = refs[:2 * n_kv_sets]
    o_ref = refs[2 * n_kv_sets]
    nt = (((1,), (1,)), ((), ()))
    for g in range(N_HEADS // N_KV_HEADS):
        q = q_ref[:, g * HEAD_DIM:(g + 1) * HEAD_DIM]
        m = None
        acc = None
        for s_idx in range(n_kv_sets):
            k_ref, v_ref = kv[2 * s_idx], kv[2 * s_idx + 1]
            lk = k_ref.shape[0]
            for c0 in range(0, lk, ATT_KEY_CHUNK):
                c1 = min(c0 + ATT_KEY_CHUNK, lk)
                s = lax.dot_general(q, k_ref[c0:c1, :], nt, preferred_element_type=F32)
                mc = jnp.max(s, axis=1, keepdims=True)
                if m is None:
                    m = mc
                    acc = _dot(jnp.exp2(s - m).astype(BF16), v_ref[c0:c1, :])
                else:
                    m_new = jnp.maximum(m, mc)
                    acc = jnp.exp2(m - m_new) * acc + _dot(jnp.exp2(s - m_new).astype(BF16),
                                                           v_ref[c0:c1, :])
                    m = m_new
        o_ref[:, g * HEAD_DIM:(g + 1) * HEAD_DIM] = (
            acc[:, :HEAD_DIM] / acc[:, HEAD_DIM:]).astype(o_ref.dtype)


def _attention(q, kv_sets, batch, seq_len):
    tq = min(ATT_Q_TILE, seq_len)
    nq = seq_len // tq
    grp = (N_HEADS // N_KV_HEADS) * HEAD_DIM
    in_specs = [pl.BlockSpec((tq, grp), lambda b, j, i: (b * nq + i, j))]
    args = [q]
    for k, v, lk in kv_sets:
        in_specs += [pl.BlockSpec((lk, HEAD_DIM), lambda b, j, i: (b, j)),
                     pl.BlockSpec((lk, 2 * HEAD_DIM), lambda b, j, i: (b, j))]
        args += [k, v]
    return pl.pallas_call(
        functools.partial(_attn_kernel, n_kv_sets=len(kv_sets)),
        grid=(batch, N_KV_HEADS, nq),
        in_specs=in_specs,
        out_specs=pl.BlockSpec((tq, grp), lambda b, j, i: (b * nq + i, j)),
        out_shape=jax.ShapeDtypeStruct(q.shape, BF16),
        compiler_params=_params("parallel", "parallel", "arbitrary"),
        name="attention",
    )(*args)


def _hy_filter_kernel(z_ref, t_ref, w1_ref, b1_ref, f1_ref, w2_ref, b2_ref, f2_ref, w3_ref,
                      dl_ref, sp_ref, sm_ref, nyq_ref):
    i = pl.program_id(0)
    tl = z_ref.shape[0]
    h = jnp.sin(f1_ref[...] * (_dot3(z_ref[...], w1_ref[...]) + b1_ref[...]))
    h = jnp.sin(f2_ref[...] * (_dot3(h, w2_ref[...]) + b2_ref[...]))
    h = _dot3(h, w3_ref[...])
    window = jnp.exp(-t_ref[...] * dl_ref[...])
    row = i * tl + lax.broadcasted_iota(jnp.int32, (tl, HY_WIDTH), 0)
    alt = (1 - 2 * (row & 1)).astype(F32)

    @pl.when(i == 0)
    def _():
        nyq_ref[...] = jnp.zeros_like(nyq_ref)

    for o in range(HY_ORDER):
        hf = h[:, (2 * o) * HY_WIDTH:(2 * o + 1) * HY_WIDTH] * window
        hb = h[:, (2 * o + 1) * HY_WIDTH:(2 * o + 2) * HY_WIDTH] * window
        hb = jnp.where(row == 0, 0.0, hb)
        plus = hf + hb
        sp_ref[:, o * HY_WIDTH:(o + 1) * HY_WIDTH] = plus.astype(BF16)
        sm_ref[:, o * HY_WIDTH:(o + 1) * HY_WIDTH] = (hf - hb).astype(BF16)
        nyq_ref[:, o * HY_WIDTH:(o + 1) * HY_WIDTH] += jnp.sum(alt * plus, axis=0, keepdims=True)


def _hy_filter(seq_len, pe_w1, pe_b1, freq1, pe_w2, pe_b2, freq2, pe_w3):
    t01 = np.linspace(0.0, 1.0, seq_len)[:, None]
    pos = np.arange(seq_len, dtype=np.float64)[:, None]
    bands = np.linspace(1e-4, HY_BANDS - 1, HY_BANDS)[None, :]
    f = 2.0 * math.pi * pos * bands / seq_len
    z = np.zeros((seq_len, HY_PE_PAD), np.float32)
    z[:, :HY_PE_DIM] = np.concatenate([t01, np.cos(f), -np.sin(f)], axis=-1)
    max_decay = math.log(HY_DECAY_TARGET) / HY_FAST_DECAY
    min_decay = math.log(HY_DECAY_TARGET) / HY_SLOW_DECAY
    deltas = np.abs(np.linspace(min_decay, max_decay, HY_WIDTH))[None, :].astype(np.float32)
    w1p = jnp.zeros((HY_PE_PAD, HY_FILTER_HIDDEN), F32).at[:HY_PE_DIM].set(pe_w1)
    tl = min(512, seq_len)
    hid = HY_FILTER_HIDDEN
    n_out = HY_ORDER * HY_WIDTH
    c2 = lambda i: (0, 0)
    return pl.pallas_call(
        _hy_filter_kernel,
        grid=(seq_len // tl,),
        in_specs=[pl.BlockSpec((tl, HY_PE_PAD), lambda i: (i, 0)),
                  pl.BlockSpec((tl, 1), lambda i: (i, 0)),
                  pl.BlockSpec((HY_PE_PAD, hid), c2), pl.BlockSpec((1, hid), c2),
                  pl.BlockSpec((1, hid), c2), pl.BlockSpec((hid, hid), c2),
                  pl.BlockSpec((1, hid), c2), pl.BlockSpec((1, hid), c2),
                  pl.BlockSpec((hid, 2 * n_out), c2), pl.BlockSpec((1, HY_WIDTH), c2)],
        out_specs=[pl.BlockSpec((tl, n_out), lambda i: (i, 0)),
                   pl.BlockSpec((tl, n_out), lambda i: (i, 0)),
                   pl.BlockSpec((1, n_out), c2)],
        out_shape=[jax.ShapeDtypeStruct((seq_len, n_out), BF16),
                   jax.ShapeDtypeStruct((seq_len, n_out), BF16),
                   jax.ShapeDtypeStruct((1, n_out), F32)],
        compiler_params=_params("arbitrary"),
        name="hy_filter",
    )(jnp.asarray(z), jnp.asarray(t01.astype(np.float32)), w1p, pe_b1.reshape(1, hid),
      freq1.reshape(1, hid), pe_w2, pe_b2.reshape(1, hid), freq2.reshape(1, hid), pe_w3,
      jnp.asarray(deltas))


def _dft_kernel(ca_ref, sa_ref, cb_ref, sb_ref, alt_ref, wc_ref, cm_ref, sm_ref, ci_ref, si_ref):
    i = pl.program_id(0)
    ca = ca_ref[...]
    sa = sa_ref[...]
    cb = cb_ref[...]
    sb = sb_ref[...]
    c = ca * cb - sa * sb
    s = sa * cb + ca * sb
    rows = lax.broadcasted_iota(jnp.int32, c.shape, 0)
    cols = lax.broadcasted_iota(jnp.int32, c.shape, 1)
    wc = wc_ref[...]
    cm_ref[...] = c.astype(BF16)
    sm_ref[...] = jnp.where((rows == 0) & (i == 0), alt_ref[...], s).astype(BF16)
    alt_t = (1 - 2 * (rows & 1)).astype(F32)
    ci_ref[...] = (c * wc).astype(BF16)
    si_ref[...] = (jnp.where(cols == 0, alt_t, s) * wc).astype(BF16)


def _dft_matrices(seq_len):
    n = 2 * seq_len
    idx = np.arange(seq_len, dtype=np.int64)[None, :]
    r1 = np.arange(seq_len // DFT_ROWS, dtype=np.int64)[:, None] * DFT_ROWS
    r0 = np.arange(DFT_ROWS, dtype=np.int64)[:, None]
    ang_a = ((r1 * idx) % n).astype(np.float64) * (2.0 * math.pi / n)
    ang_b = ((r0 * idx) % n).astype(np.float64) * (2.0 * math.pi / n)
    tab = lambda a: jnp.asarray(a.astype(np.float32))
    ca = tab(np.cos(ang_a)).reshape(-1, 1, seq_len)
    sa = tab(np.sin(ang_a)).reshape(-1, 1, seq_len)
    alt = tab(1.0 - 2.0 * (idx % 2))
    wc = tab(np.where(idx == 0, 1.0, 2.0) / n)
    row_blk = pl.BlockSpec((None, 1, seq_len), lambda i: (i, 0, 0))
    full = lambda r: pl.BlockSpec((r, seq_len), lambda i: (0, 0))
    out_blk = pl.BlockSpec((DFT_ROWS, seq_len), lambda i: (i, 0))
    return pl.pallas_call(
        _dft_kernel,
        grid=(seq_len // DFT_ROWS,),
        in_specs=[row_blk, row_blk, full(DFT_ROWS), full(DFT_ROWS), full(1), full(1)],
        out_specs=[out_blk] * 4,
        out_shape=[jax.ShapeDtypeStruct((seq_len, seq_len), BF16)] * 4,
        compiler_params=_params("parallel"),
        name="dft_matrices",
    )(ca, sa, tab(np.cos(ang_b)), tab(np.sin(ang_b)), alt, wc)


def _hy_spec_kernel(cm_ref, sm_ref, sp_ref, smn_ref, nyq_ref, kp_ref, kq_ref):
    i = pl.program_id(1)
    kp_ref[...] = _dot(cm_ref[...], sp_ref[...])
    kq = _dot(sm_ref[...], smn_ref[...])
    rows = lax.broadcasted_iota(jnp.int32, kq.shape, 0)
    kq_ref[...] = jnp.where((rows == 0) & (i == 0), nyq_ref[...], kq)


def _hy_spectrum(cm, sm, splus, sminus, nyq):
    seq_len = cm.shape[0]
    tf = min(512, seq_len)
    mat = pl.BlockSpec((tf, seq_len), lambda o, i: (i, 0))
    sig = pl.BlockSpec((seq_len, HY_WIDTH), lambda o, i: (0, o))
    out = pl.BlockSpec((tf, HY_WIDTH), lambda o, i: (i, o))
    return pl.pallas_call(
        _hy_spec_kernel,
        grid=(HY_ORDER, seq_len // tf),
        in_specs=[mat, mat, sig, sig, pl.BlockSpec((1, HY_WIDTH), lambda o, i: (0, o))],
        out_specs=[out, out],
        out_shape=[jax.ShapeDtypeStruct((seq_len, HY_ORDER * HY_WIDTH), F32)] * 2,
        compiler_params=_params("parallel", "arbitrary"),
        name="hy_spectrum",
    )(cm, sm, splus, sminus, nyq)


def _hy_sconv_kernel(u_ref, w_ref, b_ref, uc_ref, vb_ref, *, v_first_block):
    j = pl.program_id(1)
    x = u_ref[...]
    n = x.shape[0]
    rows = lax.broadcasted_iota(jnp.int32, x.shape, 0)
    prev = jnp.where(rows == 0, 0.0, pltpu.roll(x, 1, 0))
    nxt = jnp.where(rows == n - 1, 0.0, pltpu.roll(x, n - 1, 0))
    y = prev * w_ref[0:1, :] + x * w_ref[1:2, :] + nxt * w_ref[2:3, :] + b_ref[...]
    uc_ref[...] = y

    @pl.when(j >= v_first_block)
    def _():
        vb_ref[...] = y.astype(BF16)


def _hy_sconv(u, batch, seq_len, conv_w, conv_b):
    ch = u.shape[1]
    tc = 256
    v0 = (2 * HY_WIDTH) // tc
    u3 = u.reshape(batch, seq_len, ch)
    uc, vb = pl.pallas_call(
        functools.partial(_hy_sconv_kernel, v_first_block=v0),
        grid=(batch, ch // tc),
        in_specs=[pl.BlockSpec((None, seq_len, tc), lambda b, j: (b, 0, j)),
                  pl.BlockSpec((3, tc), lambda b, j: (0, j)),
                  pl.BlockSpec((1, tc), lambda b, j: (0, j))],
        out_specs=[pl.BlockSpec((None, seq_len, tc), lambda b, j: (b, 0, j)),
                   pl.BlockSpec((None, seq_len, tc), lambda b, j: (b, 0, jnp.maximum(j - v0, 0)))],
        out_shape=[jax.ShapeDtypeStruct((batch, seq_len, ch), F32),
                   jax.ShapeDtypeStruct((batch, seq_len, HY_WIDTH), BF16)],
        compiler_params=_params("parallel", "arbitrary"),
        name="hy_sconv",
    )(u3, conv_w, conv_b.reshape(1, ch))
    return uc, vb


def _hy_fwd_kernel(cm_ref, sm_ref, v_ref, kp_ref, kq_ref, yp_ref, yq_ref):
    i = pl.program_id(0)
    v = v_ref[...]
    vp = _dot(cm_ref[...], v)
    vq = _dot(sm_ref[...], v)
    kp = kp_ref[...]
    kq = kq_ref[...]
    rows = lax.broadcasted_iota(jnp.int32, vp.shape, 0)
    edge = (rows == 0) & (i == 0)
    t = vq * kq
    yp_ref[...] = (vp * kp - jnp.where(edge, 0.0, t)).astype(BF16)
    yq_ref[...] = jnp.where(edge, t, vp * kq + vq * kp).astype(BF16)


def _hy_forward(cm, sm, vb, kp, kq, order):
    batch, seq_len, _ = vb.shape
    tf = min(512, seq_len)
    mat = pl.BlockSpec((tf, seq_len), lambda i, b: (i, 0))
    spec = pl.BlockSpec((tf, HY_WIDTH), lambda i, b: (i, order))
    out = pl.BlockSpec((None, tf, HY_WIDTH), lambda i, b: (b, i, 0))
    return pl.pallas_call(
        _hy_fwd_kernel,
        grid=(seq_len // tf, batch),
        in_specs=[mat, mat, pl.BlockSpec((None, seq_len, HY_WIDTH), lambda i, b: (b, 0, 0)),
                  spec, spec],
        out_specs=[out, out],
        out_shape=[jax.ShapeDtypeStruct((batch, seq_len, HY_WIDTH), BF16)] * 2,
        compiler_params=_params("parallel", "arbitrary"),
        name="hy_forward",
    )(cm, sm, vb, kp, kq)


def _hy_inv_kernel(ci_ref, si_ref, yp_ref, yq_ref, a_ref, c_ref, skip_ref, *outs):
    y = _dot(ci_ref[...], yp_ref[...]) + _dot(si_ref[...], yq_ref[...])
    z = a_ref[...] * (y + skip_ref[...] * c_ref[...])
    for o_ref in outs:
        o_ref[...] = z.astype(o_ref.dtype)


def _hy_inverse(ci, si, yp, yq, a_arr, a_blk, c_arr, c_blk, skip, out_dtypes):
    batch, seq_len, _ = yp.shape
    tt = min(512, seq_len)
    mat = pl.BlockSpec((tt, seq_len), lambda i, b: (i, 0))
    sig = pl.BlockSpec((None, seq_len, HY_WIDTH), lambda i, b: (b, 0, 0))
    out = pl.BlockSpec((None, tt, HY_WIDTH), lambda i, b: (b, i, 0))
    return pl.pallas_call(
        _hy_inv_kernel,
        grid=(seq_len // tt, batch),
        in_specs=[mat, mat, sig, sig,
                  pl.BlockSpec((None, tt, HY_WIDTH), lambda i, b: (b, i, a_blk)),
                  pl.BlockSpec((None, tt, HY_WIDTH), lambda i, b: (b, i, c_blk)),
                  pl.BlockSpec((1, HY_WIDTH), lambda i, b: (0, 0))],
        out_specs=[out] * len(out_dtypes),
        out_shape=[jax.ShapeDtypeStruct((batch, seq_len, HY_WIDTH), dt) for dt in out_dtypes],
        compiler_params=_params("parallel", "arbitrary"),
        name="hy_inverse",
    )(ci, si, yp, yq, a_arr, c_arr, skip)


def _hyena(u, batch, seq_len, dft, kp, kq, conv_w, conv_b, skip):
    cm, sm, ci, si = dft
    uc, vb = _hy_sconv(u, batch, seq_len, conv_w, conv_b)
    yp, yq = _hy_forward(cm, sm, vb, kp, kq, 0)
    z, zb = _hy_inverse(ci, si, yp, yq, uc, 0, uc, 2, skip[0:1], (F32, BF16))
    yp, yq = _hy_forward(cm, sm, zb, kp, kq, 1)
    (y,) = _hy_inverse(ci, si, yp, yq, uc, 1, z, 0, skip[1:2], (BF16,))
    return y.reshape(batch * seq_len, HY_WIDTH)


def _merge_kernel(ya_ref, yh_ref, g_ref, x_ref, g1_ref, sh_ref, sc_ref, nw_ref,
                  wa_ref, wh_ref, wo_ref, rh_ref, rl_ref, *rest):
    xo_ref, hx_ref, lg_ref = rest[-3:]
    a = _dot(ya_ref[...], wa_ref[...])
    h = _dot(yh_ref[...], wh_ref[...])
    m = g_ref[:, :D_MODEL].astype(F32) * a + g_ref[:, D_MODEL:].astype(F32) * h
    xn = x_ref[...] + g1_ref[...] * _dot(m.astype(BF16), wo_ref[...])
    xo_ref[...] = xn
    hx = _rms(xn) * nw_ref[...]
    hx = hx * (1.0 + sc_ref[...]) + sh_ref[...]
    hx_ref[...] = hx
    hh, hl = _split(hx)
    nt = (((1,), (1,)), ((), ()))
    dn = lambda a, b: lax.dot_general(a, b, nt, preferred_element_type=F32)
    lg_ref[...] = dn(rh_ref[...], hh) + (dn(rh_ref[...], hl) + dn(rl_ref[...], hh))


def _merge(ya, yh, g, x2d, seq_len, g1, sh2, sc2, nw, wa, wh, wo, r_hi, r_lo,
           total_rows, row_offset, prev=None):
    rows, d = x2d.shape
    tm = min(ROW_TILE, seq_len)
    tiles_per_seq = seq_len // tm
    off = row_offset // tm
    nb = g1.shape[0]
    mod_idx = (lambda i: (i // tiles_per_seq, 0, 0)) if nb > 1 else (lambda i: (0, 0, 0))
    c2 = lambda i: (0, 0)
    row = lambda w: pl.BlockSpec((tm, w), lambda i: (i, 0))
    in_specs = [row(ATT_WIDTH), row(HY_WIDTH), row(2 * d), row(d),
                pl.BlockSpec((None, 1, d), mod_idx), pl.BlockSpec((None, 1, d), mod_idx),
                pl.BlockSpec((None, 1, d), mod_idx), pl.BlockSpec((1, d), c2),
                pl.BlockSpec((ATT_WIDTH, d), c2), pl.BlockSpec((HY_WIDTH, d), c2),
                pl.BlockSpec((d, d), c2), pl.BlockSpec((N_EXPERTS, d), c2),
                pl.BlockSpec((N_EXPERTS, d), c2)]
    args = [ya, yh, g, x2d, g1, sh2, sc2, nw, wa, wh, wo, r_hi, r_lo]
    aliases = {}
    if prev is not None:
        in_specs += [pl.BlockSpec(memory_space=pl.ANY)] * 2
        aliases = {len(args): 1, len(args) + 1: 2}
        args += list(prev)
    return pl.pallas_call(
        _merge_kernel,
        grid=(rows // tm,),
        in_specs=in_specs,
        out_specs=[row(d), pl.BlockSpec((tm, d), lambda i: (i + off, 0)),
                   pl.BlockSpec((N_EXPERTS, tm), lambda i: (0, i + off))],
        out_shape=[jax.ShapeDtypeStruct((rows, d), F32),
                   jax.ShapeDtypeStruct((total_rows, d), F32),
                   jax.ShapeDtypeStruct((N_EXPERTS, total_rows), F32)],
        input_output_aliases=aliases,
        compiler_params=_params("arbitrary"),
        name="merge",
    )(*args)


def _route_kernel(lg_ref, bias_ref, idx_ref, gate_ref, rank_ref, cnt_ref, run_ref):
    step = pl.program_id(0)
    tn = lg_ref.shape[2]
    shape = (N_GROUPS, EXPERTS_PER_GROUP, tn)
    s = jax.nn.sigmoid(lg_ref[...])
    b = s + bias_ref[...]
    mem = lax.broadcasted_iota(jnp.int32, shape, 1).astype(F32)
    grp = lax.broadcasted_iota(jnp.int32, (N_GROUPS, 1, tn), 0).astype(F32)
    big = float(N_EXPERTS)

    m1 = jnp.max(b, axis=1, keepdims=True)
    i1 = jnp.min(jnp.where(b == m1, mem, big), axis=1, keepdims=True)
    b2 = jnp.where(mem == i1, -jnp.inf, b)
    m2 = jnp.max(b2, axis=1, keepdims=True)
    i2 = jnp.min(jnp.where(b2 == m2, mem, big), axis=1, keepdims=True)
    gs = m1 + m2
    gmax = jnp.max(gs, axis=0, keepdims=True)
    gsel = jnp.min(jnp.where(gs == gmax, grp, big), axis=0, keepdims=True)
    selg = grp == gsel

    @pl.when(step == 0)
    def _():
        run_ref[...] = jnp.zeros_like(run_ref)

    tri = (lax.broadcasted_iota(jnp.int32, (tn, tn), 0)
           < lax.broadcasted_iota(jnp.int32, (tn, tn), 1)).astype(BF16)
    run = run_ref[...]
    ws = []
    for k, ik in enumerate((i1, i2)):
        hit = selg & (mem == ik)
        ws.append(jnp.sum(jnp.where(hit, s, 0.0), axis=(0, 1), keepdims=True))
        e_loc = jnp.sum(jnp.where(selg, ik, 0.0), axis=0, keepdims=True)
        idx_ref[k:k + 1, :] = (gsel * EXPERTS_PER_GROUP + e_loc).reshape(1, tn).astype(jnp.int32)
        oh = jnp.where(hit, 1.0, 0.0).reshape(N_EXPERTS, tn)
        before = run + _dot(oh.astype(BF16), tri)
        rank_ref[k:k + 1, :] = jnp.sum(oh * before, axis=0, keepdims=True).astype(jnp.int32)
        run = run + jnp.sum(oh, axis=1, keepdims=True)
    run_ref[...] = run
    cnt_ref[...] = run
    tot = ws[0] + ws[1]
    for k in range(TOP_K):
        gate_ref[k:k + 1, :] = (ws[k] / tot).reshape(1, tn)


def _route(logits_t, router_b):
    n_tok = logits_t.shape[1]
    tn = 512
    lg3 = logits_t.reshape(N_GROUPS, EXPERTS_PER_GROUP, n_tok)
    row2 = pl.BlockSpec((TOP_K, tn), lambda i: (0, i))
    return pl.pallas_call(
        _route_kernel,
        grid=(n_tok // tn,),
        in_specs=[pl.BlockSpec((N_GROUPS, EXPERTS_PER_GROUP, tn), lambda i: (0, 0, i)),
                  pl.BlockSpec((N_GROUPS, EXPERTS_PER_GROUP, 1), lambda i: (0, 0, 0))],
        out_specs=[row2, row2, row2, pl.BlockSpec((N_EXPERTS, 1), lambda i: (0, 0))],
        out_shape=[jax.ShapeDtypeStruct((TOP_K, n_tok), jnp.int32),
                   jax.ShapeDtypeStruct((TOP_K, n_tok), F32),
                   jax.ShapeDtypeStruct((TOP_K, n_tok), jnp.int32),
                   jax.ShapeDtypeStruct((N_EXPERTS, 1), F32)],
        scratch_shapes=[pltpu.VMEM((N_EXPERTS, 1), F32)],
        compiler_params=_params("arbitrary"),
        name="moe_route",
    )(lg3, router_b.astype(F32).reshape(N_GROUPS, EXPERTS_PER_GROUP, 1))


def _slot_kernel(idx_ref, rank_ref, start_ref, dest_ref):
    tn = idx_ref.shape[1]
    e = lax.broadcasted_iota(jnp.int32, (N_EXPERTS, tn), 0)
    for k in range(TOP_K):
        base = jnp.sum(jnp.where(e == idx_ref[k:k + 1, :], start_ref[...], 0.0), axis=0,
                       keepdims=True)
        dest_ref[k:k + 1, :] = base.astype(jnp.int32) + rank_ref[k:k + 1, :]


def _slots(idx, rank, pad_start):
    n_tok = idx.shape[1]
    tn = 512
    row2 = pl.BlockSpec((TOP_K, tn), lambda i: (0, i))
    return pl.pallas_call(
        _slot_kernel,
        grid=(n_tok // tn,),
        in_specs=[row2, row2, pl.BlockSpec((N_EXPERTS, 1), lambda i: (0, 0))],
        out_specs=row2,
        out_shape=jax.ShapeDtypeStruct((TOP_K, n_tok), jnp.int32),
        compiler_params=_params("parallel"),
        name="moe_slots",
    )(idx, rank, pad_start.astype(F32).reshape(N_EXPERTS, 1))


def _layout(counts, n_blocks):
    counts = counts.reshape(N_EXPERTS).astype(jnp.int32)
    padded = (counts + MOE_ROWS - 1) // MOE_ROWS * MOE_ROWS
    pad_end = jnp.cumsum(padded)
    pad_start = pad_end - padded
    blk_row = jnp.arange(n_blocks, dtype=jnp.int32)[:, None] * MOE_ROWS
    block_e = jnp.minimum(jnp.sum((pad_end[None, :] <= blk_row).astype(jnp.int32), axis=1),
                          N_EXPERTS - 1)
    prev_e = jnp.concatenate([jnp.full((1,), -1, jnp.int32), block_e[:-1]])
    block_new = (block_e != prev_e).astype(jnp.int32)
    n_used = (pad_end[-1] // MOE_ROWS).astype(jnp.int32).reshape(1)
    return pad_start, pad_start + counts, pad_end, block_e, block_new, n_used


_PAD_CHUNKS = tuple(2 ** p for p in range(int(math.log2(MOE_ROWS)) - 1, int(math.log2(SUBLANES)) - 1, -1))


def _scatter_kernel(dest_ref, lo_ref, hi_ref, x_ref, o_hbm, zeros, sem, zsem, *, n_tok):
    i = pl.program_id(0)
    tm = x_ref.shape[0]

    def pad_copies(act):
        def per_expert(e, carry):
            lo = lo_ref[e]
            head = (-lo) & (SUBLANES - 1)

            def single(j, c2):
                act(pltpu.make_async_copy(zeros.at[pl.ds(0, 1)], o_hbm.at[pl.ds(lo + j, 1)], zsem))
                return c2
            lax.fori_loop(0, head, single, 0)
            off = lo + head
            n = hi_ref[e] - off
            for c in _PAD_CHUNKS:
                @pl.when((n & c) != 0)
                def _():
                    act(pltpu.make_async_copy(zeros.at[pl.ds(0, c)],
                                              o_hbm.at[pl.ds(pl.multiple_of(off, SUBLANES), c)], zsem))
                off = off + (n & c)
            return carry
        lax.fori_loop(0, N_EXPERTS, per_expert, 0)

    @pl.when(i == 0)
    def _():
        zeros[...] = jnp.zeros_like(zeros)
        pad_copies(lambda cp: cp.start())

    def issue(r, carry):
        for k in range(TOP_K):
            pltpu.make_async_copy(x_ref.at[pl.ds(r, 1)],
                                  o_hbm.at[pl.ds(dest_ref[k * n_tok + i * tm + r], 1)], sem).start()
        return carry
    lax.fori_loop(0, tm, issue, 0, unroll=4)
    for k in range(TOP_K):
        pltpu.make_async_copy(x_ref, o_hbm.at[pl.ds(0, tm)], sem).wait()

    @pl.when(i == 0)
    def _():
        pad_copies(lambda cp: cp.wait())


def _scatter_rows(dest_flat, pad_lo, pad_hi, h, n_blocks):
    n_tok, d = h.shape
    tm = 256
    return pl.pallas_call(
        functools.partial(_scatter_kernel, n_tok=n_tok),
        grid_spec=pltpu.PrefetchScalarGridSpec(
            num_scalar_prefetch=3,
            grid=(n_tok // tm,),
            in_specs=[pl.BlockSpec((tm, d), lambda i, *_: (i, 0))],
            out_specs=pl.BlockSpec(memory_space=pl.ANY),
            scratch_shapes=[pltpu.VMEM((MOE_ROWS // 2, d), h.dtype),
                            pltpu.SemaphoreType.DMA(()), pltpu.SemaphoreType.DMA(())]),
        out_shape=jax.ShapeDtypeStruct((n_blocks * MOE_ROWS, d), h.dtype),
        compiler_params=pltpu.CompilerParams(dimension_semantics=("arbitrary",),
                                             vmem_limit_bytes=VMEM_LIMIT, has_side_effects=True),
        name="moe_scatter",
    )(dest_flat, pad_lo, pad_hi, h)


def _expert_kernel(be_ref, new_ref, nused_ref, x_ref, w1_ref, w3_ref, w2_ref, y_ref,
                   w1b, w3b, w2b):
    i = pl.program_id(0)

    @pl.when(i < nused_ref[0])
    def _():
        @pl.when(new_ref[i] == 1)
        def _():
            w1b[...] = w1_ref[...].astype(BF16)
            w3b[...] = w3_ref[...].astype(BF16)
            w2b[...] = w2_ref[...].astype(BF16)

        x = x_ref[...].astype(BF16)
        a = _dot(x, w1b[...])
        b = _dot(x, w3b[...])
        h = (a * jax.nn.sigmoid(a)) * b
        y_ref[...] = _dot(h.astype(BF16), w2b[...])


def _experts(block_e, block_new, n_used, xs, w1, w3, w2, layer):
    n_slots, d = xs.shape
    n_blocks = n_slots // MOE_ROWS
    e_dim = w1.shape[3]
    blk = lambda i, be, bn, nu: (jnp.minimum(i, nu[0] - 1), 0)
    wsel = lambda i, be, bn, nu: (layer, be[jnp.minimum(i, nu[0] - 1)], 0, 0)
    return pl.pallas_call(
        _expert_kernel,
        grid_spec=pltpu.PrefetchScalarGridSpec(
            num_scalar_prefetch=3,
            grid=(n_blocks,),
            in_specs=[pl.BlockSpec((MOE_ROWS, d), blk),
                      pl.BlockSpec((None, None, d, e_dim), wsel),
                      pl.BlockSpec((None, None, d, e_dim), wsel),
                      pl.BlockSpec((None, None, e_dim, d), wsel)],
            out_specs=pl.BlockSpec((MOE_ROWS, d), blk),
            scratch_shapes=[pltpu.VMEM((d, e_dim), BF16), pltpu.VMEM((d, e_dim), BF16),
                            pltpu.VMEM((e_dim, d), BF16)]),
        out_shape=jax.ShapeDtypeStruct((n_slots, d), F32),
        compiler_params=_params("arbitrary"),
        name="moe_experts",
    )(block_e, block_new, n_used, xs, w1, w3, w2)


def _combine_kernel(pos_ref, x_ref, gate_ref, g2_ref, fw_ref, y_hbm, o_ref, buf, sems, *,
                    tok_offset, n_tok, final_norm):
    i = pl.program_id(0)
    n = pl.num_programs(0)
    tm = x_ref.shape[0]

    def issue(step, slot):
        def body(r, carry):
            tok = tok_offset + step * tm + r
            for k in range(TOP_K):
                pltpu.make_async_copy(y_hbm.at[pl.ds(pos_ref[k * n_tok + tok], 1)],
                                      buf.at[slot, k, pl.ds(r, 1)], sems.at[slot]).start()
            return carry
        lax.fori_loop(0, tm, body, 0, unroll=4)

    @pl.when(i == 0)
    def _():
        issue(0, 0)

    @pl.when(i + 1 < n)
    def _():
        issue(i + 1, (i + 1) % 2)

    slot = i % 2
    pltpu.make_async_copy(buf.at[slot], buf.at[slot], sems.at[slot]).wait()
    gate = gate_ref[...]
    f = buf[slot, 0] * gate[:, 0:1] + buf[slot, 1] * gate[:, 1:2]
    out = x_ref[...] + g2_ref[...] * f
    if final_norm:
        out = _rms(out) * fw_ref[...]
    o_ref[...] = out


def _combine(pos, x2d, seq_len, gate, g2, fw, y, tok_offset, final_norm):
    rows, d = x2d.shape
    n_tok = gate.shape[0]
    tm = min(256, seq_len)
    tiles_per_seq = seq_len // tm
    off = tok_offset // tm
    nb = g2.shape[0]
    mod_idx = ((lambda i, p: (i // tiles_per_seq, 0, 0)) if nb > 1 else (lambda i, p: (0, 0, 0)))
    return pl.pallas_call(
        functools.partial(_combine_kernel, tok_offset=tok_offset, n_tok=n_tok,
                          final_norm=final_norm),
        grid_spec=pltpu.PrefetchScalarGridSpec(
            num_scalar_prefetch=1,
            grid=(rows // tm,),
            in_specs=[pl.BlockSpec((tm, d), lambda i, p: (i, 0)),
                      pl.BlockSpec((tm, TOP_K), lambda i, p: (i + off, 0)),
                      pl.BlockSpec((None, 1, d), mod_idx),
                      pl.BlockSpec((1, d), lambda i, p: (0, 0)),
                      pl.BlockSpec(memory_space=pl.ANY)],
            out_specs=pl.BlockSpec((tm, d), lambda i, p: (i, 0)),
            scratch_shapes=[pltpu.VMEM((2, TOP_K, tm, d), F32),
                            pltpu.SemaphoreType.DMA((2,))]),
        out_shape=jax.ShapeDtypeStruct((rows, d), F32),
        compiler_params=_params("arbitrary"),
        name="moe_combine",
    )(pos, x2d, gate, g2, fw, y)


def _rope_tables(seq_len):
    rows = seq_len // GRID_W
    row = np.repeat(np.arange(rows), GRID_W).astype(np.float64)
    col = np.tile(np.arange(GRID_W), rows).astype(np.float64)
    n = HEAD_DIM // 4
    inv = (ROPE_THETA ** (-np.arange(n, dtype=np.float32) / n)).astype(np.float64)
    ang = np.concatenate([row[:, None] * inv, col[:, None] * inv], axis=-1)
    cos = np.cos(ang.astype(np.float32).astype(np.float64))
    sin = np.sin(ang.astype(np.float32).astype(np.float64))
    cos2 = np.concatenate([cos, cos], axis=-1).astype(np.float32)
    sin2 = np.concatenate([-sin, sin], axis=-1).astype(np.float32)
    return jnp.asarray(cos2), jnp.asarray(sin2)


def kernel(x, c, ctx, c_ctx, w_ada, b_ada, norm1_w, norm2_w, w_in, q_norm_w, k_norm_w,
           hy_conv_w, hy_conv_b, hy_pe_w1, hy_pe_b1, hy_freq1, hy_pe_w2, hy_pe_b2, hy_freq2,
           hy_pe_w3, hy_skip, w_att_proj, w_hy_proj, w_out, router_w, router_b,
           exp_w1, exp_w3, exp_w2, final_norm_w):
    B, S, D = x.shape
    C = ctx.shape[1]
    depth = w_ada.shape[0]
    n_lat = B * S
    n_ctx = B * C

    cos2, sin2 = _rope_tables(S)
    cos_id = jnp.ones((C, HEAD_DIM), F32)
    sin_id = jnp.zeros((C, HEAD_DIM), F32)
    dft_x = _dft_matrices(S)
    dft_c = _dft_matrices(C)

    mod_rows = 16
    cs = jnp.zeros((mod_rows, D), F32).at[:B].set(c).at[B].set(c_ctx)
    mods = _adaln(cs, w_ada, b_ada)

    r_hi = router_w.T.astype(BF16)
    r_lo = (router_w.T - r_hi.astype(F32)).astype(BF16)
    fw = final_norm_w.reshape(1, D)

    x2d = x.reshape(n_lat, D)
    c2d = ctx.reshape(n_ctx, D)
    for i in range(depth):
        last = i == depth - 1
        m_lat = [mods[i, :B, j * D:(j + 1) * D].reshape(B, 1, D) for j in range(6)]
        m_ctx = [mods[i, B:B + 1, j * D:(j + 1) * D].reshape(1, 1, D) for j in range(6)]
        wb = w_in[i].astype(BF16)
        n1 = norm1_w[i].reshape(1, D)
        n2 = norm2_w[i].reshape(1, D)
        qn = q_norm_w[i].reshape(1, HEAD_DIM)
        kn = k_norm_w[i].reshape(1, HEAD_DIM)
        wa = w_att_proj[i].astype(BF16)
        wh = w_hy_proj[i].astype(BF16)
        wo = w_out[i].astype(BF16)

        q, k, v, u, g = _inproj(x2d, S, m_lat[0], m_lat[1], n1, wb, qn, kn, cos2, sin2,
                                ("q", "k", "v", "u", "g"))
        if last:
            kc, vc = _inproj(c2d, C, m_ctx[0], m_ctx[1], n1, wb, qn, kn, cos_id, sin_id, ("k", "v"))
        else:
            qc, kc, vc, uc, gc = _inproj(c2d, C, m_ctx[0], m_ctx[1], n1, wb, qn, kn, cos_id, sin_id,
                                         ("q", "k", "v", "u", "g"))
        ya = _attention(q, [(k, v, S), (kc, vc, C)], B, S)

        filt = (hy_pe_w1[i], hy_pe_b1[i], hy_freq1[i], hy_pe_w2[i], hy_pe_b2[i], hy_freq2[i],
                hy_pe_w3[i])
        sp, sm, nyq = _hy_filter(S, *filt)
        kp, kq = _hy_spectrum(dft_x[0], dft_x[1], sp, sm, nyq)
        yh = _hyena(u, B, S, dft_x, kp, kq, hy_conv_w[i], hy_conv_b[i], hy_skip[i])

        n_tok = n_lat if last else n_lat + n_ctx
        x2d, hx, logits = _merge(ya, yh, g, x2d, S, m_lat[2], m_lat[3], m_lat[4], n2,
                                 wa, wh, wo, r_hi, r_lo, n_tok, 0)
        if not last:
            ya_c = _attention(qc, [(kc, vc, C)], B, C)
            sp, sm, nyq = _hy_filter(C, *filt)
            kp, kq = _hy_spectrum(dft_c[0], dft_c[1], sp, sm, nyq)
            yh_c = _hyena(uc, B, C, dft_c, kp, kq, hy_conv_w[i], hy_conv_b[i], hy_skip[i])
            c2d, hx, logits = _merge(ya_c, yh_c, gc, c2d, C, m_ctx[2], m_ctx[3], m_ctx[4], n2,
                                     wa, wh, wo, r_hi, r_lo, n_tok, n_lat, prev=(hx, logits))

        idx, gate, rank, counts = _route(logits, router_b)
        n_blocks = -(-(n_tok * TOP_K) // MOE_ROWS) + N_EXPERTS
        pad_start, pad_lo, pad_hi, block_e, block_new, n_used = _layout(counts, n_blocks)
        dest = _slots(idx, rank, pad_start).reshape(TOP_K * n_tok)
        xs = _scatter_rows(dest, pad_lo, pad_hi, hx, n_blocks)
        y = _experts(block_e, block_new, n_used, xs, exp_w1, exp_w3, exp_w2, i)
        gate_t = gate.T
        x2d = _combine(dest, x2d, S, gate_t, m_lat[5], fw, y, 0, last)
        if not last:
            c2d = _combine(dest, c2d, C, gate_t, m_ctx[5], fw, y, n_lat, False)
    return x2d.reshape(B, S, D)
```

```python
import functools
import math

import numpy as np
import jax
import jax.numpy as jnp
from jax import lax
from jax.experimental import pallas as pl
from jax.experimental.pallas import tpu as pltpu

F32 = jnp.float32
BF16 = jnp.bfloat16

D_MODEL = 1024
DEPTH = 2
GRID_W = 64
NORM_EPS = 1e-6
N_HEADS = 8
N_KV_HEADS = 2
HEAD_DIM = 128
ATT_WIDTH = N_HEADS * HEAD_DIM
KV_WIDTH = N_KV_HEADS * HEAD_DIM
ROPE_THETA = 10000.0
ATT_SCALE = HEAD_DIM ** -0.5
LOG2_E = math.log2(math.e)
HY_WIDTH = D_MODEL // 2
HY_ORDER = 2
HY_FILTER_HIDDEN = 64
HY_BANDS = 16
HY_PE_DIM = 1 + 2 * HY_BANDS
HY_PE_PAD = 128
HY_FAST_DECAY = 0.3
HY_SLOW_DECAY = 1.5
HY_DECAY_TARGET = 1e-2
N_EXPERTS = 64
N_GROUPS = 8
EXPERTS_PER_GROUP = N_EXPERTS // N_GROUPS
TOP_K = 2
EXPERT_DIM = 512
IN_WIDTH = ATT_WIDTH + 2 * KV_WIDTH + 3 * HY_WIDTH + 2 * D_MODEL
COL_Q = 0
COL_K = ATT_WIDTH
COL_V = ATT_WIDTH + KV_WIDTH
COL_U = ATT_WIDTH + 2 * KV_WIDTH
COL_G = COL_U + 3 * HY_WIDTH

MXU_COLS = 256
SUBLANES = 8
LANES = 128
ROW_TILE = 512
ATT_Q_TILE = 512
ATT_KEY_CHUNK = 512
DFT_ROWS = 64
MOE_ROWS = 256
VMEM_LIMIT = 56 * 1024 * 1024


def _dot(a, b):
    return jnp.dot(a, b, preferred_element_type=F32)


def _split(a):
    hi = a.astype(BF16)
    lo = (a - hi.astype(F32)).astype(BF16)
    return hi, lo


def _dot3(a, b):
    ah, al = _split(a)
    bh, bl = _split(b)
    return _dot(ah, bh) + (_dot(al, bh) + _dot(ah, bl))


def _tiles_to_rows(t):
    c = pltpu.einshape("tjl->jtl", t)
    return jnp.concatenate([c[j] for j in range(c.shape[0])], axis=1)


def _rows_to_tiles(x):
    c = jnp.stack([x[:, j * LANES:(j + 1) * LANES] for j in range(x.shape[1] // LANES)], axis=0)
    return pltpu.einshape("jtl->tjl", c)


def _rms(t):
    return t * lax.rsqrt(jnp.mean(t * t, axis=-1, keepdims=True) + NORM_EPS)


def _params(*sem):
    return pltpu.CompilerParams(dimension_semantics=sem, vmem_limit_bytes=VMEM_LIMIT)


def _adaln_kernel(c_ref, w_ref, b_ref, o_ref):
    c = c_ref[...]
    o_ref[...] = _dot3(c * jax.nn.sigmoid(c), w_ref[...]) + b_ref[...]


def _adaln(cs, w_ada, b_ada):
    depth, d, n = w_ada.shape
    rows = cs.shape[0]
    tn = 1536
    return pl.pallas_call(
        _adaln_kernel,
        grid=(depth, n // tn),
        in_specs=[pl.BlockSpec((rows, d), lambda l, j: (0, 0)),
                  pl.BlockSpec((None, d, tn), lambda l, j: (l, 0, j)),
                  pl.BlockSpec((None, 1, tn), lambda l, j: (l, 0, j))],
        out_specs=pl.BlockSpec((None, rows, tn), lambda l, j: (l, 0, j)),
        out_shape=jax.ShapeDtypeStruct((depth, rows, n), F32),
        compiler_params=_params("parallel", "parallel"),
        name="adaln",
    )(cs, w_ada, b_ada.reshape(depth, 1, n))


def _inproj_kernel(x_ref, sh_ref, sc_ref, nw_ref, w_ref, qn_ref, kn_ref, cos_ref, sin_ref,
                   *outs, sections):
    h = _rms(x_ref[...]) * nw_ref[...]
    h = h * (1.0 + sc_ref[...]) + sh_ref[...]
    hb = h.astype(BF16)
    cos = cos_ref[...]
    sin = sin_ref[...]
    o = dict(zip(sections, outs))

    def head_cols(col0, n_cols, norm_w, scale, out_ref):
        for c in range(n_cols // MXU_COLS):
            acc = _dot(hb, w_ref[:, col0 + c * MXU_COLS:col0 + (c + 1) * MXU_COLS])
            for j in range(MXU_COLS // HEAD_DIM):
                t = _rms(acc[:, j * HEAD_DIM:(j + 1) * HEAD_DIM]) * norm_w
                t = t * cos + pltpu.roll(t, HEAD_DIM // 2, 1) * sin
                lo = c * MXU_COLS + j * HEAD_DIM
                out_ref[:, lo:lo + HEAD_DIM] = (t * scale).astype(out_ref.dtype)

    if "q" in o:
        head_cols(COL_Q, ATT_WIDTH, qn_ref[...], ATT_SCALE * LOG2_E, o["q"])
    if "k" in o:
        head_cols(COL_K, KV_WIDTH, kn_ref[...], 1.0, o["k"])
    if "v" in o:
        acc = _dot(hb, w_ref[:, COL_V:COL_V + KV_WIDTH]).astype(BF16)
        ones = jnp.ones((acc.shape[0], HEAD_DIM), BF16)
        for j in range(N_KV_HEADS):
            o["v"][:, 2 * j * HEAD_DIM:(2 * j + 1) * HEAD_DIM] = acc[:, j * HEAD_DIM:(j + 1) * HEAD_DIM]
            o["v"][:, (2 * j + 1) * HEAD_DIM:(2 * j + 2) * HEAD_DIM] = ones
    if "u" in o:
        for c in range(3):
            o["u"][:, c * HY_WIDTH:(c + 1) * HY_WIDTH] = _dot(
                hb, w_ref[:, COL_U + c * HY_WIDTH:COL_U + (c + 1) * HY_WIDTH])
    if "g" in o:
        for c in range(4):
            acc = _dot(hb, w_ref[:, COL_G + c * 512:COL_G + (c + 1) * 512])
            o["g"][:, c * 512:(c + 1) * 512] = jax.nn.sigmoid(acc).astype(BF16)


_SECTION_SHAPES = {"q": (ATT_WIDTH, BF16), "k": (KV_WIDTH, BF16), "v": (2 * KV_WIDTH, BF16),
                   "u": (3 * HY_WIDTH, F32), "g": (2 * D_MODEL, BF16)}


def _inproj(x2d, seq_len, sh, sc, nw, w_bf16, qn, kn, cos2, sin2, sections):
    rows, d = x2d.shape
    tm = min(ROW_TILE, seq_len)
    tiles_per_seq = seq_len // tm
    nb = sh.shape[0]
    mod_idx = (lambda i: (i // tiles_per_seq, 0, 0)) if nb > 1 else (lambda i: (0, 0, 0))
    const2 = lambda i: (0, 0)
    out_shape = [jax.ShapeDtypeStruct((rows, _SECTION_SHAPES[s][0]), _SECTION_SHAPES[s][1])
                 for s in sections]
    out_specs = [pl.BlockSpec((tm, _SECTION_SHAPES[s][0]), lambda i: (i, 0)) for s in sections]
    return pl.pallas_call(
        functools.partial(_inproj_kernel, sections=tuple(sections)),
        grid=(rows // tm,),
        in_specs=[pl.BlockSpec((tm, d), lambda i: (i, 0)),
                  pl.BlockSpec((None, 1, d), mod_idx),
                  pl.BlockSpec((None, 1, d), mod_idx),
                  pl.BlockSpec((1, d), const2),
                  pl.BlockSpec((d, IN_WIDTH), const2),
                  pl.BlockSpec((1, HEAD_DIM), const2),
                  pl.BlockSpec((1, HEAD_DIM), const2),
                  pl.BlockSpec((tm, HEAD_DIM), lambda i: (i % tiles_per_seq, 0)),
                  pl.BlockSpec((tm, HEAD_DIM), lambda i: (i % tiles_per_seq, 0))],
        out_specs=out_specs,
        out_shape=out_shape,
        compiler_params=_params("parallel"),
        name="inproj",
    )(x2d, sh, sc, nw, w_bf16, qn, kn, cos2, sin2)


def _attn_kernel(q_ref, *refs, n_kv_sets):
    kv = refs[:2 * n_kv_sets]
    o_ref = refs[2 * n_kv_sets]
    nt = (((1,), (1,)), ((), ()))
    for g in range(N_HEADS // N_KV_HEADS):
        q = q_ref[:, g * HEAD_DIM:(g + 1) * HEAD_DIM]
        m = None
        acc = None
        for s_idx in range(n_kv_sets):
            k_ref, v_ref = kv[2 * s_idx], kv[2 * s_idx + 1]
            lk = k_ref.shape[0]
            for c0 in range(0, lk, ATT_KEY_CHUNK):
                c1 = min(c0 + ATT_KEY_CHUNK, lk)
                s = lax.dot_general(q, k_ref[c0:c1, :], nt, preferred_element_type=F32)
                mc = jnp.max(s, axis=1, keepdims=True)
                if m is None:
                    m = mc
                    acc = _dot(jnp.exp2(s - m).astype(BF16), v_ref[c0:c1, :])
                else:
                    m_new = jnp.maximum(m, mc)
                    acc = jnp.exp2(m - m_new) * acc + _dot(jnp.exp2(s - m_new).astype(BF16),
                                                           v_ref[c0:c1, :])
                    m = m_new
        o_ref[:, g * HEAD_DIM:(g + 1) * HEAD_DIM] = (
            acc[:, :HEAD_DIM] / acc[:, HEAD_DIM:]).astype(o_ref.dtype)


def _attention(q, kv_sets, batch, seq_len):
    tq = min(ATT_Q_TILE, seq_len)
    nq = seq_len // tq
    grp = (N_HEADS // N_KV_HEADS) * HEAD_DIM
    in_specs = [pl.BlockSpec((tq, grp), lambda b, j, i: (b * nq + i, j))]
    args = [q]
    for k, v, lk in kv_sets:
        in_specs += [pl.BlockSpec((lk, HEAD_DIM), lambda b, j, i: (b, j)),
                     pl.BlockSpec((lk, 2 * HEAD_DIM), lambda b, j, i: (b, j))]
        args += [k, v]
    return pl.pallas_call(
        functools.partial(_attn_kernel, n_kv_sets=len(kv_sets)),
        grid=(batch, N_KV_HEADS, nq),
        in_specs=in_specs,
        out_specs=pl.BlockSpec((tq, grp), lambda b, j, i: (b * nq + i, j)),
        out_shape=jax.ShapeDtypeStruct(q.shape, BF16),
        compiler_params=_params("parallel", "parallel", "arbitrary"),
        name="attention",
    )(*args)


def _hy_filter_kernel(z_ref, t_ref, w1_ref, b1_ref, f1_ref, w2_ref, b2_ref, f2_ref, w3_ref,
                      dl_ref, sp_ref, sm_ref, nyq_ref):
    i = pl.program_id(0)
    tl = z_ref.shape[0]
    h = jnp.sin(f1_ref[...] * (_dot3(z_ref[...], w1_ref[...]) + b1_ref[...]))
    h = jnp.sin(f2_ref[...] * (_dot3(h, w2_ref[...]) + b2_ref[...]))
    h = _dot3(h, w3_ref[...])
    window = jnp.exp(-t_ref[...] * dl_ref[...])
    row = i * tl + lax.broadcasted_iota(jnp.int32, (tl, HY_WIDTH), 0)
    alt = (1 - 2 * (row & 1)).astype(F32)

    @pl.when(i == 0)
    def _():
        nyq_ref[...] = jnp.zeros_like(nyq_ref)

    for o in range(HY_ORDER):
        hf = h[:, (2 * o) * HY_WIDTH:(2 * o + 1) * HY_WIDTH] * window
        hb = h[:, (2 * o + 1) * HY_WIDTH:(2 * o + 2) * HY_WIDTH] * window
        hb = jnp.where(row == 0, 0.0, hb)
        plus = hf + hb
        sp_ref[:, o * HY_WIDTH:(o + 1) * HY_WIDTH] = plus.astype(BF16)
        sm_ref[:, o * HY_WIDTH:(o + 1) * HY_WIDTH] = (hf - hb).astype(BF16)
        nyq_ref[:, o * HY_WIDTH:(o + 1) * HY_WIDTH] += jnp.sum(alt * plus, axis=0, keepdims=True)


def _hy_filter(seq_len, pe_w1, pe_b1, freq1, pe_w2, pe_b2, freq2, pe_w3):
    t01 = np.linspace(0.0, 1.0, seq_len)[:, None]
    pos = np.arange(seq_len, dtype=np.float64)[:, None]
    bands = np.linspace(1e-4, HY_BANDS - 1, HY_BANDS)[None, :]
    f = 2.0 * math.pi * pos * bands / seq_len
    z = np.zeros((seq_len, HY_PE_PAD), np.float32)
    z[:, :HY_PE_DIM] = np.concatenate([t01, np.cos(f), -np.sin(f)], axis=-1)
    max_decay = math.log(HY_DECAY_TARGET) / HY_FAST_DECAY
    min_decay = math.log(HY_DECAY_TARGET) / HY_SLOW_DECAY
    deltas = np.abs(np.linspace(min_decay, max_decay, HY_WIDTH))[None, :].astype(np.float32)
    w1p = jnp.zeros((HY_PE_PAD, HY_FILTER_HIDDEN), F32).at[:HY_PE_DIM].set(pe_w1)
    tl = min(512, seq_len)
    hid = HY_FILTER_HIDDEN
    n_out = HY_ORDER * HY_WIDTH
    c2 = lambda i: (0, 0)
    return pl.pallas_call(
        _hy_filter_kernel,
        grid=(seq_len // tl,),
        in_specs=[pl.BlockSpec((tl, HY_PE_PAD), lambda i: (i, 0)),
                  pl.BlockSpec((tl, 1), lambda i: (i, 0)),
                  pl.BlockSpec((HY_PE_PAD, hid), c2), pl.BlockSpec((1, hid), c2),
                  pl.BlockSpec((1, hid), c2), pl.BlockSpec((hid, hid), c2),
                  pl.BlockSpec((1, hid), c2), pl.BlockSpec((1, hid), c2),
                  pl.BlockSpec((hid, 2 * n_out), c2), pl.BlockSpec((1, HY_WIDTH), c2)],
        out_specs=[pl.BlockSpec((tl, n_out), lambda i: (i, 0)),
                   pl.BlockSpec((tl, n_out), lambda i: (i, 0)),
                   pl.BlockSpec((1, n_out), c2)],
        out_shape=[jax.ShapeDtypeStruct((seq_len, n_out), BF16),
                   jax.ShapeDtypeStruct((seq_len, n_out), BF16),
                   jax.ShapeDtypeStruct((1, n_out), F32)],
        compiler_params=_params("arbitrary"),
        name="hy_filter",
    )(jnp.asarray(z), jnp.asarray(t01.astype(np.float32)), w1p, pe_b1.reshape(1, hid),
      freq1.reshape(1, hid), pe_w2, pe_b2.reshape(1, hid), freq2.reshape(1, hid), pe_w3,
      jnp.asarray(deltas))


def _dft_kernel(ca_ref, sa_ref, cb_ref, sb_ref, alt_ref, wc_ref, cm_ref, sm_ref, *inverse_refs):
    i = pl.program_id(0)
    ca = ca_ref[...]
    sa = sa_ref[...]
    cb = cb_ref[...]
    sb = sb_ref[...]
    c = ca * cb - sa * sb
    s = sa * cb + ca * sb
    rows = lax.broadcasted_iota(jnp.int32, c.shape, 0)
    cols = lax.broadcasted_iota(jnp.int32, c.shape, 1)
    wc = wc_ref[...]
    cm_ref[...] = c.astype(BF16)
    sm_ref[...] = jnp.where((rows == 0) & (i == 0), alt_ref[...], s).astype(BF16)
    if inverse_refs:
        ci_ref, si_ref = inverse_refs
        alt_t = (1 - 2 * (rows & 1)).astype(F32)
        ci_ref[...] = (c * wc).astype(BF16)
        si_ref[...] = (jnp.where(cols == 0, alt_t, s) * wc).astype(BF16)


def _dft_matrices(seq_len, inverse):
    n = 2 * seq_len
    idx = np.arange(seq_len, dtype=np.int64)[None, :]
    r1 = np.arange(seq_len // DFT_ROWS, dtype=np.int64)[:, None] * DFT_ROWS
    r0 = np.arange(DFT_ROWS, dtype=np.int64)[:, None]
    ang_a = ((r1 * idx) % n).astype(np.float64) * (2.0 * math.pi / n)
    ang_b = ((r0 * idx) % n).astype(np.float64) * (2.0 * math.pi / n)
    tab = lambda a: jnp.asarray(a.astype(np.float32))
    ca = tab(np.cos(ang_a)).reshape(-1, 1, seq_len)
    sa = tab(np.sin(ang_a)).reshape(-1, 1, seq_len)
    alt = tab(1.0 - 2.0 * (idx % 2))
    wc = tab(np.where(idx == 0, 1.0, 2.0) / n)
    row_blk = pl.BlockSpec((None, 1, seq_len), lambda i: (i, 0, 0))
    full = lambda r: pl.BlockSpec((r, seq_len), lambda i: (0, 0))
    out_blk = pl.BlockSpec((DFT_ROWS, seq_len), lambda i: (i, 0))
    n_out = 4 if inverse else 2
    mats = pl.pallas_call(
        _dft_kernel,
        grid=(seq_len // DFT_ROWS,),
        in_specs=[row_blk, row_blk, full(DFT_ROWS), full(DFT_ROWS), full(1), full(1)],
        out_specs=[out_blk] * n_out,
        out_shape=[jax.ShapeDtypeStruct((seq_len, seq_len), BF16)] * n_out,
        compiler_params=_params("parallel"),
        name="dft_matrices",
    )(ca, sa, tab(np.cos(ang_b)), tab(np.sin(ang_b)), alt, wc)
    if not inverse:
        return tuple(mats)
    tw = np.arange(seq_len, dtype=np.float64)[:, None] * (2.0 * math.pi / (2 * n))
    rep = lambda a: tab(np.broadcast_to(a, (seq_len, LANES)))
    return tuple(mats) + (rep(np.cos(tw)), rep(np.sin(tw)))


def _hy_spec_kernel(cm_ref, sm_ref, sp_ref, smn_ref, nyq_ref, kp_ref, kq_ref):
    i = pl.program_id(1)
    kp_ref[...] = _dot(cm_ref[...], sp_ref[...])
    kq = _dot(sm_ref[...], smn_ref[...])
    rows = lax.broadcasted_iota(jnp.int32, kq.shape, 0)
    kq_ref[...] = jnp.where((rows == 0) & (i == 0), nyq_ref[...], kq)


def _hy_spectrum(cm, sm, splus, sminus, nyq):
    seq_len = cm.shape[0]
    tf = min(512, seq_len)
    mat = pl.BlockSpec((tf, seq_len), lambda o, i: (i, 0))
    sig = pl.BlockSpec((seq_len, HY_WIDTH), lambda o, i: (0, o))
    out = pl.BlockSpec((tf, HY_WIDTH), lambda o, i: (i, o))
    return pl.pallas_call(
        _hy_spec_kernel,
        grid=(HY_ORDER, seq_len // tf),
        in_specs=[mat, mat, sig, sig, pl.BlockSpec((1, HY_WIDTH), lambda o, i: (0, o))],
        out_specs=[out, out],
        out_shape=[jax.ShapeDtypeStruct((seq_len, HY_ORDER * HY_WIDTH), F32)] * 2,
        compiler_params=_params("parallel", "arbitrary"),
        name="hy_spectrum",
    )(cm, sm, splus, sminus, nyq)


def _hy_sconv_kernel(u_ref, w_ref, b_ref, uc_ref, vb_ref, *, v_first_block):
    j = pl.program_id(1)
    x = u_ref[...]
    n = x.shape[0]
    rows = lax.broadcasted_iota(jnp.int32, x.shape, 0)
    prev = jnp.where(rows == 0, 0.0, pltpu.roll(x, 1, 0))
    nxt = jnp.where(rows == n - 1, 0.0, pltpu.roll(x, n - 1, 0))
    y = prev * w_ref[0:1, :] + x * w_ref[1:2, :] + nxt * w_ref[2:3, :] + b_ref[...]
    uc_ref[...] = y

    @pl.when(j >= v_first_block)
    def _():
        vb_ref[...] = y.astype(BF16)


def _hy_sconv(u, batch, seq_len, conv_w, conv_b):
    ch = u.shape[1]
    tc = 256
    v0 = (2 * HY_WIDTH) // tc
    u3 = u.reshape(batch, seq_len, ch)
    uc, vb = pl.pallas_call(
        functools.partial(_hy_sconv_kernel, v_first_block=v0),
        grid=(batch, ch // tc),
        in_specs=[pl.BlockSpec((None, seq_len, tc), lambda b, j: (b, 0, j)),
                  pl.BlockSpec((3, tc), lambda b, j: (0, j)),
                  pl.BlockSpec((1, tc), lambda b, j: (0, j))],
        out_specs=[pl.BlockSpec((None, seq_len, tc), lambda b, j: (b, 0, j)),
                   pl.BlockSpec((None, seq_len, tc), lambda b, j: (b, 0, jnp.maximum(j - v0, 0)))],
        out_shape=[jax.ShapeDtypeStruct((batch, seq_len, ch), F32),
                   jax.ShapeDtypeStruct((batch, seq_len, HY_WIDTH), BF16)],
        compiler_params=_params("parallel", "arbitrary"),
        name="hy_sconv",
    )(u3, conv_w, conv_b.reshape(1, ch))
    return uc, vb


def _hy_fwd_kernel(cm_ref, sm_ref, tc_ref, ts_ref, v_ref, klr_ref, kli_ref, khr_ref, khi_ref,
                   zp_ref, zq_ref):
    i = pl.program_id(0)
    w = v_ref.shape[1] // 2
    v = v_ref[...]
    p = _dot(cm_ref[...], v)
    q = _dot(sm_ref[...], v)
    pe, po, qe, qo = p[:, :w], p[:, w:], q[:, :w], q[:, w:]
    reps = w // LANES
    tc = jnp.concatenate([tc_ref[...]] * reps, axis=1)
    ts = jnp.concatenate([ts_ref[...]] * reps, axis=1)
    rows = lax.broadcasted_iota(jnp.int32, pe.shape, 0)
    edge = (rows == 0) & (i == 0)

    a = tc * po - ts * qo
    b = tc * qo + ts * po
    xlr = pe + a
    xhr = pe - a
    xli = jnp.where(edge, qe, -(qe + b))
    xhi = jnp.where(edge, -qo, qe - b)

    klr, kli, khr, khi = klr_ref[...], kli_ref[...], khr_ref[...], khi_ref[...]
    ylr = xlr * klr - jnp.where(edge, 0.0, xli * kli)
    yhr = xhr * khr - jnp.where(edge, 0.0, xhi * khi)
    yli = jnp.where(edge, xli * kli - xhi * khi, xlr * kli + xli * klr)
    yhi = jnp.where(edge, xli * khi + xhi * kli, xhr * khi + xhi * khr)

    dr = ylr - yhr
    di = yli + yhi
    zp_ref[:, :w] = (0.5 * (ylr + yhr)).astype(BF16)
    zp_ref[:, w:] = (0.5 * (tc * dr - ts * di)).astype(BF16)
    zq_ref[:, :w] = jnp.where(edge, yli, -0.5 * (yli - yhi)).astype(BF16)
    zq_ref[:, w:] = jnp.where(edge, -yhi, -0.5 * (tc * di + ts * dr)).astype(BF16)


def _hy_forward(dfth, vb, kpack, order):
    cm, sm, _, _, tc, ts = dfth
    batch, seq_len, w = vb.shape
    half = seq_len // 2
    tf = min(512, half)
    mat = pl.BlockSpec((tf, half), lambda i, b: (i, 0))
    tw = pl.BlockSpec((tf, LANES), lambda i, b: (i, 0))
    spec = pl.BlockSpec((tf, w), lambda i, b: (i, order))
    out = pl.BlockSpec((None, tf, 2 * w), lambda i, b: (b, i, 0))
    return pl.pallas_call(
        _hy_fwd_kernel,
        grid=(half // tf, batch),
        in_specs=[mat, mat, tw, tw,
                  pl.BlockSpec((None, half, 2 * w), lambda i, b: (b, 0, 0)),
                  spec, spec, spec, spec],
        out_specs=[out, out],
        out_shape=[jax.ShapeDtypeStruct((batch, half, 2 * w), BF16)] * 2,
        compiler_params=_params("parallel", "arbitrary"),
        name="hy_forward",
    )(cm, sm, tc, ts, vb.reshape(batch, half, 2 * w), *kpack)


def _hy_inv_kernel(ci_ref, si_ref, zp_ref, zq_ref, ae_ref, ao_ref, ce_ref, co_ref, skip_ref, *outs):
    y = _dot(ci_ref[...], zp_ref[...]) + _dot(si_ref[...], zq_ref[...])
    a = jnp.concatenate([ae_ref[...], ao_ref[...]], axis=1)
    c = jnp.concatenate([ce_ref[...], co_ref[...]], axis=1)
    skip = jnp.concatenate([skip_ref[...]] * 2, axis=1)
    z = a * (y + skip * c)
    for o_ref in outs:
        o_ref[...] = z.astype(o_ref.dtype)


def _hy_inverse(dfth, zp, zq, a_arr, a_blk, c_arr, c_blk, skip, out_dtypes):
    _, _, ci, si, _, _ = dfth
    batch, half, w2 = zp.shape
    w = w2 // 2
    tt = min(512, half)
    mat = pl.BlockSpec((tt, half), lambda i, b: (i, 0))
    sig = pl.BlockSpec((None, half, w2), lambda i, b: (b, 0, 0))
    out = pl.BlockSpec((None, tt, w2), lambda i, b: (b, i, 0))

    def pair(arr, blk):
        nblk = arr.shape[2] // w
        a2 = arr.reshape(batch, half, 2 * arr.shape[2])
        return a2, [pl.BlockSpec((None, tt, w), lambda i, b: (b, i, blk)),
                    pl.BlockSpec((None, tt, w), lambda i, b: (b, i, nblk + blk))]

    a2, a_specs = pair(a_arr, a_blk)
    c2, c_specs = pair(c_arr, c_blk)
    outs = pl.pallas_call(
        _hy_inv_kernel,
        grid=(half // tt, batch),
        in_specs=[mat, mat, sig, sig] + a_specs + c_specs
                 + [pl.BlockSpec((1, w), lambda i, b: (0, 0))],
        out_specs=[out] * len(out_dtypes),
        out_shape=[jax.ShapeDtypeStruct((batch, half, w2), dt) for dt in out_dtypes],
        compiler_params=_params("parallel", "arbitrary"),
        name="hy_inverse",
    )(ci, si, zp, zq, a2, a2, c2, c2, skip)
    return [o.reshape(batch, 2 * half, w) for o in outs]


def _pack_spectrum(kp, kq):
    half = kp.shape[0] // 2
    lo_re = kp[:half]
    lo_im = jnp.concatenate([kp[half:half + 1], -kq[1:half]], axis=0)
    hi_re = jnp.concatenate([kq[0:1], jnp.flip(kp[half + 1:], axis=0)], axis=0)
    hi_im = jnp.concatenate([-kq[half:half + 1], -jnp.flip(kq[half + 1:], axis=0)], axis=0)
    return lo_re, lo_im, hi_re, hi_im


def _hyena(u, batch, seq_len, dfth, kpack, conv_w, conv_b, skip):
    uc, vb = _hy_sconv(u, batch, seq_len, conv_w, conv_b)
    zp, zq = _hy_forward(dfth, vb, kpack, 0)
    z, zb = _hy_inverse(dfth, zp, zq, uc, 0, uc, 2, skip[0:1], (F32, BF16))
    zp, zq = _hy_forward(dfth, zb, kpack, 1)
    (y,) = _hy_inverse(dfth, zp, zq, uc, 1, z, 0, skip[1:2], (BF16,))
    return y.reshape(batch * seq_len, HY_WIDTH)


def _merge_kernel(ya_ref, yh_ref, g_ref, x_ref, g1_ref, sh_ref, sc_ref, nw_ref,
                  wa_ref, wh_ref, wo_ref, rh_ref, rl_ref, *rest):
    xo_ref, hx_ref, lg_ref = rest[-3:]
    a = _dot(ya_ref[...], wa_ref[...])
    h = _dot(yh_ref[...], wh_ref[...])
    m = g_ref[:, :D_MODEL].astype(F32) * a + g_ref[:, D_MODEL:].astype(F32) * h
    xn = x_ref[...] + g1_ref[...] * _dot(m.astype(BF16), wo_ref[...])
    xo_ref[...] = xn
    hx = _rms(xn) * nw_ref[...]
    hx = hx * (1.0 + sc_ref[...]) + sh_ref[...]
    hx_ref[...] = _rows_to_tiles(hx)
    hh, hl = _split(hx)
    nt = (((1,), (1,)), ((), ()))
    dn = lambda a, b: lax.dot_general(a, b, nt, preferred_element_type=F32)
    lg_ref[...] = dn(rh_ref[...], hh) + (dn(rh_ref[...], hl) + dn(rl_ref[...], hh))


def _merge(ya, yh, g, x2d, seq_len, g1, sh2, sc2, nw, wa, wh, wo, r_hi, r_lo,
           total_rows, row_offset, prev=None):
    rows, d = x2d.shape
    tm = min(ROW_TILE, seq_len)
    tiles_per_seq = seq_len // tm
    off = row_offset // tm
    nb = g1.shape[0]
    mod_idx = (lambda i: (i // tiles_per_seq, 0, 0)) if nb > 1 else (lambda i: (0, 0, 0))
    c2 = lambda i: (0, 0)
    row = lambda w: pl.BlockSpec((tm, w), lambda i: (i, 0))
    in_specs = [row(ATT_WIDTH), row(HY_WIDTH), row(2 * d), row(d),
                pl.BlockSpec((None, 1, d), mod_idx), pl.BlockSpec((None, 1, d), mod_idx),
                pl.BlockSpec((None, 1, d), mod_idx), pl.BlockSpec((1, d), c2),
                pl.BlockSpec((ATT_WIDTH, d), c2), pl.BlockSpec((HY_WIDTH, d), c2),
                pl.BlockSpec((d, d), c2), pl.BlockSpec((N_EXPERTS, d), c2),
                pl.BlockSpec((N_EXPERTS, d), c2)]
    args = [ya, yh, g, x2d, g1, sh2, sc2, nw, wa, wh, wo, r_hi, r_lo]
    aliases = {}
    if prev is not None:
        in_specs += [pl.BlockSpec(memory_space=pl.ANY)] * 2
        aliases = {len(args): 1, len(args) + 1: 2}
        args += list(prev)
    return pl.pallas_call(
        _merge_kernel,
        grid=(rows // tm,),
        in_specs=in_specs,
        out_specs=[row(d), pl.BlockSpec((tm, d // LANES, LANES), lambda i: (i + off, 0, 0)),
                   pl.BlockSpec((N_EXPERTS, tm), lambda i: (0, i + off))],
        out_shape=[jax.ShapeDtypeStruct((rows, d), F32),
                   jax.ShapeDtypeStruct((total_rows, d // LANES, LANES), F32),
                   jax.ShapeDtypeStruct((N_EXPERTS, total_rows), F32)],
        input_output_aliases=aliases,
        compiler_params=_params("arbitrary"),
        name="merge",
    )(*args)


def _route_kernel(lg_ref, bias_ref, idx_ref, gate_ref, rank_ref, cnt_ref, run_ref):
    step = pl.program_id(0)
    tn = lg_ref.shape[2]
    shape = (N_GROUPS, EXPERTS_PER_GROUP, tn)
    s = jax.nn.sigmoid(lg_ref[...])
    b = s + bias_ref[...]
    mem = lax.broadcasted_iota(jnp.int32, shape, 1).astype(F32)
    grp = lax.broadcasted_iota(jnp.int32, (N_GROUPS, 1, tn), 0).astype(F32)
    big = float(N_EXPERTS)

    m1 = jnp.max(b, axis=1, keepdims=True)
    i1 = jnp.min(jnp.where(b == m1, mem, big), axis=1, keepdims=True)
    b2 = jnp.where(mem == i1, -jnp.inf, b)
    m2 = jnp.max(b2, axis=1, keepdims=True)
    i2 = jnp.min(jnp.where(b2 == m2, mem, big), axis=1, keepdims=True)
    gs = m1 + m2
    gmax = jnp.max(gs, axis=0, keepdims=True)
    gsel = jnp.min(jnp.where(gs == gmax, grp, big), axis=0, keepdims=True)
    selg = grp == gsel

    @pl.when(step == 0)
    def _():
        run_ref[...] = jnp.zeros_like(run_ref)

    tri = (lax.broadcasted_iota(jnp.int32, (tn, tn), 0)
           < lax.broadcasted_iota(jnp.int32, (tn, tn), 1)).astype(BF16)
    run = run_ref[...]
    ws = []
    for k, ik in enumerate((i1, i2)):
        hit = selg & (mem == ik)
        ws.append(jnp.sum(jnp.where(hit, s, 0.0), axis=(0, 1), keepdims=True))
        e_loc = jnp.sum(jnp.where(selg, ik, 0.0), axis=0, keepdims=True)
        idx_ref[k:k + 1, :] = (gsel * EXPERTS_PER_GROUP + e_loc).reshape(1, tn).astype(jnp.int32)
        oh = jnp.where(hit, 1.0, 0.0).reshape(N_EXPERTS, tn)
        before = run + _dot(oh.astype(BF16), tri)
        rank_ref[k:k + 1, :] = jnp.sum(oh * before, axis=0, keepdims=True).astype(jnp.int32)
        run = run + jnp.sum(oh, axis=1, keepdims=True)
    run_ref[...] = run
    cnt_ref[...] = run
    tot = ws[0] + ws[1]
    for k in range(TOP_K):
        gate_ref[k:k + 1, :] = (ws[k] / tot).reshape(1, tn)


def _route(logits_t, router_b):
    n_tok = logits_t.shape[1]
    tn = 512
    lg3 = logits_t.reshape(N_GROUPS, EXPERTS_PER_GROUP, n_tok)
    row2 = pl.BlockSpec((TOP_K, tn), lambda i: (0, i))
    return pl.pallas_call(
        _route_kernel,
        grid=(n_tok // tn,),
        in_specs=[pl.BlockSpec((N_GROUPS, EXPERTS_PER_GROUP, tn), lambda i: (0, 0, i)),
                  pl.BlockSpec((N_GROUPS, EXPERTS_PER_GROUP, 1), lambda i: (0, 0, 0))],
        out_specs=[row2, row2, row2, pl.BlockSpec((N_EXPERTS, 1), lambda i: (0, 0))],
        out_shape=[jax.ShapeDtypeStruct((TOP_K, n_tok), jnp.int32),
                   jax.ShapeDtypeStruct((TOP_K, n_tok), F32),
                   jax.ShapeDtypeStruct((TOP_K, n_tok), jnp.int32),
                   jax.ShapeDtypeStruct((N_EXPERTS, 1), F32)],
        scratch_shapes=[pltpu.VMEM((N_EXPERTS, 1), F32)],
        compiler_params=_params("arbitrary"),
        name="moe_route",
    )(lg3, router_b.astype(F32).reshape(N_GROUPS, EXPERTS_PER_GROUP, 1))


def _slot_kernel(idx_ref, rank_ref, start_ref, dest_ref):
    tn = idx_ref.shape[1]
    e = lax.broadcasted_iota(jnp.int32, (N_EXPERTS, tn), 0)
    for k in range(TOP_K):
        base = jnp.sum(jnp.where(e == idx_ref[k:k + 1, :], start_ref[...], 0.0), axis=0,
                       keepdims=True)
        dest_ref[k:k + 1, :] = base.astype(jnp.int32) + rank_ref[k:k + 1, :]


def _slots(idx, rank, pad_start):
    n_tok = idx.shape[1]
    tn = 512
    row2 = pl.BlockSpec((TOP_K, tn), lambda i: (0, i))
    return pl.pallas_call(
        _slot_kernel,
        grid=(n_tok // tn,),
        in_specs=[row2, row2, pl.BlockSpec((N_EXPERTS, 1), lambda i: (0, 0))],
        out_specs=row2,
        out_shape=jax.ShapeDtypeStruct((TOP_K, n_tok), jnp.int32),
        compiler_params=_params("parallel"),
        name="moe_slots",
    )(idx, rank, pad_start.astype(F32).reshape(N_EXPERTS, 1))


def _layout(counts, n_blocks):
    counts = counts.reshape(N_EXPERTS).astype(jnp.int32)
    padded = (counts + MOE_ROWS - 1) // MOE_ROWS * MOE_ROWS
    pad_end = jnp.cumsum(padded)
    pad_start = pad_end - padded
    blk_row = jnp.arange(n_blocks, dtype=jnp.int32)[:, None] * MOE_ROWS
    block_e = jnp.minimum(jnp.sum((pad_end[None, :] <= blk_row).astype(jnp.int32), axis=1),
                          N_EXPERTS - 1)
    prev_e = jnp.concatenate([jnp.full((1,), -1, jnp.int32), block_e[:-1]])
    block_new = (block_e != prev_e).astype(jnp.int32)
    n_used = (pad_end[-1] // MOE_ROWS).astype(jnp.int32).reshape(1)
    return pad_start, pad_start + counts, pad_end, block_e, block_new, n_used


_PAD_CHUNKS = tuple(2 ** p for p in range(int(math.log2(MOE_ROWS)) - 1, -1, -1))


def _scatter_kernel(dest_ref, lo_ref, hi_ref, x_ref, o_hbm, zeros, sem, zsem, *, n_tok):
    i = pl.program_id(0)
    tm = x_ref.shape[0]

    def pad_copies(act):
        def per_expert(e, carry):
            off = lo_ref[e]
            n = hi_ref[e] - off
            for c in _PAD_CHUNKS:
                @pl.when((n & c) != 0)
                def _():
                    act(pltpu.make_async_copy(zeros.at[pl.ds(0, c)], o_hbm.at[pl.ds(off, c)], zsem))
                off = off + (n & c)
            return carry
        lax.fori_loop(0, N_EXPERTS, per_expert, 0)

    @pl.when(i == 0)
    def _():
        zeros[...] = jnp.zeros_like(zeros)
        pad_copies(lambda cp: cp.start())

    def issue(r, carry):
        for k in range(TOP_K):
            pltpu.make_async_copy(x_ref.at[r], o_hbm.at[dest_ref[k * n_tok + i * tm + r]], sem).start()
        return carry
    lax.fori_loop(0, tm, issue, 0, unroll=8)
    for k in range(TOP_K):
        pltpu.make_async_copy(x_ref, o_hbm.at[pl.ds(0, tm)], sem).wait()

    @pl.when(i == 0)
    def _():
        pad_copies(lambda cp: cp.wait())


def _scatter_rows(dest_flat, pad_lo, pad_hi, h, n_blocks):
    n_tok = h.shape[0]
    tile = h.shape[1:]
    tm = 256
    return pl.pallas_call(
        functools.partial(_scatter_kernel, n_tok=n_tok),
        grid_spec=pltpu.PrefetchScalarGridSpec(
            num_scalar_prefetch=3,
            grid=(n_tok // tm,),
            in_specs=[pl.BlockSpec((tm,) + tile, lambda i, *_: (i, 0, 0))],
            out_specs=pl.BlockSpec(memory_space=pl.ANY),
            scratch_shapes=[pltpu.VMEM((MOE_ROWS // 2,) + tile, h.dtype),
                            pltpu.SemaphoreType.DMA(()), pltpu.SemaphoreType.DMA(())]),
        out_shape=jax.ShapeDtypeStruct((n_blocks * MOE_ROWS,) + tile, h.dtype),
        compiler_params=pltpu.CompilerParams(dimension_semantics=("arbitrary",),
                                             vmem_limit_bytes=VMEM_LIMIT, has_side_effects=True),
        name="moe_scatter",
    )(dest_flat, pad_lo, pad_hi, h)


def _expert_kernel(be_ref, new_ref, nused_ref, x_ref, w1_ref, w3_ref, w2_ref, y_ref,
                   w1b, w3b, w2b):
    i = pl.program_id(0)

    @pl.when(i < nused_ref[0])
    def _():
        @pl.when(new_ref[i] == 1)
        def _():
            w1b[...] = w1_ref[...].astype(BF16)
            w3b[...] = w3_ref[...].astype(BF16)
            w2b[...] = w2_ref[...].astype(BF16)

        x = _tiles_to_rows(x_ref[...]).astype(BF16)
        a = _dot(x, w1b[...])
        b = _dot(x, w3b[...])
        h = (a * jax.nn.sigmoid(a)) * b
        y_ref[...] = _rows_to_tiles(_dot(h.astype(BF16), w2b[...]))


def _experts(block_e, block_new, n_used, xs, w1, w3, w2, layer):
    n_slots = xs.shape[0]
    tile = xs.shape[1:]
    d = tile[0] * tile[1]
    n_blocks = n_slots // MOE_ROWS
    e_dim = w1.shape[3]
    blk = lambda i, be, bn, nu: (jnp.minimum(i, nu[0] - 1), 0, 0)
    wsel = lambda i, be, bn, nu: (layer, be[jnp.minimum(i, nu[0] - 1)], 0, 0)
    return pl.pallas_call(
        _expert_kernel,
        grid_spec=pltpu.PrefetchScalarGridSpec(
            num_scalar_prefetch=3,
            grid=(n_blocks,),
            in_specs=[pl.BlockSpec((MOE_ROWS,) + tile, blk),
                      pl.BlockSpec((None, None, d, e_dim), wsel),
                      pl.BlockSpec((None, None, d, e_dim), wsel),
                      pl.BlockSpec((None, None, e_dim, d), wsel)],
            out_specs=pl.BlockSpec((MOE_ROWS,) + tile, blk),
            scratch_shapes=[pltpu.VMEM((d, e_dim), BF16), pltpu.VMEM((d, e_dim), BF16),
                            pltpu.VMEM((e_dim, d), BF16)]),
        out_shape=jax.ShapeDtypeStruct(xs.shape, F32),
        compiler_params=_params("arbitrary"),
        name="moe_experts",
    )(block_e, block_new, n_used, xs, w1, w3, w2)


def _combine_kernel(pos_ref, x_ref, gate_ref, g2_ref, fw_ref, y_hbm, o_ref, buf, sems, *,
                    tok_offset, n_tok, final_norm):
    i = pl.program_id(0)
    n = pl.num_programs(0)
    tm = x_ref.shape[0]

    def issue(step, slot):
        def body(r, carry):
            tok = tok_offset + step * tm + r
            for k in range(TOP_K):
                pltpu.make_async_copy(y_hbm.at[pos_ref[k * n_tok + tok]],
                                      buf.at[slot, k, r], sems.at[slot]).start()
            return carry
        lax.fori_loop(0, tm, body, 0, unroll=8)

    @pl.when(i == 0)
    def _():
        issue(0, 0)

    @pl.when(i + 1 < n)
    def _():
        issue(i + 1, (i + 1) % 2)

    slot = i % 2
    pltpu.make_async_copy(buf.at[slot], buf.at[slot], sems.at[slot]).wait()
    gate = gate_ref[...]
    f = (_tiles_to_rows(buf[slot, 0]) * gate[:, 0:1] + _tiles_to_rows(buf[slot, 1]) * gate[:, 1:2])
    out = x_ref[...] + g2_ref[...] * f
    if final_norm:
        out = _rms(out) * fw_ref[...]
    o_ref[...] = out


def _combine(pos, x2d, seq_len, gate, g2, fw, y, tok_offset, final_norm):
    rows, d = x2d.shape
    n_tok = gate.shape[0]
    tm = min(256, seq_len)
    tiles_per_seq = seq_len // tm
    off = tok_offset // tm
    nb = g2.shape[0]
    mod_idx = ((lambda i, p: (i // tiles_per_seq, 0, 0)) if nb > 1 else (lambda i, p: (0, 0, 0)))
    return pl.pallas_call(
        functools.partial(_combine_kernel, tok_offset=tok_offset, n_tok=n_tok,
                          final_norm=final_norm),
        grid_spec=pltpu.PrefetchScalarGridSpec(
            num_scalar_prefetch=1,
            grid=(rows // tm,),
            in_specs=[pl.BlockSpec((tm, d), lambda i, p: (i, 0)),
                      pl.BlockSpec((tm, TOP_K), lambda i, p: (i + off, 0)),
                      pl.BlockSpec((None, 1, d), mod_idx),
                      pl.BlockSpec((1, d), lambda i, p: (0, 0)),
                      pl.BlockSpec(memory_space=pl.ANY)],
            out_specs=pl.BlockSpec((tm, d), lambda i, p: (i, 0)),
            scratch_shapes=[pltpu.VMEM((2, TOP_K, tm) + y.shape[1:], F32),
                            pltpu.SemaphoreType.DMA((2,))]),
        out_shape=jax.ShapeDtypeStruct((rows, d), F32),
        compiler_params=_params("arbitrary"),
        name="moe_combine",
    )(pos, x2d, gate, g2, fw, y)


def _rope_tables(seq_len):
    rows = seq_len // GRID_W
    row = np.repeat(np.arange(rows), GRID_W).astype(np.float64)
    col = np.tile(np.arange(GRID_W), rows).astype(np.float64)
    n = HEAD_DIM // 4
    inv = (ROPE_THETA ** (-np.arange(n, dtype=np.float32) / n)).astype(np.float64)
    ang = np.concatenate([row[:, None] * inv, col[:, None] * inv], axis=-1)
    cos = np.cos(ang.astype(np.float32).astype(np.float64))
    sin = np.sin(ang.astype(np.float32).astype(np.float64))
    cos2 = np.concatenate([cos, cos], axis=-1).astype(np.float32)
    sin2 = np.concatenate([-sin, sin], axis=-1).astype(np.float32)
    return jnp.asarray(cos2), jnp.asarray(sin2)


def kernel(x, c, ctx, c_ctx, w_ada, b_ada, norm1_w, norm2_w, w_in, q_norm_w, k_norm_w,
           hy_conv_w, hy_conv_b, hy_pe_w1, hy_pe_b1, hy_freq1, hy_pe_w2, hy_pe_b2, hy_freq2,
           hy_pe_w3, hy_skip, w_att_proj, w_hy_proj, w_out, router_w, router_b,
           exp_w1, exp_w3, exp_w2, final_norm_w):
    B, S, D = x.shape
    C = ctx.shape[1]
    depth = w_ada.shape[0]
    n_lat = B * S
    n_ctx = B * C

    cos2, sin2 = _rope_tables(S)
    cos_id = jnp.ones((C, HEAD_DIM), F32)
    sin_id = jnp.zeros((C, HEAD_DIM), F32)
    dft_xf = _dft_matrices(S, False)
    dft_cf = _dft_matrices(C, False)
    dft_x = _dft_matrices(S // 2, True)
    dft_c = _dft_matrices(C // 2, True)

    mod_rows = 16
    cs = jnp.zeros((mod_rows, D), F32).at[:B].set(c).at[B].set(c_ctx)
    mods = _adaln(cs, w_ada, b_ada)

    r_hi = router_w.T.astype(BF16)
    r_lo = (router_w.T - r_hi.astype(F32)).astype(BF16)
    fw = final_norm_w.reshape(1, D)

    x2d = x.reshape(n_lat, D)
    c2d = ctx.reshape(n_ctx, D)
    for i in range(depth):
        last = i == depth - 1
        m_lat = [mods[i, :B, j * D:(j + 1) * D].reshape(B, 1, D) for j in range(6)]
        m_ctx = [mods[i, B:B + 1, j * D:(j + 1) * D].reshape(1, 1, D) for j in range(6)]
        wb = w_in[i].astype(BF16)
        n1 = norm1_w[i].reshape(1, D)
        n2 = norm2_w[i].reshape(1, D)
        qn = q_norm_w[i].reshape(1, HEAD_DIM)
        kn = k_norm_w[i].reshape(1, HEAD_DIM)
        wa = w_att_proj[i].astype(BF16)
        wh = w_hy_proj[i].astype(BF16)
        wo = w_out[i].astype(BF16)

        q, k, v, u, g = _inproj(x2d, S, m_lat[0], m_lat[1], n1, wb, qn, kn, cos2, sin2,
                                ("q", "k", "v", "u", "g"))
        if last:
            kc, vc = _inproj(c2d, C, m_ctx[0], m_ctx[1], n1, wb, qn, kn, cos_id, sin_id, ("k", "v"))
        else:
            qc, kc, vc, uc, gc = _inproj(c2d, C, m_ctx[0], m_ctx[1], n1, wb, qn, kn, cos_id, sin_id,
                                         ("q", "k", "v", "u", "g"))
        ya = _attention(q, [(k, v, S), (kc, vc, C)], B, S)

        filt = (hy_pe_w1[i], hy_pe_b1[i], hy_freq1[i], hy_pe_w2[i], hy_pe_b2[i], hy_freq2[i],
                hy_pe_w3[i])
        sp, sm, nyq = _hy_filter(S, *filt)
        kpack = _pack_spectrum(*_hy_spectrum(dft_xf[0], dft_xf[1], sp, sm, nyq))
        yh = _hyena(u, B, S, dft_x, kpack, hy_conv_w[i], hy_conv_b[i], hy_skip[i])

        n_tok = n_lat if last else n_lat + n_ctx
        x2d, hx, logits = _merge(ya, yh, g, x2d, S, m_lat[2], m_lat[3], m_lat[4], n2,
                                 wa, wh, wo, r_hi, r_lo, n_tok, 0)
        if not last:
            ya_c = _attention(qc, [(kc, vc, C)], B, C)
            sp, sm, nyq = _hy_filter(C, *filt)
            kpack = _pack_spectrum(*_hy_spectrum(dft_cf[0], dft_cf[1], sp, sm, nyq))
            yh_c = _hyena(uc, B, C, dft_c, kpack, hy_conv_w[i], hy_conv_b[i], hy_skip[i])
            c2d, hx, logits = _merge(ya_c, yh_c, gc, c2d, C, m_ctx[2], m_ctx[3], m_ctx[4], n2,
                                     wa, wh, wo, r_hi, r_lo, n_tok, n_lat, prev=(hx, logits))

        idx, gate, rank, counts = _route(logits, router_b)
        n_blocks = -(-(n_tok * TOP_K) // MOE_ROWS) + N_EXPERTS
        pad_start, pad_lo, pad_hi, block_e, block_new, n_used = _layout(counts, n_blocks)
        dest = _slots(idx, rank, pad_start).reshape(TOP_K * n_tok)
        xs = _scatter_rows(dest, pad_lo, pad_hi, hx, n_blocks)
        y = _experts(block_e, block_new, n_used, xs, exp_w1, exp_w3, exp_w2, i)
        gate_t = gate.T
        x2d = _combine(dest, x2d, S, gate_t, m_lat[5], fw, y, 0, last)
        if not last:
            c2d = _combine(dest, c2d, C, gate_t, m_ctx[5], fw, y, n_lat, False)
    return x2d.reshape(B, S, D)
```

```python
import functools
import math

import numpy as np
import jax
import jax.numpy as jnp
from jax import lax
from jax.experimental import pallas as pl
from jax.experimental.pallas import tpu as pltpu

F32 = jnp.float32
BF16 = jnp.bfloat16

D_MODEL = 1024
DEPTH = 2
GRID_W = 64
NORM_EPS = 1e-6
N_HEADS = 8
N_KV_HEADS = 2
HEAD_DIM = 128
ATT_WIDTH = N_HEADS * HEAD_DIM
KV_WIDTH = N_KV_HEADS * HEAD_DIM
ROPE_THETA = 10000.0
ATT_SCALE = HEAD_DIM ** -0.5
LOG2_E = math.log2(math.e)
HY_WIDTH = D_MODEL // 2
HY_ORDER = 2
HY_FILTER_HIDDEN = 64
HY_BANDS = 16
HY_PE_DIM = 1 + 2 * HY_BANDS
HY_PE_PAD = 128
HY_FAST_DECAY = 0.3
HY_SLOW_DECAY = 1.5
HY_DECAY_TARGET = 1e-2
N_EXPERTS = 64
N_GROUPS = 8
EXPERTS_PER_GROUP = N_EXPERTS // N_GROUPS
TOP_K = 2
EXPERT_DIM = 512
IN_WIDTH = ATT_WIDTH + 2 * KV_WIDTH + 3 * HY_WIDTH + 2 * D_MODEL
COL_Q = 0
COL_K = ATT_WIDTH
COL_V = ATT_WIDTH + KV_WIDTH
COL_U = ATT_WIDTH + 2 * KV_WIDTH
COL_G = COL_U + 3 * HY_WIDTH

MXU_COLS = 256
SUBLANES = 8
LANES = 128
ROW_TILE = 512
ATT_Q_TILE = 512
ATT_KEY_CHUNK = 512
DFT_ROWS = 64
MOE_ROWS = 256
VMEM_LIMIT = 56 * 1024 * 1024


def _dot(a, b):
    return jnp.dot(a, b, preferred_element_type=F32)


def _split(a):
    hi = a.astype(BF16)
    lo = (a - hi.astype(F32)).astype(BF16)
    return hi, lo


def _dot3(a, b):
    ah, al = _split(a)
    bh, bl = _split(b)
    return _dot(ah, bh) + (_dot(al, bh) + _dot(ah, bl))


def _tiles_to_rows(t):
    c = pltpu.einshape("tjl->jtl", t)
    return jnp.concatenate([c[j] for j in range(c.shape[0])], axis=1)


def _rows_to_tiles(x):
    c = jnp.stack([x[:, j * LANES:(j + 1) * LANES] for j in range(x.shape[1] // LANES)], axis=0)
    return pltpu.einshape("jtl->tjl", c)


def _rms(t):
    return t * lax.rsqrt(jnp.mean(t * t, axis=-1, keepdims=True) + NORM_EPS)


def _params(*sem):
    return pltpu.CompilerParams(dimension_semantics=sem, vmem_limit_bytes=VMEM_LIMIT)


def _adaln_kernel(c_ref, w_ref, b_ref, o_ref):
    c = c_ref[...]
    o_ref[...] = _dot3(c * jax.nn.sigmoid(c), w_ref[...]) + b_ref[...]


def _adaln(cs, w_ada, b_ada):
    depth, d, n = w_ada.shape
    rows = cs.shape[0]
    tn = 1536
    return pl.pallas_call(
        _adaln_kernel,
        grid=(depth, n // tn),
        in_specs=[pl.BlockSpec((rows, d), lambda l, j: (0, 0)),
                  pl.BlockSpec((None, d, tn), lambda l, j: (l, 0, j)),
                  pl.BlockSpec((None, 1, tn), lambda l, j: (l, 0, j))],
        out_specs=pl.BlockSpec((None, rows, tn), lambda l, j: (l, 0, j)),
        out_shape=jax.ShapeDtypeStruct((depth, rows, n), F32),
        compiler_params=_params("parallel", "parallel"),
        name="adaln",
    )(cs, w_ada, b_ada.reshape(depth, 1, n))


def _inproj_kernel(x_ref, sh_ref, sc_ref, nw_ref, w_ref, qn_ref, kn_ref, cos_ref, sin_ref,
                   *outs, sections):
    h = _rms(x_ref[...]) * nw_ref[...]
    h = h * (1.0 + sc_ref[...]) + sh_ref[...]
    hb = h.astype(BF16)
    cos = cos_ref[...]
    sin = sin_ref[...]
    o = dict(zip(sections, outs))

    def head_cols(col0, n_cols, norm_w, scale, out_ref):
        for c in range(n_cols // MXU_COLS):
            acc = _dot(hb, w_ref[:, col0 + c * MXU_COLS:col0 + (c + 1) * MXU_COLS])
            for j in range(MXU_COLS // HEAD_DIM):
                t = _rms(acc[:, j * HEAD_DIM:(j + 1) * HEAD_DIM]) * norm_w
                t = t * cos + pltpu.roll(t, HEAD_DIM // 2, 1) * sin
                lo = c * MXU_COLS + j * HEAD_DIM
                out_ref[:, lo:lo + HEAD_DIM] = (t * scale).astype(out_ref.dtype)

    if "q" in o:
        head_cols(COL_Q, ATT_WIDTH, qn_ref[...], ATT_SCALE * LOG2_E, o["q"])
    if "k" in o:
        head_cols(COL_K, KV_WIDTH, kn_ref[...], 1.0, o["k"])
    if "v" in o:
        acc = _dot(hb, w_ref[:, COL_V:COL_V + KV_WIDTH]).astype(BF16)
        ones = jnp.ones((acc.shape[0], HEAD_DIM), BF16)
        for j in range(N_KV_HEADS):
            o["v"][:, 2 * j * HEAD_DIM:(2 * j + 1) * HEAD_DIM] = acc[:, j * HEAD_DIM:(j + 1) * HEAD_DIM]
            o["v"][:, (2 * j + 1) * HEAD_DIM:(2 * j + 2) * HEAD_DIM] = ones
    if "u" in o:
        for c in range(3):
            o["u"][:, c * HY_WIDTH:(c + 1) * HY_WIDTH] = _dot(
                hb, w_ref[:, COL_U + c * HY_WIDTH:COL_U + (c + 1) * HY_WIDTH])
    if "g" in o:
        for c in range(4):
            acc = _dot(hb, w_ref[:, COL_G + c * 512:COL_G + (c + 1) * 512])
            o["g"][:, c * 512:(c + 1) * 512] = jax.nn.sigmoid(acc).astype(BF16)


_SECTION_SHAPES = {"q": (ATT_WIDTH, BF16), "k": (KV_WIDTH, BF16), "v": (2 * KV_WIDTH, BF16),
                   "u": (3 * HY_WIDTH, F32), "g": (2 * D_MODEL, BF16)}


def _inproj(x2d, seq_len, sh, sc, nw, w_bf16, qn, kn, cos2, sin2, sections):
    rows, d = x2d.shape
    tm = min(ROW_TILE, seq_len)
    tiles_per_seq = seq_len // tm
    nb = sh.shape[0]
    mod_idx = (lambda i: (i // tiles_per_seq, 0, 0)) if nb > 1 else (lambda i: (0, 0, 0))
    const2 = lambda i: (0, 0)
    out_shape = [jax.ShapeDtypeStruct((rows, _SECTION_SHAPES[s][0]), _SECTION_SHAPES[s][1])
                 for s in sections]
    out_specs = [pl.BlockSpec((tm, _SECTION_SHAPES[s][0]), lambda i: (i, 0)) for s in sections]
    return pl.pallas_call(
        functools.partial(_inproj_kernel, sections=tuple(sections)),
        grid=(rows // tm,),
        in_specs=[pl.BlockSpec((tm, d), lambda i: (i, 0)),
                  pl.BlockSpec((None, 1, d), mod_idx),
                  pl.BlockSpec((None, 1, d), mod_idx),
                  pl.BlockSpec((1, d), const2),
                  pl.BlockSpec((d, IN_WIDTH), const2),
                  pl.BlockSpec((1, HEAD_DIM), const2),
                  pl.BlockSpec((1, HEAD_DIM), const2),
                  pl.BlockSpec((tm, HEAD_DIM), lambda i: (i % tiles_per_seq, 0)),
                  pl.BlockSpec((tm, HEAD_DIM), lambda i: (i % tiles_per_seq, 0))],
        out_specs=out_specs,
        out_shape=out_shape,
        compiler_params=_params("parallel"),
        name="inproj",
    )(x2d, sh, sc, nw, w_bf16, qn, kn, cos2, sin2)


def _attn_kernel(q_ref, *refs, n_kv_sets):
    kv = refs[:2 * n_kv_sets]
    o_ref = refs[2 * n_kv_sets]
    nt = (((1,), (1,)), ((), ()))
    for g in range(N_HEADS // N_KV_HEADS):
        q = q_ref[:, g * HEAD_DIM:(g + 1) * HEAD_DIM]
        m = None
        acc = None
        for s_idx in range(n_kv_sets):
            k_ref, v_ref = kv[2 * s_idx], kv[2 * s_idx + 1]
            lk = k_ref.shape[0]
            for c0 in range(0, lk, ATT_KEY_CHUNK):
                c1 = min(c0 + ATT_KEY_CHUNK, lk)
                s = lax.dot_general(q, k_ref[c0:c1, :], nt, preferred_element_type=F32)
                mc = jnp.max(s, axis=1, keepdims=True)
                if m is None:
                    m = mc
                    acc = _dot(jnp.exp2(s - m).astype(BF16), v_ref[c0:c1, :])
                else:
                    m_new = jnp.maximum(m, mc)
                    acc = jnp.exp2(m - m_new) * acc + _dot(jnp.exp2(s - m_new).astype(BF16),
                                                           v_ref[c0:c1, :])
                    m = m_new
        o_ref[:, g * HEAD_DIM:(g + 1) * HEAD_DIM] = (
            acc[:, :HEAD_DIM] / acc[:, HEAD_DIM:]).astype(o_ref.dtype)


def _attention(q, kv_sets, batch, seq_len):
    tq = min(ATT_Q_TILE, seq_len)
    nq = seq_len // tq
    grp = (N_HEADS // N_KV_HEADS) * HEAD_DIM
    in_specs = [pl.BlockSpec((tq, grp), lambda b, j, i: (b * nq + i, j))]
    args = [q]
    for k, v, lk in kv_sets:
        in_specs += [pl.BlockSpec((lk, HEAD_DIM), lambda b, j, i: (b, j)),
                     pl.BlockSpec((lk, 2 * HEAD_DIM), lambda b, j, i: (b, j))]
        args += [k, v]
    return pl.pallas_call(
        functools.partial(_attn_kernel, n_kv_sets=len(kv_sets)),
        grid=(batch, N_KV_HEADS, nq),
        in_specs=in_specs,
        out_specs=pl.BlockSpec((tq, grp), lambda b, j, i: (b * nq + i, j)),
        out_shape=jax.ShapeDtypeStruct(q.shape, BF16),
        compiler_params=_params("parallel", "parallel", "arbitrary"),
        name="attention",
    )(*args)


def _hy_filter_kernel(z_ref, t_ref, w1_ref, b1_ref, f1_ref, w2_ref, b2_ref, f2_ref, w3_ref,
                      dl_ref, sp_ref, sm_ref, spa_ref, sma_ref, mid_ref):
    i = pl.program_id(0)
    tl = z_ref.shape[0]
    h = jnp.sin(f1_ref[...] * (_dot3(z_ref[...], w1_ref[...]) + b1_ref[...]))
    h = jnp.sin(f2_ref[...] * (_dot3(h, w2_ref[...]) + b2_ref[...]))
    h = _dot3(h, w3_ref[...])
    window = jnp.exp(-t_ref[...] * dl_ref[...])
    row = i * tl + lax.broadcasted_iota(jnp.int32, (tl, HY_WIDTH), 0)
    alt = (1 - 2 * (row & 1)).astype(F32)
    quarter = row & 3
    cos4 = jnp.where(quarter == 0, 1.0, jnp.where(quarter == 2, -1.0, 0.0))
    sin4 = jnp.where(quarter == 1, 1.0, jnp.where(quarter == 3, -1.0, 0.0))

    @pl.when(i == 0)
    def _():
        mid_ref[...] = jnp.zeros_like(mid_ref)

    for o in range(HY_ORDER):
        cols = slice(o * HY_WIDTH, (o + 1) * HY_WIDTH)
        hf = h[:, (2 * o) * HY_WIDTH:(2 * o + 1) * HY_WIDTH] * window
        hb = h[:, (2 * o + 1) * HY_WIDTH:(2 * o + 2) * HY_WIDTH] * window
        hb = jnp.where(row == 0, 0.0, hb)
        plus = hf + hb
        minus = hf - hb
        sp_ref[:, cols] = plus.astype(BF16)
        sm_ref[:, cols] = minus.astype(BF16)
        spa_ref[:, cols] = (alt * plus).astype(BF16)
        sma_ref[:, cols] = (alt * minus).astype(BF16)
        mid_ref[0:1, cols] += jnp.sum(cos4 * plus, axis=0, keepdims=True)
        mid_ref[1:2, cols] += jnp.sum(sin4 * minus, axis=0, keepdims=True)


def _hy_filter(seq_len, pe_w1, pe_b1, freq1, pe_w2, pe_b2, freq2, pe_w3):
    t01 = np.linspace(0.0, 1.0, seq_len)[:, None]
    pos = np.arange(seq_len, dtype=np.float64)[:, None]
    bands = np.linspace(1e-4, HY_BANDS - 1, HY_BANDS)[None, :]
    f = 2.0 * math.pi * pos * bands / seq_len
    z = np.zeros((seq_len, HY_PE_PAD), np.float32)
    z[:, :HY_PE_DIM] = np.concatenate([t01, np.cos(f), -np.sin(f)], axis=-1)
    max_decay = math.log(HY_DECAY_TARGET) / HY_FAST_DECAY
    min_decay = math.log(HY_DECAY_TARGET) / HY_SLOW_DECAY
    deltas = np.abs(np.linspace(min_decay, max_decay, HY_WIDTH))[None, :].astype(np.float32)
    w1p = jnp.zeros((HY_PE_PAD, HY_FILTER_HIDDEN), F32).at[:HY_PE_DIM].set(pe_w1)
    tl = min(512, seq_len)
    hid = HY_FILTER_HIDDEN
    n_out = HY_ORDER * HY_WIDTH
    c2 = lambda i: (0, 0)
    return pl.pallas_call(
        _hy_filter_kernel,
        grid=(seq_len // tl,),
        in_specs=[pl.BlockSpec((tl, HY_PE_PAD), lambda i: (i, 0)),
                  pl.BlockSpec((tl, 1), lambda i: (i, 0)),
                  pl.BlockSpec((HY_PE_PAD, hid), c2), pl.BlockSpec((1, hid), c2),
                  pl.BlockSpec((1, hid), c2), pl.BlockSpec((hid, hid), c2),
                  pl.BlockSpec((1, hid), c2), pl.BlockSpec((1, hid), c2),
                  pl.BlockSpec((hid, 2 * n_out), c2), pl.BlockSpec((1, HY_WIDTH), c2)],
        out_specs=[pl.BlockSpec((tl, n_out), lambda i: (i, 0))] * 4
                  + [pl.BlockSpec((SUBLANES, n_out), c2)],
        out_shape=[jax.ShapeDtypeStruct((seq_len, n_out), BF16)] * 4
                  + [jax.ShapeDtypeStruct((SUBLANES, n_out), F32)],
        compiler_params=_params("arbitrary"),
        name="hy_filter",
    )(jnp.asarray(z), jnp.asarray(t01.astype(np.float32)), w1p, pe_b1.reshape(1, hid),
      freq1.reshape(1, hid), pe_w2, pe_b2.reshape(1, hid), freq2.reshape(1, hid), pe_w3,
      jnp.asarray(deltas))


def _dft_kernel(ca_ref, sa_ref, cb_ref, sb_ref, alt_ref, wc_ref, cm_ref, sm_ref, *inverse_refs):
    i = pl.program_id(0)
    ca = ca_ref[...]
    sa = sa_ref[...]
    cb = cb_ref[...]
    sb = sb_ref[...]
    c = ca * cb - sa * sb
    s = sa * cb + ca * sb
    rows = lax.broadcasted_iota(jnp.int32, c.shape, 0)
    cols = lax.broadcasted_iota(jnp.int32, c.shape, 1)
    wc = wc_ref[...]
    cm_ref[...] = c.astype(BF16)
    sm_ref[...] = jnp.where((rows == 0) & (i == 0), alt_ref[...], s).astype(BF16)
    if inverse_refs:
        ci_ref, si_ref = inverse_refs
        alt_t = (1 - 2 * (rows & 1)).astype(F32)
        ci_ref[...] = (c * wc).astype(BF16)
        si_ref[...] = (jnp.where(cols == 0, alt_t, s) * wc).astype(BF16)


def _dft_matrices(seq_len, inverse, n_rows=None):
    n = 2 * seq_len
    n_rows = seq_len if n_rows is None else n_rows
    idx = np.arange(seq_len, dtype=np.int64)[None, :]
    r1 = np.arange(n_rows // DFT_ROWS, dtype=np.int64)[:, None] * DFT_ROWS
    r0 = np.arange(DFT_ROWS, dtype=np.int64)[:, None]
    ang_a = ((r1 * idx) % n).astype(np.float64) * (2.0 * math.pi / n)
    ang_b = ((r0 * idx) % n).astype(np.float64) * (2.0 * math.pi / n)
    tab = lambda a: jnp.asarray(a.astype(np.float32))
    ca = tab(np.cos(ang_a)).reshape(-1, 1, seq_len)
    sa = tab(np.sin(ang_a)).reshape(-1, 1, seq_len)
    alt = tab(1.0 - 2.0 * (idx % 2))
    wc = tab(np.where(idx == 0, 1.0, 2.0) / n)
    row_blk = pl.BlockSpec((None, 1, seq_len), lambda i: (i, 0, 0))
    full = lambda r: pl.BlockSpec((r, seq_len), lambda i: (0, 0))
    out_blk = pl.BlockSpec((DFT_ROWS, seq_len), lambda i: (i, 0))
    n_out = 4 if inverse else 2
    mats = pl.pallas_call(
        _dft_kernel,
        grid=(n_rows // DFT_ROWS,),
        in_specs=[row_blk, row_blk, full(DFT_ROWS), full(DFT_ROWS), full(1), full(1)],
        out_specs=[out_blk] * n_out,
        out_shape=[jax.ShapeDtypeStruct((n_rows, seq_len), BF16)] * n_out,
        compiler_params=_params("parallel"),
        name="dft_matrices",
    )(ca, sa, tab(np.cos(ang_b)), tab(np.sin(ang_b)), alt, wc)
    if not inverse:
        return tuple(mats)
    tw = np.arange(seq_len, dtype=np.float64)[:, None] * (2.0 * math.pi / (2 * n))
    rep = lambda a: tab(np.broadcast_to(a, (seq_len, LANES)))
    return tuple(mats) + (rep(np.cos(tw)), rep(np.sin(tw)))


def _hy_spec_kernel(cm_ref, sm_ref, sp_ref, smn_ref, spa_ref, sma_ref, mid_ref,
                    klr_ref, kli_ref, khr_ref, khi_ref):
    i = pl.program_id(1)
    cm = cm_ref[...]
    sm = sm_ref[...]
    klr_ref[...] = _dot(cm, sp_ref[...])
    khr_ref[...] = _dot(cm, spa_ref[...])
    lo_q = _dot(sm, smn_ref[...])
    hi_q = _dot(sm, sma_ref[...])
    rows = lax.broadcasted_iota(jnp.int32, lo_q.shape, 0)
    edge = (rows == 0) & (i == 0)
    kli_ref[...] = jnp.where(edge, mid_ref[0:1, :], -lo_q)
    khi_ref[...] = jnp.where(edge, -mid_ref[1:2, :], hi_q)


def _hy_spectrum(cm, sm, splus, sminus, splus_alt, sminus_alt, mid):
    half, seq_len = cm.shape
    tf = min(256, half)
    mat = pl.BlockSpec((tf, seq_len), lambda o, i: (i, 0))
    sig = pl.BlockSpec((seq_len, HY_WIDTH), lambda o, i: (0, o))
    out = pl.BlockSpec((tf, HY_WIDTH), lambda o, i: (i, o))
    return pl.pallas_call(
        _hy_spec_kernel,
        grid=(HY_ORDER, half // tf),
        in_specs=[mat, mat, sig, sig, sig, sig,
                  pl.BlockSpec((SUBLANES, HY_WIDTH), lambda o, i: (0, o))],
        out_specs=[out] * 4,
        out_shape=[jax.ShapeDtypeStruct((half, HY_ORDER * HY_WIDTH), F32)] * 4,
        compiler_params=_params("parallel", "arbitrary"),
        name="hy_spectrum",
    )(cm, sm, splus, sminus, splus_alt, sminus_alt, mid)


def _hy_sconv_kernel(u_ref, w_ref, b_ref, ue_ref, uo_ref, ve_ref, vo_ref, *, v_first_block):
    j = pl.program_id(1)
    half = u_ref.shape[0] // 2
    tc = u_ref.shape[1]
    x2 = pltpu.einshape("(tp)c->t(pc)", u_ref[...], p=2)
    xe = x2[:, :tc]
    xo = x2[:, tc:]
    rows = lax.broadcasted_iota(jnp.int32, xe.shape, 0)
    xo_prev = jnp.where(rows == 0, 0.0, pltpu.roll(xo, 1, 0))
    xe_next = jnp.where(rows == half - 1, 0.0, pltpu.roll(xe, half - 1, 0))
    w0, w1, w2 = w_ref[0:1, :], w_ref[1:2, :], w_ref[2:3, :]
    ye = xo_prev * w0 + xe * w1 + xo * w2 + b_ref[...]
    yo = xe * w0 + xo * w1 + xe_next * w2 + b_ref[...]
    ue_ref[...] = ye
    uo_ref[...] = yo

    @pl.when(j >= v_first_block)
    def _():
        ve_ref[...] = ye.astype(BF16)
        vo_ref[...] = yo.astype(BF16)


def _hy_sconv(u, batch, seq_len, conv_w, conv_b):
    ch = u.shape[1]
    tc = 256
    v0 = (2 * HY_WIDTH) // tc
    half = seq_len // 2
    u3 = u.reshape(batch, seq_len, ch)
    plane = pl.BlockSpec((None, half, tc), lambda b, j: (b, 0, j))
    vplane = pl.BlockSpec((None, half, tc), lambda b, j: (b, 0, jnp.maximum(j - v0, 0)))
    ue, uo, ve, vo = pl.pallas_call(
        functools.partial(_hy_sconv_kernel, v_first_block=v0),
        grid=(batch, ch // tc),
        in_specs=[pl.BlockSpec((None, seq_len, tc), lambda b, j: (b, 0, j)),
                  pl.BlockSpec((3, tc), lambda b, j: (0, j)),
                  pl.BlockSpec((1, tc), lambda b, j: (0, j))],
        out_specs=[plane, plane, vplane, vplane],
        out_shape=[jax.ShapeDtypeStruct((batch, half, ch), F32)] * 2
                  + [jax.ShapeDtypeStruct((batch, half, HY_WIDTH), BF16)] * 2,
        compiler_params=_params("parallel", "arbitrary"),
        name="hy_sconv",
    )(u3, conv_w, conv_b.reshape(1, ch))
    return (ue, uo), (ve, vo)


def _hy_fwd_kernel(cm_ref, sm_ref, tc_ref, ts_ref, ve_ref, vo_ref, klr_ref, kli_ref, khr_ref,
                   khi_ref, zp_ref, zq_ref):
    i = pl.program_id(0)
    w = ve_ref.shape[1]
    v = jnp.concatenate([ve_ref[...], vo_ref[...]], axis=1)
    p = _dot(cm_ref[...], v)
    q = _dot(sm_ref[...], v)
    pe, po, qe, qo = p[:, :w], p[:, w:], q[:, :w], q[:, w:]
    reps = w // LANES
    tc = jnp.concatenate([tc_ref[...]] * reps, axis=1)
    ts = jnp.concatenate([ts_ref[...]] * reps, axis=1)
    rows = lax.broadcasted_iota(jnp.int32, pe.shape, 0)
    edge = (rows == 0) & (i == 0)

    a = tc * po - ts * qo
    b = tc * qo + ts * po
    xlr = pe + a
    xhr = pe - a
    xli = jnp.where(edge, qe, -(qe + b))
    xhi = jnp.where(edge, -qo, qe - b)

    klr, kli, khr, khi = klr_ref[...], kli_ref[...], khr_ref[...], khi_ref[...]
    ylr = xlr * klr - jnp.where(edge, 0.0, xli * kli)
    yhr = xhr * khr - jnp.where(edge, 0.0, xhi * khi)
    yli = jnp.where(edge, xli * kli - xhi * khi, xlr * kli + xli * klr)
    yhi = jnp.where(edge, xli * khi + xhi * kli, xhr * khi + xhi * khr)

    dr = ylr - yhr
    di = yli + yhi
    zp_ref[:, :w] = (0.5 * (ylr + yhr)).astype(BF16)
    zp_ref[:, w:] = (0.5 * (tc * dr - ts * di)).astype(BF16)
    zq_ref[:, :w] = jnp.where(edge, yli, -0.5 * (yli - yhi)).astype(BF16)
    zq_ref[:, w:] = jnp.where(edge, -yhi, -0.5 * (tc * di + ts * dr)).astype(BF16)


def _hy_forward(dfth, v_planes, kpack, order):
    cm, sm, _, _, tc, ts = dfth
    batch, half, w = v_planes[0].shape
    tf = min(512, half)
    mat = pl.BlockSpec((tf, half), lambda i, b: (i, 0))
    tw = pl.BlockSpec((tf, LANES), lambda i, b: (i, 0))
    spec = pl.BlockSpec((tf, w), lambda i, b: (i, order))
    out = pl.BlockSpec((None, tf, 2 * w), lambda i, b: (b, i, 0))
    return pl.pallas_call(
        _hy_fwd_kernel,
        grid=(half // tf, batch),
        in_specs=[mat, mat, tw, tw,
                  pl.BlockSpec((None, half, w), lambda i, b: (b, 0, 0)),
                  pl.BlockSpec((None, half, w), lambda i, b: (b, 0, 0)),
                  spec, spec, spec, spec],
        out_specs=[out, out],
        out_shape=[jax.ShapeDtypeStruct((batch, half, 2 * w), BF16)] * 2,
        compiler_params=_params("parallel", "arbitrary"),
        name="hy_forward",
    )(cm, sm, tc, ts, v_planes[0], v_planes[1], *kpack)


def _hy_inv_kernel(ci_ref, si_ref, zp_ref, zq_ref, ae_ref, ao_ref, ce_ref, co_ref, skip_ref, *outs,
                   interleave):
    w = ae_ref.shape[1]
    tt = ae_ref.shape[0]
    y = _dot(ci_ref[...], zp_ref[...]) + _dot(si_ref[...], zq_ref[...])
    skip = skip_ref[...]
    ze = ae_ref[...] * (y[:, :w] + skip * ce_ref[...])
    zo = ao_ref[...] * (y[:, w:] + skip * co_ref[...])
    if interleave:
        (o_ref,) = outs
        z = jnp.concatenate([ze, zo], axis=1).astype(o_ref.dtype)
        o_ref[...] = pltpu.einshape("t(pw)->(tp)w", z, p=2)
    else:
        for k in range(len(outs) // 2):
            outs[2 * k][...] = ze.astype(outs[2 * k].dtype)
            outs[2 * k + 1][...] = zo.astype(outs[2 * k + 1].dtype)


def _hy_inverse(dfth, zp, zq, a_planes, a_blk, c_planes, c_blk, skip, out_dtypes, interleave):
    _, _, ci, si, _, _ = dfth
    batch, half, w2 = zp.shape
    w = w2 // 2
    tt = min(512, half)
    mat = pl.BlockSpec((tt, half), lambda i, b: (i, 0))
    sig = pl.BlockSpec((None, half, w2), lambda i, b: (b, 0, 0))
    a_spec = pl.BlockSpec((None, tt, w), lambda i, b: (b, i, a_blk))
    c_spec = pl.BlockSpec((None, tt, w), lambda i, b: (b, i, c_blk))
    plane = pl.BlockSpec((None, tt, w), lambda i, b: (b, i, 0))
    if interleave:
        (dt,) = out_dtypes
        out_specs = [pl.BlockSpec((None, 2 * tt, w), lambda i, b: (b, i, 0))]
        out_shape = [jax.ShapeDtypeStruct((batch, 2 * half, w), dt)]
    else:
        out_specs = [plane] * (2 * len(out_dtypes))
        out_shape = [jax.ShapeDtypeStruct((batch, half, w), dt) for dt in out_dtypes for _ in (0, 1)]
    outs = pl.pallas_call(
        functools.partial(_hy_inv_kernel, interleave=interleave),
        grid=(half // tt, batch),
        in_specs=[mat, mat, sig, sig, a_spec, a_spec, c_spec, c_spec,
                  pl.BlockSpec((1, w), lambda i, b: (0, 0))],
        out_specs=out_specs,
        out_shape=out_shape,
        compiler_params=_params("parallel", "arbitrary"),
        name="hy_inverse",
    )(ci, si, zp, zq, a_planes[0], a_planes[1], c_planes[0], c_planes[1], skip)
    if interleave:
        return outs[0]
    return [(outs[2 * k], outs[2 * k + 1]) for k in range(len(out_dtypes))]


def _hyena(u, batch, seq_len, dfth, kpack, conv_w, conv_b, skip):
    uc, vb = _hy_sconv(u, batch, seq_len, conv_w, conv_b)
    zp, zq = _hy_forward(dfth, vb, kpack, 0)
    z, zb = _hy_inverse(dfth, zp, zq, uc, 0, uc, 2, skip[0:1], (F32, BF16), False)
    zp, zq = _hy_forward(dfth, zb, kpack, 1)
    y = _hy_inverse(dfth, zp, zq, uc, 1, z, 0, skip[1:2], (F32,), True)
    return y.reshape(batch * seq_len, HY_WIDTH)


def _merge_kernel(ya_ref, yh_ref, g_ref, x_ref, g1_ref, sh_ref, sc_ref, nw_ref,
                  wa_ref, wh_ref, wo_ref, rh_ref, rl_ref, *rest):
    xo_ref, hx_ref, lg_ref = rest[-3:]
    a = _dot(ya_ref[...], wa_ref[...])
    h = _dot(yh_ref[...].astype(BF16), wh_ref[...])
    m = g_ref[:, :D_MODEL].astype(F32) * a + g_ref[:, D_MODEL:].astype(F32) * h
    xn = x_ref[...] + g1_ref[...] * _dot(m.astype(BF16), wo_ref[...])
    xo_ref[...] = xn
    hx = _rms(xn) * nw_ref[...]
    hx = hx * (1.0 + sc_ref[...]) + sh_ref[...]
    hx_ref[...] = _rows_to_tiles(hx)
    hh, hl = _split(hx)
    nt = (((1,), (1,)), ((), ()))
    dn = lambda a, b: lax.dot_general(a, b, nt, preferred_element_type=F32)
    lg_ref[...] = dn(rh_ref[...], hh) + (dn(rh_ref[...], hl) + dn(rl_ref[...], hh))


def _merge(ya, yh, g, x2d, seq_len, g1, sh2, sc2, nw, wa, wh, wo, r_hi, r_lo,
           total_rows, row_offset, prev=None):
    rows, d = x2d.shape
    tm = min(ROW_TILE, seq_len)
    tiles_per_seq = seq_len // tm
    off = row_offset // tm
    nb = g1.shape[0]
    mod_idx = (lambda i: (i // tiles_per_seq, 0, 0)) if nb > 1 else (lambda i: (0, 0, 0))
    c2 = lambda i: (0, 0)
    row = lambda w: pl.BlockSpec((tm, w), lambda i: (i, 0))
    in_specs = [row(ATT_WIDTH), row(HY_WIDTH), row(2 * d), row(d),
                pl.BlockSpec((None, 1, d), mod_idx), pl.BlockSpec((None, 1, d), mod_idx),
                pl.BlockSpec((None, 1, d), mod_idx), pl.BlockSpec((1, d), c2),
                pl.BlockSpec((ATT_WIDTH, d), c2), pl.BlockSpec((HY_WIDTH, d), c2),
                pl.BlockSpec((d, d), c2), pl.BlockSpec((N_EXPERTS, d), c2),
                pl.BlockSpec((N_EXPERTS, d), c2)]
    args = [ya, yh, g, x2d, g1, sh2, sc2, nw, wa, wh, wo, r_hi, r_lo]
    aliases = {}
    if prev is not None:
        in_specs += [pl.BlockSpec(memory_space=pl.ANY)] * 2
        aliases = {len(args): 1, len(args) + 1: 2}
        args += list(prev)
    return pl.pallas_call(
        _merge_kernel,
        grid=(rows // tm,),
        in_specs=in_specs,
        out_specs=[row(d), pl.BlockSpec((tm, d // LANES, LANES), lambda i: (i + off, 0, 0)),
                   pl.BlockSpec((N_EXPERTS, tm), lambda i: (0, i + off))],
        out_shape=[jax.ShapeDtypeStruct((rows, d), F32),
                   jax.ShapeDtypeStruct((total_rows, d // LANES, LANES), F32),
                   jax.ShapeDtypeStruct((N_EXPERTS, total_rows), F32)],
        input_output_aliases=aliases,
        compiler_params=_params("arbitrary"),
        name="merge",
    )(*args)


def _route_kernel(lg_ref, bias_ref, idx_ref, gate_ref, rank_ref, cnt_ref, run_ref):
    step = pl.program_id(0)
    tn = lg_ref.shape[2]
    shape = (N_GROUPS, EXPERTS_PER_GROUP, tn)
    s = jax.nn.sigmoid(lg_ref[...])
    b = s + bias_ref[...]
    mem = lax.broadcasted_iota(jnp.int32, shape, 1).astype(F32)
    grp = lax.broadcasted_iota(jnp.int32, (N_GROUPS, 1, tn), 0).astype(F32)
    big = float(N_EXPERTS)

    m1 = jnp.max(b, axis=1, keepdims=True)
    i1 = jnp.min(jnp.where(b == m1, mem, big), axis=1, keepdims=True)
    b2 = jnp.where(mem == i1, -jnp.inf, b)
    m2 = jnp.max(b2, axis=1, keepdims=True)
    i2 = jnp.min(jnp.where(b2 == m2, mem, big), axis=1, keepdims=True)
    gs = m1 + m2
    gmax = jnp.max(gs, axis=0, keepdims=True)
    gsel = jnp.min(jnp.where(gs == gmax, grp, big), axis=0, keepdims=True)
    selg = grp == gsel

    @pl.when(step == 0)
    def _():
        run_ref[...] = jnp.zeros_like(run_ref)

    tri = (lax.broadcasted_iota(jnp.int32, (tn, tn), 0)
           < lax.broadcasted_iota(jnp.int32, (tn, tn), 1)).astype(BF16)
    run = run_ref[...]
    ws = []
    for k, ik in enumerate((i1, i2)):
        hit = selg & (mem == ik)
        ws.append(jnp.sum(jnp.where(hit, s, 0.0), axis=(0, 1), keepdims=True))
        e_loc = jnp.sum(jnp.where(selg, ik, 0.0), axis=0, keepdims=True)
        idx_ref[k:k + 1, :] = (gsel * EXPERTS_PER_GROUP + e_loc).reshape(1, tn).astype(jnp.int32)
        oh = jnp.where(hit, 1.0, 0.0).reshape(N_EXPERTS, tn)
        before = run + _dot(oh.astype(BF16), tri)
        rank_ref[k:k + 1, :] = jnp.sum(oh * before, axis=0, keepdims=True).astype(jnp.int32)
        run = run + jnp.sum(oh, axis=1, keepdims=True)
    run_ref[...] = run
    cnt_ref[...] = run
    tot = ws[0] + ws[1]
    for k in range(TOP_K):
        gate_ref[k:k + 1, :] = (ws[k] / tot).reshape(1, tn)


def _route(logits_t, router_b):
    n_tok = logits_t.shape[1]
    tn = 512
    lg3 = logits_t.reshape(N_GROUPS, EXPERTS_PER_GROUP, n_tok)
    row2 = pl.BlockSpec((TOP_K, tn), lambda i: (0, i))
    return pl.pallas_call(
        _route_kernel,
        grid=(n_tok // tn,),
        in_specs=[pl.BlockSpec((N_GROUPS, EXPERTS_PER_GROUP, tn), lambda i: (0, 0, i)),
                  pl.BlockSpec((N_GROUPS, EXPERTS_PER_GROUP, 1), lambda i: (0, 0, 0))],
        out_specs=[row2, row2, row2, pl.BlockSpec((N_EXPERTS, 1), lambda i: (0, 0))],
        out_shape=[jax.ShapeDtypeStruct((TOP_K, n_tok), jnp.int32),
                   jax.ShapeDtypeStruct((TOP_K, n_tok), F32),
                   jax.ShapeDtypeStruct((TOP_K, n_tok), jnp.int32),
                   jax.ShapeDtypeStruct((N_EXPERTS, 1), F32)],
        scratch_shapes=[pltpu.VMEM((N_EXPERTS, 1), F32)],
        compiler_params=_params("arbitrary"),
        name="moe_route",
    )(lg3, router_b.astype(F32).reshape(N_GROUPS, EXPERTS_PER_GROUP, 1))


def _slot_kernel(idx_ref, rank_ref, start_ref, dest_ref):
    tn = idx_ref.shape[1]
    e = lax.broadcasted_iota(jnp.int32, (N_EXPERTS, tn), 0)
    for k in range(TOP_K):
        base = jnp.sum(jnp.where(e == idx_ref[k:k + 1, :], start_ref[...], 0.0), axis=0,
                       keepdims=True)
        dest_ref[k:k + 1, :] = base.astype(jnp.int32) + rank_ref[k:k + 1, :]


def _slots(idx, rank, pad_start):
    n_tok = idx.shape[1]
    tn = 512
    row2 = pl.BlockSpec((TOP_K, tn), lambda i: (0, i))
    return pl.pallas_call(
        _slot_kernel,
        grid=(n_tok // tn,),
        in_specs=[row2, row2, pl.BlockSpec((N_EXPERTS, 1), lambda i: (0, 0))],
        out_specs=row2,
        out_shape=jax.ShapeDtypeStruct((TOP_K, n_tok), jnp.int32),
        compiler_params=_params("parallel"),
        name="moe_slots",
    )(idx, rank, pad_start.astype(F32).reshape(N_EXPERTS, 1))


def _layout(counts, n_blocks):
    counts = counts.reshape(N_EXPERTS).astype(jnp.int32)
    padded = (counts + MOE_ROWS - 1) // MOE_ROWS * MOE_ROWS
    pad_end = jnp.cumsum(padded)
    pad_start = pad_end - padded
    blk_row = jnp.arange(n_blocks, dtype=jnp.int32)[:, None] * MOE_ROWS
    block_e = jnp.minimum(jnp.sum((pad_end[None, :] <= blk_row).astype(jnp.int32), axis=1),
                          N_EXPERTS - 1)
    prev_e = jnp.concatenate([jnp.full((1,), -1, jnp.int32), block_e[:-1]])
    block_new = (block_e != prev_e).astype(jnp.int32)
    n_used = (pad_end[-1] // MOE_ROWS).astype(jnp.int32).reshape(1)
    return pad_start, pad_start + counts, pad_end, block_e, block_new, n_used


_PAD_CHUNKS = tuple(2 ** p for p in range(int(math.log2(MOE_ROWS)) - 1, -1, -1))


def _scatter_kernel(dest_ref, lo_ref, hi_ref, x_ref, o_hbm, zeros, sem, zsem, *, n_tok):
    i = pl.program_id(0)
    tm = x_ref.shape[0]

    def pad_copies(act):
        def per_expert(e, carry):
            off = lo_ref[e]
            n = hi_ref[e] - off
            for c in _PAD_CHUNKS:
                @pl.when((n & c) != 0)
                def _():
                    act(pltpu.make_async_copy(zeros.at[pl.ds(0, c)], o_hbm.at[pl.ds(off, c)], zsem))
                off = off + (n & c)
            return carry
        lax.fori_loop(0, N_EXPERTS, per_expert, 0)

    @pl.when(i == 0)
    def _():
        zeros[...] = jnp.zeros_like(zeros)
        pad_copies(lambda cp: cp.start())

    def issue(r, carry):
        for k in range(TOP_K):
            pltpu.make_async_copy(x_ref.at[r], o_hbm.at[dest_ref[k * n_tok + i * tm + r]], sem).start()
        return carry
    lax.fori_loop(0, tm, issue, 0, unroll=8)
    for k in range(TOP_K):
        pltpu.make_async_copy(x_ref, o_hbm.at[pl.ds(0, tm)], sem).wait()

    @pl.when(i == 0)
    def _():
        pad_copies(lambda cp: cp.wait())


def _scatter_rows(dest_flat, pad_lo, pad_hi, h, n_blocks):
    n_tok = h.shape[0]
    tile = h.shape[1:]
    tm = 256
    return pl.pallas_call(
        functools.partial(_scatter_kernel, n_tok=n_tok),
        grid_spec=pltpu.PrefetchScalarGridSpec(
            num_scalar_prefetch=3,
            grid=(n_tok // tm,),
            in_specs=[pl.BlockSpec((tm,) + tile, lambda i, *_: (i, 0, 0))],
            out_specs=pl.BlockSpec(memory_space=pl.ANY),
            scratch_shapes=[pltpu.VMEM((MOE_ROWS // 2,) + tile, h.dtype),
                            pltpu.SemaphoreType.DMA(()), pltpu.SemaphoreType.DMA(())]),
        out_shape=jax.ShapeDtypeStruct((n_blocks * MOE_ROWS,) + tile, h.dtype),
        compiler_params=pltpu.CompilerParams(dimension_semantics=("arbitrary",),
                                             vmem_limit_bytes=VMEM_LIMIT, has_side_effects=True),
        name="moe_scatter",
    )(dest_flat, pad_lo, pad_hi, h)


def _expert_kernel(be_ref, new_ref, nused_ref, x_ref, w1_ref, w3_ref, w2_ref, y_ref,
                   w1b, w3b, w2b):
    i = pl.program_id(0)

    @pl.when(i < nused_ref[0])
    def _():
        @pl.when(new_ref[i] == 1)
        def _():
            w1b[...] = w1_ref[...].astype(BF16)
            w3b[...] = w3_ref[...].astype(BF16)
            w2b[...] = w2_ref[...].astype(BF16)

        x = _tiles_to_rows(x_ref[...]).astype(BF16)
        a = _dot(x, w1b[...])
        b = _dot(x, w3b[...])
        h = (a * jax.nn.sigmoid(a)) * b
        y_ref[...] = _rows_to_tiles(_dot(h.astype(BF16), w2b[...]))


def _experts(block_e, block_new, n_used, xs, w1, w3, w2, layer):
    n_slots = xs.shape[0]
    tile = xs.shape[1:]
    d = tile[0] * tile[1]
    n_blocks = n_slots // MOE_ROWS
    e_dim = w1.shape[3]
    blk = lambda i, be, bn, nu: (jnp.minimum(i, nu[0] - 1), 0, 0)
    wsel = lambda i, be, bn, nu: (layer, be[jnp.minimum(i, nu[0] - 1)], 0, 0)
    return pl.pallas_call(
        _expert_kernel,
        grid_spec=pltpu.PrefetchScalarGridSpec(
            num_scalar_prefetch=3,
            grid=(n_blocks,),
            in_specs=[pl.BlockSpec((MOE_ROWS,) + tile, blk),
                      pl.BlockSpec((None, None, d, e_dim), wsel),
                      pl.BlockSpec((None, None, d, e_dim), wsel),
                      pl.BlockSpec((None, None, e_dim, d), wsel)],
            out_specs=pl.BlockSpec((MOE_ROWS,) + tile, blk),
            scratch_shapes=[pltpu.VMEM((d, e_dim), BF16), pltpu.VMEM((d, e_dim), BF16),
                            pltpu.VMEM((e_dim, d), BF16)]),
        out_shape=jax.ShapeDtypeStruct(xs.shape, F32),
        compiler_params=_params("arbitrary"),
        name="moe_experts",
    )(block_e, block_new, n_used, xs, w1, w3, w2)


def _combine_kernel(pos_ref, x_ref, gate_ref, g2_ref, fw_ref, y_hbm, o_ref, buf, sems, *,
                    tok_offset, n_tok, final_norm):
    i = pl.program_id(0)
    n = pl.num_programs(0)
    tm = x_ref.shape[0]

    def issue(step, slot):
        def body(r, carry):
            tok = tok_offset + step * tm + r
            for k in range(TOP_K):
                pltpu.make_async_copy(y_hbm.at[pos_ref[k * n_tok + tok]],
                                      buf.at[slot, k, r], sems.at[slot]).start()
            return carry
        lax.fori_loop(0, tm, body, 0, unroll=8)

    @pl.when(i == 0)
    def _():
        issue(0, 0)

    @pl.when(i + 1 < n)
    def _():
        issue(i + 1, (i + 1) % 2)

    slot = i % 2
    pltpu.make_async_copy(buf.at[slot], buf.at[slot], sems.at[slot]).wait()
    gate = gate_ref[...]
    f = (_tiles_to_rows(buf[slot, 0]) * gate[:, 0:1] + _tiles_to_rows(buf[slot, 1]) * gate[:, 1:2])
    out = x_ref[...] + g2_ref[...] * f
    if final_norm:
        out = _rms(out) * fw_ref[...]
    o_ref[...] = out


def _combine(pos, x2d, seq_len, gate, g2, fw, y, tok_offset, final_norm):
    rows, d = x2d.shape
    n_tok = gate.shape[0]
    tm = min(256, seq_len)
    tiles_per_seq = seq_len // tm
    off = tok_offset // tm
    nb = g2.shape[0]
    mod_idx = ((lambda i, p: (i // tiles_per_seq, 0, 0)) if nb > 1 else (lambda i, p: (0, 0, 0)))
    return pl.pallas_call(
        functools.partial(_combine_kernel, tok_offset=tok_offset, n_tok=n_tok,
                          final_norm=final_norm),
        grid_spec=pltpu.PrefetchScalarGridSpec(
            num_scalar_prefetch=1,
            grid=(rows // tm,),
            in_specs=[pl.BlockSpec((tm, d), lambda i, p: (i, 0)),
                      pl.BlockSpec((tm, TOP_K), lambda i, p: (i + off, 0)),
                      pl.BlockSpec((None, 1, d), mod_idx),
                      pl.BlockSpec((1, d), lambda i, p: (0, 0)),
                      pl.BlockSpec(memory_space=pl.ANY)],
            out_specs=pl.BlockSpec((tm, d), lambda i, p: (i, 0)),
            scratch_shapes=[pltpu.VMEM((2, TOP_K, tm) + y.shape[1:], F32),
                            pltpu.SemaphoreType.DMA((2,))]),
        out_shape=jax.ShapeDtypeStruct((rows, d), F32),
        compiler_params=_params("arbitrary"),
        name="moe_combine",
    )(pos, x2d, gate, g2, fw, y)


def _rope_tables(seq_len):
    rows = seq_len // GRID_W
    row = np.repeat(np.arange(rows), GRID_W).astype(np.float64)
    col = np.tile(np.arange(GRID_W), rows).astype(np.float64)
    n = HEAD_DIM // 4
    inv = (ROPE_THETA ** (-np.arange(n, dtype=np.float32) / n)).astype(np.float64)
    ang = np.concatenate([row[:, None] * inv, col[:, None] * inv], axis=-1)
    cos = np.cos(ang.astype(np.float32).astype(np.float64))
    sin = np.sin(ang.astype(np.float32).astype(np.float64))
    cos2 = np.concatenate([cos, cos], axis=-1).astype(np.float32)
    sin2 = np.concatenate([-sin, sin], axis=-1).astype(np.float32)
    return jnp.asarray(cos2), jnp.asarray(sin2)


def kernel(x, c, ctx, c_ctx, w_ada, b_ada, norm1_w, norm2_w, w_in, q_norm_w, k_norm_w,
           hy_conv_w, hy_conv_b, hy_pe_w1, hy_pe_b1, hy_freq1, hy_pe_w2, hy_pe_b2, hy_freq2,
           hy_pe_w3, hy_skip, w_att_proj, w_hy_proj, w_out, router_w, router_b,
           exp_w1, exp_w3, exp_w2, final_norm_w):
    B, S, D = x.shape
    C = ctx.shape[1]
    depth = w_ada.shape[0]
    n_lat = B * S
    n_ctx = B * C

    cos2, sin2 = _rope_tables(S)
    cos_id = jnp.ones((C, HEAD_DIM), F32)
    sin_id = jnp.zeros((C, HEAD_DIM), F32)
    dft_xf = _dft_matrices(S, False, S // 2)
    dft_cf = _dft_matrices(C, False, C // 2)
    dft_x = _dft_matrices(S // 2, True)
    dft_c = _dft_matrices(C // 2, True)

    mod_rows = 16
    cs = jnp.zeros((mod_rows, D), F32).at[:B].set(c).at[B].set(c_ctx)
    mods = _adaln(cs, w_ada, b_ada)

    r_hi = router_w.T.astype(BF16)
    r_lo = (router_w.T - r_hi.astype(F32)).astype(BF16)
    fw = final_norm_w.reshape(1, D)

    x2d = x.reshape(n_lat, D)
    c2d = ctx.reshape(n_ctx, D)
    for i in range(depth):
        last = i == depth - 1
        m_lat = [mods[i, :B, j * D:(j + 1) * D].reshape(B, 1, D) for j in range(6)]
        m_ctx = [mods[i, B:B + 1, j * D:(j + 1) * D].reshape(1, 1, D) for j in range(6)]
        wb = w_in[i].astype(BF16)
        n1 = norm1_w[i].reshape(1, D)
        n2 = norm2_w[i].reshape(1, D)
        qn = q_norm_w[i].reshape(1, HEAD_DIM)
        kn = k_norm_w[i].reshape(1, HEAD_DIM)
        wa = w_att_proj[i].astype(BF16)
        wh = w_hy_proj[i].astype(BF16)
        wo = w_out[i].astype(BF16)

        q, k, v, u, g = _inproj(x2d, S, m_lat[0], m_lat[1], n1, wb, qn, kn, cos2, sin2,
                                ("q", "k", "v", "u", "g"))
        if last:
            kc, vc = _inproj(c2d, C, m_ctx[0], m_ctx[1], n1, wb, qn, kn, cos_id, sin_id, ("k", "v"))
        else:
            qc, kc, vc, uc, gc = _inproj(c2d, C, m_ctx[0], m_ctx[1], n1, wb, qn, kn, cos_id, sin_id,
                                         ("q", "k", "v", "u", "g"))
        ya = _attention(q, [(k, v, S), (kc, vc, C)], B, S)

        filt = (hy_pe_w1[i], hy_pe_b1[i], hy_freq1[i], hy_pe_w2[i], hy_pe_b2[i], hy_freq2[i],
                hy_pe_w3[i])
        kpack = _hy_spectrum(dft_xf[0], dft_xf[1], *_hy_filter(S, *filt))
        yh = _hyena(u, B, S, dft_x, kpack, hy_conv_w[i], hy_conv_b[i], hy_skip[i])

        n_tok = n_lat if last else n_lat + n_ctx
        x2d, hx, logits = _merge(ya, yh, g, x2d, S, m_lat[2], m_lat[3], m_lat[4], n2,
                                 wa, wh, wo, r_hi, r_lo, n_tok, 0)
        if not last:
            ya_c = _attention(qc, [(kc, vc, C)], B, C)
            kpack = _hy_spectrum(dft_cf[0], dft_cf[1], *_hy_filter(C, *filt))
            yh_c = _hyena(uc, B, C, dft_c, kpack, hy_conv_w[i], hy_conv_b[i], hy_skip[i])
            c2d, hx, logits = _merge(ya_c, yh_c, gc, c2d, C, m_ctx[2], m_ctx[3], m_ctx[4], n2,
                                     wa, wh, wo, r_hi, r_lo, n_tok, n_lat, prev=(hx, logits))

        idx, gate, rank, counts = _route(logits, router_b)
        n_blocks = -(-(n_tok * TOP_K) // MOE_ROWS) + N_EXPERTS
        pad_start, pad_lo, pad_hi, block_e, block_new, n_used = _layout(counts, n_blocks)
        dest = _slots(idx, rank, pad_start).reshape(TOP_K * n_tok)
        xs = _scatter_rows(dest, pad_lo, pad_hi, hx, n_blocks)
        y = _experts(block_e, block_new, n_used, xs, exp_w1, exp_w3, exp_w2, i)
        gate_t = gate.T
        x2d = _combine(dest, x2d, S, gate_t, m_lat[5], fw, y, 0, last)
        if not last:
            c2d = _combine(dest, c2d, C, gate_t, m_ctx[5], fw, y, n_lat, False)
    return x2d.reshape(B, S, D)
```

```python
import functools
import math

import numpy as np
import jax
import jax.numpy as jnp
from jax import lax
from jax.experimental import pallas as pl
from jax.experimental.pallas import tpu as pltpu

F32 = jnp.float32
BF16 = jnp.bfloat16

D_MODEL = 1024
DEPTH = 2
GRID_W = 64
NORM_EPS = 1e-6
N_HEADS = 8
N_KV_HEADS = 2
HEAD_DIM = 128
ATT_WIDTH = N_HEADS * HEAD_DIM
KV_WIDTH = N_KV_HEADS * HEAD_DIM
ROPE_THETA = 10000.0
ATT_SCALE = HEAD_DIM ** -0.5
LOG2_E = math.log2(math.e)
HY_WIDTH = D_MODEL // 2
HY_ORDER = 2
HY_FILTER_HIDDEN = 64
HY_BANDS = 16
HY_PE_DIM = 1 + 2 * HY_BANDS
HY_PE_PAD = 128
HY_FAST_DECAY = 0.3
HY_SLOW_DECAY = 1.5
HY_DECAY_TARGET = 1e-2
N_EXPERTS = 64
N_GROUPS = 8
EXPERTS_PER_GROUP = N_EXPERTS // N_GROUPS
TOP_K = 2
EXPERT_DIM = 512
IN_WIDTH = ATT_WIDTH + 2 * KV_WIDTH + 3 * HY_WIDTH + 2 * D_MODEL
COL_Q = 0
COL_K = ATT_WIDTH
COL_V = ATT_WIDTH + KV_WIDTH
COL_U = ATT_WIDTH + 2 * KV_WIDTH
COL_G = COL_U + 3 * HY_WIDTH

MXU_COLS = 256
SUBLANES = 8
LANES = 128
ROW_TILE = 512
ATT_Q_TILE = 512
ATT_KEY_CHUNK = 512
DFT_ROWS = 64
HY_CHAINS = 4
MOE_ROWS = 256
VMEM_LIMIT = 56 * 1024 * 1024


def _dot(a, b):
    return jnp.dot(a, b, preferred_element_type=F32)


def _split(a):
    hi = a.astype(BF16)
    lo = (a - hi.astype(F32)).astype(BF16)
    return hi, lo


def _dot3(a, b):
    ah, al = _split(a)
    bh, bl = _split(b)
    return _dot(ah, bh) + (_dot(al, bh) + _dot(ah, bl))


def _tiles_to_rows(t):
    c = pltpu.einshape("tjl->jtl", t)
    return jnp.concatenate([c[j] for j in range(c.shape[0])], axis=1)


def _rows_to_tiles(x):
    c = jnp.stack([x[:, j * LANES:(j + 1) * LANES] for j in range(x.shape[1] // LANES)], axis=0)
    return pltpu.einshape("jtl->tjl", c)


def _rms(t):
    return t * lax.rsqrt(jnp.mean(t * t, axis=-1, keepdims=True) + NORM_EPS)


def _params(*sem):
    return pltpu.CompilerParams(dimension_semantics=sem, vmem_limit_bytes=VMEM_LIMIT)


def _adaln_kernel(c_ref, w_ref, b_ref, o_ref):
    c = c_ref[...]
    o_ref[...] = _dot3(c * jax.nn.sigmoid(c), w_ref[...]) + b_ref[...]


def _adaln(cs, w_ada, b_ada):
    depth, d, n = w_ada.shape
    rows = cs.shape[0]
    tn = 1536
    return pl.pallas_call(
        _adaln_kernel,
        grid=(depth, n // tn),
        in_specs=[pl.BlockSpec((rows, d), lambda l, j: (0, 0)),
                  pl.BlockSpec((None, d, tn), lambda l, j: (l, 0, j)),
                  pl.BlockSpec((None, 1, tn), lambda l, j: (l, 0, j))],
        out_specs=pl.BlockSpec((None, rows, tn), lambda l, j: (l, 0, j)),
        out_shape=jax.ShapeDtypeStruct((depth, rows, n), F32),
        compiler_params=_params("parallel", "parallel"),
        name="adaln",
    )(cs, w_ada, b_ada.reshape(depth, 1, n))


def _inproj_kernel(x_ref, sh_ref, sc_ref, nw_ref, w_ref, qn_ref, kn_ref, cos_ref, sin_ref,
                   *outs, sections):
    h = _rms(x_ref[...]) * nw_ref[...]
    h = h * (1.0 + sc_ref[...]) + sh_ref[...]
    hb = h.astype(BF16)
    cos = cos_ref[...]
    sin = sin_ref[...]
    o = dict(zip(sections, outs))

    def head_cols(col0, n_cols, norm_w, scale, out_ref):
        for c in range(n_cols // MXU_COLS):
            acc = _dot(hb, w_ref[:, col0 + c * MXU_COLS:col0 + (c + 1) * MXU_COLS])
            for j in range(MXU_COLS // HEAD_DIM):
                t = _rms(acc[:, j * HEAD_DIM:(j + 1) * HEAD_DIM]) * norm_w
                t = t * cos + pltpu.roll(t, HEAD_DIM // 2, 1) * sin
                lo = c * MXU_COLS + j * HEAD_DIM
                out_ref[:, lo:lo + HEAD_DIM] = (t * scale).astype(out_ref.dtype)

    if "q" in o:
        head_cols(COL_Q, ATT_WIDTH, qn_ref[...], ATT_SCALE * LOG2_E, o["q"])
    if "k" in o:
        head_cols(COL_K, KV_WIDTH, kn_ref[...], 1.0, o["k"])
    if "v" in o:
        acc = _dot(hb, w_ref[:, COL_V:COL_V + KV_WIDTH]).astype(BF16)
        ones = jnp.ones((acc.shape[0], HEAD_DIM), BF16)
        for j in range(N_KV_HEADS):
            o["v"][:, 2 * j * HEAD_DIM:(2 * j + 1) * HEAD_DIM] = acc[:, j * HEAD_DIM:(j + 1) * HEAD_DIM]
            o["v"][:, (2 * j + 1) * HEAD_DIM:(2 * j + 2) * HEAD_DIM] = ones
    if "u" in o:
        for c in range(3):
            o["u"][:, c * HY_WIDTH:(c + 1) * HY_WIDTH] = _dot(
                hb, w_ref[:, COL_U + c * HY_WIDTH:COL_U + (c + 1) * HY_WIDTH])
    if "g" in o:
        for c in range(4):
            acc = _dot(hb, w_ref[:, COL_G + c * 512:COL_G + (c + 1) * 512])
            o["g"][:, c * 512:(c + 1) * 512] = jax.nn.sigmoid(acc).astype(BF16)


_SECTION_SHAPES = {"q": (ATT_WIDTH, BF16), "k": (KV_WIDTH, BF16), "v": (2 * KV_WIDTH, BF16),
                   "u": (3 * HY_WIDTH, F32), "g": (2 * D_MODEL, BF16)}


def _inproj(x2d, seq_len, sh, sc, nw, w_bf16, qn, kn, cos2, sin2, sections):
    rows, d = x2d.shape
    tm = min(ROW_TILE, seq_len)
    tiles_per_seq = seq_len // tm
    nb = sh.shape[0]
    mod_idx = (lambda i: (i // tiles_per_seq, 0, 0)) if nb > 1 else (lambda i: (0, 0, 0))
    const2 = lambda i: (0, 0)
    out_shape = [jax.ShapeDtypeStruct((rows, _SECTION_SHAPES[s][0]), _SECTION_SHAPES[s][1])
                 for s in sections]
    out_specs = [pl.BlockSpec((tm, _SECTION_SHAPES[s][0]), lambda i: (i, 0)) for s in sections]
    return pl.pallas_call(
        functools.partial(_inproj_kernel, sections=tuple(sections)),
        grid=(rows // tm,),
        in_specs=[pl.BlockSpec((tm, d), lambda i: (i, 0)),
                  pl.BlockSpec((None, 1, d), mod_idx),
                  pl.BlockSpec((None, 1, d), mod_idx),
                  pl.BlockSpec((1, d), const2),
                  pl.BlockSpec((d, IN_WIDTH), const2),
                  pl.BlockSpec((1, HEAD_DIM), const2),
                  pl.BlockSpec((1, HEAD_DIM), const2),
                  pl.BlockSpec((tm, HEAD_DIM), lambda i: (i % tiles_per_seq, 0)),
                  pl.BlockSpec((tm, HEAD_DIM), lambda i: (i % tiles_per_seq, 0))],
        out_specs=out_specs,
        out_shape=out_shape,
        compiler_params=_params("parallel"),
        name="inproj",
    )(x2d, sh, sc, nw, w_bf16, qn, kn, cos2, sin2)


def _attn_kernel(q_ref, *refs, n_kv_sets):
    kv = refs[:2 * n_kv_sets]
    o_ref = refs[2 * n_kv_sets]
    nt = (((1,), (1,)), ((), ()))
    for g in range(N_HEADS // N_KV_HEADS):
        q = q_ref[:, g * HEAD_DIM:(g + 1) * HEAD_DIM]
        m = None
        acc = None
        for s_idx in range(n_kv_sets):
            k_ref, v_ref = kv[2 * s_idx], kv[2 * s_idx + 1]
            lk = k_ref.shape[0]
            for c0 in range(0, lk, ATT_KEY_CHUNK):
                c1 = min(c0 + ATT_KEY_CHUNK, lk)
                s = lax.dot_general(q, k_ref[c0:c1, :], nt, preferred_element_type=F32)
                mc = jnp.max(s, axis=1, keepdims=True)
                if m is None:
                    m = mc
                    acc = _dot(jnp.exp2(s - m).astype(BF16), v_ref[c0:c1, :])
                else:
                    m_new = jnp.maximum(m, mc)
                    acc = jnp.exp2(m - m_new) * acc + _dot(jnp.exp2(s - m_new).astype(BF16),
                                                           v_ref[c0:c1, :])
                    m = m_new
        o_ref[:, g * HEAD_DIM:(g + 1) * HEAD_DIM] = (
            acc[:, :HEAD_DIM] / acc[:, HEAD_DIM:]).astype(o_ref.dtype)


def _attention(q, kv_sets, batch, seq_len):
    tq = min(ATT_Q_TILE, seq_len)
    nq = seq_len // tq
    grp = (N_HEADS // N_KV_HEADS) * HEAD_DIM
    in_specs = [pl.BlockSpec((tq, grp), lambda b, j, i: (b * nq + i, j))]
    args = [q]
    for k, v, lk in kv_sets:
        in_specs += [pl.BlockSpec((lk, HEAD_DIM), lambda b, j, i: (b, j)),
                     pl.BlockSpec((lk, 2 * HEAD_DIM), lambda b, j, i: (b, j))]
        args += [k, v]
    return pl.pallas_call(
        functools.partial(_attn_kernel, n_kv_sets=len(kv_sets)),
        grid=(batch, N_KV_HEADS, nq),
        in_specs=in_specs,
        out_specs=pl.BlockSpec((tq, grp), lambda b, j, i: (b * nq + i, j)),
        out_shape=jax.ShapeDtypeStruct(q.shape, BF16),
        compiler_params=_params("parallel", "parallel", "arbitrary"),
        name="attention",
    )(*args)


def _hy_filter_kernel(z_ref, t_ref, w1_ref, b1_ref, f1_ref, w2_ref, b2_ref, f2_ref, w3_ref,
                      dl_ref, sp_ref, sm_ref, spa_ref, sma_ref, mid_ref):
    i = pl.program_id(0)
    tl = z_ref.shape[0]
    h = jnp.sin(f1_ref[...] * (_dot3(z_ref[...], w1_ref[...]) + b1_ref[...]))
    h = jnp.sin(f2_ref[...] * (_dot3(h, w2_ref[...]) + b2_ref[...]))
    h = _dot3(h, w3_ref[...])
    window = jnp.exp(-t_ref[...] * dl_ref[...])
    row = i * tl + lax.broadcasted_iota(jnp.int32, (tl, HY_WIDTH), 0)
    alt = (1 - 2 * (row & 1)).astype(F32)
    quarter = row & 3
    cos4 = jnp.where(quarter == 0, 1.0, jnp.where(quarter == 2, -1.0, 0.0))
    sin4 = jnp.where(quarter == 1, 1.0, jnp.where(quarter == 3, -1.0, 0.0))

    @pl.when(i == 0)
    def _():
        mid_ref[...] = jnp.zeros_like(mid_ref)

    for o in range(HY_ORDER):
        cols = slice(o * HY_WIDTH, (o + 1) * HY_WIDTH)
        hf = h[:, (2 * o) * HY_WIDTH:(2 * o + 1) * HY_WIDTH] * window
        hb = h[:, (2 * o + 1) * HY_WIDTH:(2 * o + 2) * HY_WIDTH] * window
        hb = jnp.where(row == 0, 0.0, hb)
        plus = hf + hb
        minus = hf - hb
        sp_ref[:, cols] = plus.astype(BF16)
        sm_ref[:, cols] = minus.astype(BF16)
        spa_ref[:, cols] = (alt * plus).astype(BF16)
        sma_ref[:, cols] = (alt * minus).astype(BF16)
        mid_ref[0:1, cols] += jnp.sum(cos4 * plus, axis=0, keepdims=True)
        mid_ref[1:2, cols] += jnp.sum(sin4 * minus, axis=0, keepdims=True)


def _hy_filter(seq_len, pe_w1, pe_b1, freq1, pe_w2, pe_b2, freq2, pe_w3):
    t01 = np.linspace(0.0, 1.0, seq_len)[:, None]
    pos = np.arange(seq_len, dtype=np.float64)[:, None]
    bands = np.linspace(1e-4, HY_BANDS - 1, HY_BANDS)[None, :]
    f = 2.0 * math.pi * pos * bands / seq_len
    z = np.zeros((seq_len, HY_PE_PAD), np.float32)
    z[:, :HY_PE_DIM] = np.concatenate([t01, np.cos(f), -np.sin(f)], axis=-1)
    max_decay = math.log(HY_DECAY_TARGET) / HY_FAST_DECAY
    min_decay = math.log(HY_DECAY_TARGET) / HY_SLOW_DECAY
    deltas = np.abs(np.linspace(min_decay, max_decay, HY_WIDTH))[None, :].astype(np.float32)
    w1p = jnp.zeros((HY_PE_PAD, HY_FILTER_HIDDEN), F32).at[:HY_PE_DIM].set(pe_w1)
    tl = min(512, seq_len)
    hid = HY_FILTER_HIDDEN
    n_out = HY_ORDER * HY_WIDTH
    c2 = lambda i: (0, 0)
    return pl.pallas_call(
        _hy_filter_kernel,
        grid=(seq_len // tl,),
        in_specs=[pl.BlockSpec((tl, HY_PE_PAD), lambda i: (i, 0)),
                  pl.BlockSpec((tl, 1), lambda i: (i, 0)),
                  pl.BlockSpec((HY_PE_PAD, hid), c2), pl.BlockSpec((1, hid), c2),
                  pl.BlockSpec((1, hid), c2), pl.BlockSpec((hid, hid), c2),
                  pl.BlockSpec((1, hid), c2), pl.BlockSpec((1, hid), c2),
                  pl.BlockSpec((hid, 2 * n_out), c2), pl.BlockSpec((1, HY_WIDTH), c2)],
        out_specs=[pl.BlockSpec((tl, n_out), lambda i: (i, 0))] * 4
                  + [pl.BlockSpec((SUBLANES, n_out), c2)],
        out_shape=[jax.ShapeDtypeStruct((seq_len, n_out), BF16)] * 4
                  + [jax.ShapeDtypeStruct((SUBLANES, n_out), F32)],
        compiler_params=_params("arbitrary"),
        name="hy_filter",
    )(jnp.asarray(z), jnp.asarray(t01.astype(np.float32)), w1p, pe_b1.reshape(1, hid),
      freq1.reshape(1, hid), pe_w2, pe_b2.reshape(1, hid), freq2.reshape(1, hid), pe_w3,
      jnp.asarray(deltas))


def _dft_kernel(ca_ref, sa_ref, cb_ref, sb_ref, alt_ref, wc_ref, cm_ref, sm_ref, *inverse_refs):
    i = pl.program_id(0)
    ca = ca_ref[...]
    sa = sa_ref[...]
    cb = cb_ref[...]
    sb = sb_ref[...]
    c = ca * cb - sa * sb
    s = sa * cb + ca * sb
    rows = lax.broadcasted_iota(jnp.int32, c.shape, 0)
    cols = lax.broadcasted_iota(jnp.int32, c.shape, 1)
    wc = wc_ref[...]
    cm_ref[...] = c.astype(BF16)
    sm_ref[...] = jnp.where((rows == 0) & (i == 0), alt_ref[...], s).astype(BF16)
    if inverse_refs:
        ci_ref, si_ref = inverse_refs
        alt_t = (1 - 2 * (rows & 1)).astype(F32)
        ci_ref[...] = (c * wc).astype(BF16)
        si_ref[...] = (jnp.where(cols == 0, alt_t, s) * wc).astype(BF16)


def _dft_matrices(seq_len, inverse, n_rows=None):
    n = 2 * seq_len
    n_rows = seq_len if n_rows is None else n_rows
    idx = np.arange(seq_len, dtype=np.int64)[None, :]
    r1 = np.arange(n_rows // DFT_ROWS, dtype=np.int64)[:, None] * DFT_ROWS
    r0 = np.arange(DFT_ROWS, dtype=np.int64)[:, None]
    ang_a = ((r1 * idx) % n).astype(np.float64) * (2.0 * math.pi / n)
    ang_b = ((r0 * idx) % n).astype(np.float64) * (2.0 * math.pi / n)
    tab = lambda a: jnp.asarray(a.astype(np.float32))
    ca = tab(np.cos(ang_a)).reshape(-1, 1, seq_len)
    sa = tab(np.sin(ang_a)).reshape(-1, 1, seq_len)
    alt = tab(1.0 - 2.0 * (idx % 2))
    wc = tab(np.where(idx == 0, 1.0, 2.0) / n)
    row_blk = pl.BlockSpec((None, 1, seq_len), lambda i: (i, 0, 0))
    full = lambda r: pl.BlockSpec((r, seq_len), lambda i: (0, 0))
    out_blk = pl.BlockSpec((DFT_ROWS, seq_len), lambda i: (i, 0))
    n_out = 4 if inverse else 2
    mats = pl.pallas_call(
        _dft_kernel,
        grid=(n_rows // DFT_ROWS,),
        in_specs=[row_blk, row_blk, full(DFT_ROWS), full(DFT_ROWS), full(1), full(1)],
        out_specs=[out_blk] * n_out,
        out_shape=[jax.ShapeDtypeStruct((n_rows, seq_len), BF16)] * n_out,
        compiler_params=_params("parallel"),
        name="dft_matrices",
    )(ca, sa, tab(np.cos(ang_b)), tab(np.sin(ang_b)), alt, wc)
    if not inverse:
        return tuple(mats)
    tw = np.arange(seq_len, dtype=np.float64)[:, None] * (2.0 * math.pi / (2 * n))
    rep = lambda a: tab(np.broadcast_to(a, (seq_len, LANES)))
    return tuple(mats) + (rep(np.cos(tw)), rep(np.sin(tw)))


def _hy_spec_kernel(cm_ref, sm_ref, sp_ref, smn_ref, spa_ref, sma_ref, mid_ref,
                    klr_ref, kli_ref, khr_ref, khi_ref):
    i = pl.program_id(1)
    cm = cm_ref[...]
    sm = sm_ref[...]
    klr_ref[...] = _dot(cm, sp_ref[...])
    khr_ref[...] = _dot(cm, spa_ref[...])
    lo_q = _dot(sm, smn_ref[...])
    hi_q = _dot(sm, sma_ref[...])
    rows = lax.broadcasted_iota(jnp.int32, lo_q.shape, 0)
    edge = (rows == 0) & (i == 0)
    kli_ref[...] = jnp.where(edge, mid_ref[0:1, :], -lo_q)
    khi_ref[...] = jnp.where(edge, -mid_ref[1:2, :], hi_q)


def _hy_spectrum(cm, sm, splus, sminus, splus_alt, sminus_alt, mid):
    half, seq_len = cm.shape
    tf = min(256, half)
    mat = pl.BlockSpec((tf, seq_len), lambda o, i: (i, 0))
    sig = pl.BlockSpec((seq_len, HY_WIDTH), lambda o, i: (0, o))
    out = pl.BlockSpec((tf, HY_WIDTH), lambda o, i: (i, o))
    return pl.pallas_call(
        _hy_spec_kernel,
        grid=(HY_ORDER, half // tf),
        in_specs=[mat, mat, sig, sig, sig, sig,
                  pl.BlockSpec((SUBLANES, HY_WIDTH), lambda o, i: (0, o))],
        out_specs=[out] * 4,
        out_shape=[jax.ShapeDtypeStruct((half, HY_ORDER * HY_WIDTH), F32)] * 4,
        compiler_params=_params("parallel", "arbitrary"),
        name="hy_spectrum",
    )(cm, sm, splus, sminus, splus_alt, sminus_alt, mid)


def _hy_sconv_kernel(u_ref, w_ref, b_ref, ue_ref, uo_ref, ve_ref, vo_ref, *, v_first_block):
    j = pl.program_id(1)
    half = u_ref.shape[0] // 2
    tc = u_ref.shape[1]
    x2 = pltpu.einshape("(tp)c->t(pc)", u_ref[...], p=2)
    xe = x2[:, :tc]
    xo = x2[:, tc:]
    rows = lax.broadcasted_iota(jnp.int32, xe.shape, 0)
    xo_prev = jnp.where(rows == 0, 0.0, pltpu.roll(xo, 1, 0))
    xe_next = jnp.where(rows == half - 1, 0.0, pltpu.roll(xe, half - 1, 0))
    w0, w1, w2 = w_ref[0:1, :], w_ref[1:2, :], w_ref[2:3, :]
    ye = xo_prev * w0 + xe * w1 + xo * w2 + b_ref[...]
    yo = xe * w0 + xo * w1 + xe_next * w2 + b_ref[...]
    ue_ref[...] = ye
    uo_ref[...] = yo

    @pl.when(j >= v_first_block)
    def _():
        ve_ref[...] = ye.astype(BF16)
        vo_ref[...] = yo.astype(BF16)


def _hy_sconv(u, batch, seq_len, conv_w, conv_b):
    ch = u.shape[1]
    tc = 256
    v0 = (2 * HY_WIDTH) // tc
    half = seq_len // 2
    u3 = u.reshape(batch, seq_len, ch)
    plane = pl.BlockSpec((None, half, tc), lambda b, j: (b, 0, j))
    vplane = pl.BlockSpec((None, half, tc), lambda b, j: (b, 0, jnp.maximum(j - v0, 0)))
    ue, uo, ve, vo = pl.pallas_call(
        functools.partial(_hy_sconv_kernel, v_first_block=v0),
        grid=(batch, ch // tc),
        in_specs=[pl.BlockSpec((None, seq_len, tc), lambda b, j: (b, 0, j)),
                  pl.BlockSpec((3, tc), lambda b, j: (0, j)),
                  pl.BlockSpec((1, tc), lambda b, j: (0, j))],
        out_specs=[plane, plane, vplane, vplane],
        out_shape=[jax.ShapeDtypeStruct((batch, half, ch), F32)] * 2
                  + [jax.ShapeDtypeStruct((batch, half, HY_WIDTH), BF16)] * 2,
        compiler_params=_params("parallel", "arbitrary"),
        name="hy_sconv",
    )(u3, conv_w, conv_b.reshape(1, ch))
    return (ue, uo), (ve, vo)


def _hy_fwd_kernel(cm_ref, sm_ref, tc_ref, ts_ref, ve_ref, vo_ref, klr_ref, kli_ref, khr_ref,
                   khi_ref, zp_ref, zq_ref):
    i = pl.program_id(0)
    w = ve_ref.shape[1]
    wc = w // HY_CHAINS
    reps = wc // LANES
    tc = jnp.concatenate([tc_ref[...]] * reps, axis=1)
    ts = jnp.concatenate([ts_ref[...]] * reps, axis=1)
    rows = lax.broadcasted_iota(jnp.int32, (cm_ref.shape[0], wc), 0)
    edge = (rows == 0) & (i == 0)
    cm = cm_ref[...]
    sm = sm_ref[...]
    for c in range(HY_CHAINS):
        lo, hi = c * wc, (c + 1) * wc
        v = jnp.concatenate([ve_ref[:, lo:hi], vo_ref[:, lo:hi]], axis=1)
        p = _dot(cm, v)
        q = _dot(sm, v)
        pe, po, qe, qo = p[:, :wc], p[:, wc:], q[:, :wc], q[:, wc:]

        a = tc * po - ts * qo
        b = tc * qo + ts * po
        xlr = pe + a
        xhr = pe - a
        xli = jnp.where(edge, qe, -(qe + b))
        xhi = jnp.where(edge, -qo, qe - b)

        klr, kli, khr, khi = klr_ref[:, lo:hi], kli_ref[:, lo:hi], khr_ref[:, lo:hi], khi_ref[:, lo:hi]
        ylr = xlr * klr - jnp.where(edge, 0.0, xli * kli)
        yhr = xhr * khr - jnp.where(edge, 0.0, xhi * khi)
        yli = jnp.where(edge, xli * kli - xhi * khi, xlr * kli + xli * klr)
        yhi = jnp.where(edge, xli * khi + xhi * kli, xhr * khi + xhi * khr)

        dr = ylr - yhr
        di = yli + yhi
        zp_ref[:, lo:hi] = (0.5 * (ylr + yhr)).astype(BF16)
        zp_ref[:, w + lo:w + hi] = (0.5 * (tc * dr - ts * di)).astype(BF16)
        zq_ref[:, lo:hi] = jnp.where(edge, yli, -0.5 * (yli - yhi)).astype(BF16)
        zq_ref[:, w + lo:w + hi] = jnp.where(edge, -yhi, -0.5 * (tc * di + ts * dr)).astype(BF16)


def _hy_forward(dfth, v_planes, kpack, order):
    cm, sm, _, _, tc, ts = dfth
    batch, half, w = v_planes[0].shape
    tf = min(512, half)
    mat = pl.BlockSpec((tf, half), lambda i, b: (i, 0))
    tw = pl.BlockSpec((tf, LANES), lambda i, b: (i, 0))
    spec = pl.BlockSpec((tf, w), lambda i, b: (i, order))
    out = pl.BlockSpec((None, tf, 2 * w), lambda i, b: (b, i, 0))
    return pl.pallas_call(
        _hy_fwd_kernel,
        grid=(half // tf, batch),
        in_specs=[mat, mat, tw, tw,
                  pl.BlockSpec((None, half, w), lambda i, b: (b, 0, 0)),
                  pl.BlockSpec((None, half, w), lambda i, b: (b, 0, 0)),
                  spec, spec, spec, spec],
        out_specs=[out, out],
        out_shape=[jax.ShapeDtypeStruct((batch, half, 2 * w), BF16)] * 2,
        compiler_params=_params("parallel", "arbitrary"),
        name="hy_forward",
    )(cm, sm, tc, ts, v_planes[0], v_planes[1], *kpack)


def _hy_inv_kernel(ci_ref, si_ref, zp_ref, zq_ref, ae_ref, ao_ref, ce_ref, co_ref, skip_ref, *outs,
                   interleave):
    w = ae_ref.shape[1]
    tt = ae_ref.shape[0]
    y = _dot(ci_ref[...], zp_ref[...]) + _dot(si_ref[...], zq_ref[...])
    skip = skip_ref[...]
    ze = ae_ref[...] * (y[:, :w] + skip * ce_ref[...])
    zo = ao_ref[...] * (y[:, w:] + skip * co_ref[...])
    if interleave:
        (o_ref,) = outs
        z = jnp.concatenate([ze, zo], axis=1).astype(o_ref.dtype)
        o_ref[...] = pltpu.einshape("t(pw)->(tp)w", z, p=2)
    else:
        for k in range(len(outs) // 2):
            outs[2 * k][...] = ze.astype(outs[2 * k].dtype)
            outs[2 * k + 1][...] = zo.astype(outs[2 * k + 1].dtype)


def _hy_inverse(dfth, zp, zq, a_planes, a_blk, c_planes, c_blk, skip, out_dtypes, interleave):
    _, _, ci, si, _, _ = dfth
    batch, half, w2 = zp.shape
    w = w2 // 2
    tt = min(512, half)
    mat = pl.BlockSpec((tt, half), lambda i, b: (i, 0))
    sig = pl.BlockSpec((None, half, w2), lambda i, b: (b, 0, 0))
    a_spec = pl.BlockSpec((None, tt, w), lambda i, b: (b, i, a_blk))
    c_spec = pl.BlockSpec((None, tt, w), lambda i, b: (b, i, c_blk))
    plane = pl.BlockSpec((None, tt, w), lambda i, b: (b, i, 0))
    if interleave:
        (dt,) = out_dtypes
        out_specs = [pl.BlockSpec((None, 2 * tt, w), lambda i, b: (b, i, 0))]
        out_shape = [jax.ShapeDtypeStruct((batch, 2 * half, w), dt)]
    else:
        out_specs = [plane] * (2 * len(out_dtypes))
        out_shape = [jax.ShapeDtypeStruct((batch, half, w), dt) for dt in out_dtypes for _ in (0, 1)]
    outs = pl.pallas_call(
        functools.partial(_hy_inv_kernel, interleave=interleave),
        grid=(half // tt, batch),
        in_specs=[mat, mat, sig, sig, a_spec, a_spec, c_spec, c_spec,
                  pl.BlockSpec((1, w), lambda i, b: (0, 0))],
        out_specs=out_specs,
        out_shape=out_shape,
        compiler_params=_params("parallel", "arbitrary"),
        name="hy_inverse",
    )(ci, si, zp, zq, a_planes[0], a_planes[1], c_planes[0], c_planes[1], skip)
    if interleave:
        return outs[0]
    return [(outs[2 * k], outs[2 * k + 1]) for k in range(len(out_dtypes))]


def _hyena(u, batch, seq_len, dfth, kpack, conv_w, conv_b, skip):
    uc, vb = _hy_sconv(u, batch, seq_len, conv_w, conv_b)
    zp, zq = _hy_forward(dfth, vb, kpack, 0)
    z, zb = _hy_inverse(dfth, zp, zq, uc, 0, uc, 2, skip[0:1], (F32, BF16), False)
    zp, zq = _hy_forward(dfth, zb, kpack, 1)
    y = _hy_inverse(dfth, zp, zq, uc, 1, z, 0, skip[1:2], (F32,), True)
    return y.reshape(batch * seq_len, HY_WIDTH)


def _merge_kernel(ya_ref, yh_ref, g_ref, x_ref, g1_ref, sh_ref, sc_ref, nw_ref,
                  wa_ref, wh_ref, wo_ref, rh_ref, rl_ref, *rest):
    xo_ref, hx_ref, lg_ref = rest[-3:]
    a = _dot(ya_ref[...], wa_ref[...])
    h = _dot(yh_ref[...].astype(BF16), wh_ref[...])
    m = g_ref[:, :D_MODEL].astype(F32) * a + g_ref[:, D_MODEL:].astype(F32) * h
    xn = x_ref[...] + g1_ref[...] * _dot(m.astype(BF16), wo_ref[...])
    xo_ref[...] = xn
    hx = _rms(xn) * nw_ref[...]
    hx = hx * (1.0 + sc_ref[...]) + sh_ref[...]
    hx_ref[...] = _rows_to_tiles(hx)
    hh, hl = _split(hx)
    nt = (((1,), (1,)), ((), ()))
    dn = lambda a, b: lax.dot_general(a, b, nt, preferred_element_type=F32)
    lg_ref[...] = dn(rh_ref[...], hh) + (dn(rh_ref[...], hl) + dn(rl_ref[...], hh))


def _merge(ya, yh, g, x2d, seq_len, g1, sh2, sc2, nw, wa, wh, wo, r_hi, r_lo,
           total_rows, row_offset, prev=None):
    rows, d = x2d.shape
    tm = min(ROW_TILE, seq_len)
    tiles_per_seq = seq_len // tm
    off = row_offset // tm
    nb = g1.shape[0]
    mod_idx = (lambda i: (i // tiles_per_seq, 0, 0)) if nb > 1 else (lambda i: (0, 0, 0))
    c2 = lambda i: (0, 0)
    row = lambda w: pl.BlockSpec((tm, w), lambda i: (i, 0))
    in_specs = [row(ATT_WIDTH), row(HY_WIDTH), row(2 * d), row(d),
                pl.BlockSpec((None, 1, d), mod_idx), pl.BlockSpec((None, 1, d), mod_idx),
                pl.BlockSpec((None, 1, d), mod_idx), pl.BlockSpec((1, d), c2),
                pl.BlockSpec((ATT_WIDTH, d), c2), pl.BlockSpec((HY_WIDTH, d), c2),
                pl.BlockSpec((d, d), c2), pl.BlockSpec((N_EXPERTS, d), c2),
                pl.BlockSpec((N_EXPERTS, d), c2)]
    args = [ya, yh, g, x2d, g1, sh2, sc2, nw, wa, wh, wo, r_hi, r_lo]
    aliases = {}
    if prev is not None:
        in_specs += [pl.BlockSpec(memory_space=pl.ANY)] * 2
        aliases = {len(args): 1, len(args) + 1: 2}
        args += list(prev)
    return pl.pallas_call(
        _merge_kernel,
        grid=(rows // tm,),
        in_specs=in_specs,
        out_specs=[row(d), pl.BlockSpec((tm, d // LANES, LANES), lambda i: (i + off, 0, 0)),
                   pl.BlockSpec((N_EXPERTS, tm), lambda i: (0, i + off))],
        out_shape=[jax.ShapeDtypeStruct((rows, d), F32),
                   jax.ShapeDtypeStruct((total_rows, d // LANES, LANES), F32),
                   jax.ShapeDtypeStruct((N_EXPERTS, total_rows), F32)],
        input_output_aliases=aliases,
        compiler_params=_params("arbitrary"),
        name="merge",
    )(*args)


def _route_kernel(lg_ref, bias_ref, idx_ref, gate_ref, rank_ref, cnt_ref, run_ref):
    step = pl.program_id(0)
    tn = lg_ref.shape[2]
    shape = (N_GROUPS, EXPERTS_PER_GROUP, tn)
    s = jax.nn.sigmoid(lg_ref[...])
    b = s + bias_ref[...]
    mem = lax.broadcasted_iota(jnp.int32, shape, 1).astype(F32)
    grp = lax.broadcasted_iota(jnp.int32, (N_GROUPS, 1, tn), 0).astype(F32)
    big = float(N_EXPERTS)

    m1 = jnp.max(b, axis=1, keepdims=True)
    i1 = jnp.min(jnp.where(b == m1, mem, big), axis=1, keepdims=True)
    b2 = jnp.where(mem == i1, -jnp.inf, b)
    m2 = jnp.max(b2, axis=1, keepdims=True)
    i2 = jnp.min(jnp.where(b2 == m2, mem, big), axis=1, keepdims=True)
    gs = m1 + m2
    gmax = jnp.max(gs, axis=0, keepdims=True)
    gsel = jnp.min(jnp.where(gs == gmax, grp, big), axis=0, keepdims=True)
    selg = grp == gsel

    @pl.when(step == 0)
    def _():
        run_ref[...] = jnp.zeros_like(run_ref)

    tri = (lax.broadcasted_iota(jnp.int32, (tn, tn), 0)
           < lax.broadcasted_iota(jnp.int32, (tn, tn), 1)).astype(BF16)
    run = run_ref[...]
    ws = []
    for k, ik in enumerate((i1, i2)):
        hit = selg & (mem == ik)
        ws.append(jnp.sum(jnp.where(hit, s, 0.0), axis=(0, 1), keepdims=True))
        e_loc = jnp.sum(jnp.where(selg, ik, 0.0), axis=0, keepdims=True)
        idx_ref[k:k + 1, :] = (gsel * EXPERTS_PER_GROUP + e_loc).reshape(1, tn).astype(jnp.int32)
        oh = jnp.where(hit, 1.0, 0.0).reshape(N_EXPERTS, tn)
        before = run + _dot(oh.astype(BF16), tri)
        rank_ref[k:k + 1, :] = jnp.sum(oh * before, axis=0, keepdims=True).astype(jnp.int32)
        run = run + jnp.sum(oh, axis=1, keepdims=True)
    run_ref[...] = run
    cnt_ref[...] = run
    tot = ws[0] + ws[1]
    for k in range(TOP_K):
        gate_ref[k:k + 1, :] = (ws[k] / tot).reshape(1, tn)


def _route(logits_t, router_b):
    n_tok = logits_t.shape[1]
    tn = 512
    lg3 = logits_t.reshape(N_GROUPS, EXPERTS_PER_GROUP, n_tok)
    row2 = pl.BlockSpec((TOP_K, tn), lambda i: (0, i))
    return pl.pallas_call(
        _route_kernel,
        grid=(n_tok // tn,),
        in_specs=[pl.BlockSpec((N_GROUPS, EXPERTS_PER_GROUP, tn), lambda i: (0, 0, i)),
                  pl.BlockSpec((N_GROUPS, EXPERTS_PER_GROUP, 1), lambda i: (0, 0, 0))],
        out_specs=[row2, row2, row2, pl.BlockSpec((N_EXPERTS, 1), lambda i: (0, 0))],
        out_shape=[jax.ShapeDtypeStruct((TOP_K, n_tok), jnp.int32),
                   jax.ShapeDtypeStruct((TOP_K, n_tok), F32),
                   jax.ShapeDtypeStruct((TOP_K, n_tok), jnp.int32),
                   jax.ShapeDtypeStruct((N_EXPERTS, 1), F32)],
        scratch_shapes=[pltpu.VMEM((N_EXPERTS, 1), F32)],
        compiler_params=_params("arbitrary"),
        name="moe_route",
    )(lg3, router_b.astype(F32).reshape(N_GROUPS, EXPERTS_PER_GROUP, 1))


def _slot_kernel(idx_ref, rank_ref, start_ref, dest_ref):
    tn = idx_ref.shape[1]
    e = lax.broadcasted_iota(jnp.int32, (N_EXPERTS, tn), 0)
    for k in range(TOP_K):
        base = jnp.sum(jnp.where(e == idx_ref[k:k + 1, :], start_ref[...], 0.0), axis=0,
                       keepdims=True)
        dest_ref[k:k + 1, :] = base.astype(jnp.int32) + rank_ref[k:k + 1, :]


def _slots(idx, rank, pad_start):
    n_tok = idx.shape[1]
    tn = 512
    row2 = pl.BlockSpec((TOP_K, tn), lambda i: (0, i))
    return pl.pallas_call(
        _slot_kernel,
        grid=(n_tok // tn,),
        in_specs=[row2, row2, pl.BlockSpec((N_EXPERTS, 1), lambda i: (0, 0))],
        out_specs=row2,
        out_shape=jax.ShapeDtypeStruct((TOP_K, n_tok), jnp.int32),
        compiler_params=_params("parallel"),
        name="moe_slots",
    )(idx, rank, pad_start.astype(F32).reshape(N_EXPERTS, 1))


def _layout(counts, n_blocks):
    counts = counts.reshape(N_EXPERTS).astype(jnp.int32)
    padded = (counts + MOE_ROWS - 1) // MOE_ROWS * MOE_ROWS
    pad_end = jnp.cumsum(padded)
    pad_start = pad_end - padded
    blk_row = jnp.arange(n_blocks, dtype=jnp.int32)[:, None] * MOE_ROWS
    block_e = jnp.minimum(jnp.sum((pad_end[None, :] <= blk_row).astype(jnp.int32), axis=1),
                          N_EXPERTS - 1)
    prev_e = jnp.concatenate([jnp.full((1,), -1, jnp.int32), block_e[:-1]])
    block_new = (block_e != prev_e).astype(jnp.int32)
    block_slot = (jnp.cumsum(block_new) - 1) & 1
    e_ids = jnp.arange(N_EXPERTS, dtype=jnp.int32)
    later = (e_ids[None, :] > e_ids[:, None]) & (counts[None, :] > 0)
    next_e = jnp.min(jnp.where(later, e_ids[None, :], N_EXPERTS), axis=1)
    next_e = jnp.where(next_e == N_EXPERTS, -1, next_e)
    block_next = jnp.sum(jnp.where(block_e[:, None] == e_ids[None, :], next_e[None, :], 0), axis=1)
    n_used = (pad_end[-1] // MOE_ROWS).astype(jnp.int32).reshape(1)
    tables = (block_e, block_new, block_slot.astype(jnp.int32), block_next.astype(jnp.int32), n_used)
    return pad_start, pad_start + counts, pad_end, tables


_PAD_CHUNKS = tuple(2 ** p for p in range(int(math.log2(MOE_ROWS)) - 1, -1, -1))


def _scatter_kernel(dest_ref, lo_ref, hi_ref, x_ref, o_hbm, zeros, sem, zsem, *, n_tok):
    i = pl.program_id(0)
    tm = x_ref.shape[0]

    def pad_copies(act):
        def per_expert(e, carry):
            off = lo_ref[e]
            n = hi_ref[e] - off
            for c in _PAD_CHUNKS:
                @pl.when((n & c) != 0)
                def _():
                    act(pltpu.make_async_copy(zeros.at[pl.ds(0, c)], o_hbm.at[pl.ds(off, c)], zsem))
                off = off + (n & c)
            return carry
        lax.fori_loop(0, N_EXPERTS, per_expert, 0)

    @pl.when(i == 0)
    def _():
        zeros[...] = jnp.zeros_like(zeros)
        pad_copies(lambda cp: cp.start())

    def issue(r, carry):
        for k in range(TOP_K):
            pltpu.make_async_copy(x_ref.at[r], o_hbm.at[dest_ref[k * n_tok + i * tm + r]], sem).start()
        return carry
    lax.fori_loop(0, tm, issue, 0, unroll=8)
    for k in range(TOP_K):
        pltpu.make_async_copy(x_ref, o_hbm.at[pl.ds(0, tm)], sem).wait()

    @pl.when(i == 0)
    def _():
        pad_copies(lambda cp: cp.wait())


def _scatter_rows(dest_flat, pad_lo, pad_hi, h, n_blocks):
    n_tok = h.shape[0]
    tile = h.shape[1:]
    tm = 256
    return pl.pallas_call(
        functools.partial(_scatter_kernel, n_tok=n_tok),
        grid_spec=pltpu.PrefetchScalarGridSpec(
            num_scalar_prefetch=3,
            grid=(n_tok // tm,),
            in_specs=[pl.BlockSpec((tm,) + tile, lambda i, *_: (i, 0, 0))],
            out_specs=pl.BlockSpec(memory_space=pl.ANY),
            scratch_shapes=[pltpu.VMEM((MOE_ROWS // 2,) + tile, h.dtype),
                            pltpu.SemaphoreType.DMA(()), pltpu.SemaphoreType.DMA(())]),
        out_shape=jax.ShapeDtypeStruct((n_blocks * MOE_ROWS,) + tile, h.dtype),
        compiler_params=pltpu.CompilerParams(dimension_semantics=("arbitrary",),
                                             vmem_limit_bytes=VMEM_LIMIT, has_side_effects=True),
        name="moe_scatter",
    )(dest_flat, pad_lo, pad_hi, h)


def _expert_kernel(be_ref, new_ref, slot_ref, next_ref, nused_ref, x_ref, w1_hbm, w3_hbm, w2_hbm,
                   y_ref, w1f, w3f, w2f, w1b, w3b, w2b, sems, *, layer):
    i = pl.program_id(0)
    hbm = (w1_hbm, w3_hbm, w2_hbm)
    f32_bufs = (w1f, w3f, w2f)
    bf16_bufs = (w1b, w3b, w2b)

    def copies(e, s):
        return [pltpu.make_async_copy(hbm[k].at[layer, e], f32_bufs[k].at[s], sems.at[s, k])
                for k in range(3)]

    @pl.when(i < nused_ref[0])
    def _():
        @pl.when(new_ref[i] == 1)
        def _():
            s = slot_ref[i]

            @pl.when(i == 0)
            def _():
                for cp in copies(be_ref[0], 0):
                    cp.start()

            for cp in copies(be_ref[i], s):
                cp.wait()

            @pl.when(next_ref[i] >= 0)
            def _():
                for cp in copies(next_ref[i], 1 - s):
                    cp.start()

            for k in range(3):
                bf16_bufs[k][...] = f32_bufs[k][s].astype(BF16)

        x = _tiles_to_rows(x_ref[...]).astype(BF16)
        a = _dot(x, w1b[...])
        b = _dot(x, w3b[...])
        h = (a * jax.nn.sigmoid(a)) * b
        y_ref[...] = _rows_to_tiles(_dot(h.astype(BF16), w2b[...]))


def _experts(tables, xs, w1, w3, w2, layer):
    n_slots = xs.shape[0]
    tile = xs.shape[1:]
    d = tile[0] * tile[1]
    n_blocks = n_slots // MOE_ROWS
    e_dim = w1.shape[3]
    blk = lambda i, be, bn, bs, bx, nu: (jnp.minimum(i, nu[0] - 1), 0, 0)
    hbm = pl.BlockSpec(memory_space=pl.ANY)
    return pl.pallas_call(
        functools.partial(_expert_kernel, layer=layer),
        grid_spec=pltpu.PrefetchScalarGridSpec(
            num_scalar_prefetch=5,
            grid=(n_blocks,),
            in_specs=[pl.BlockSpec((MOE_ROWS,) + tile, blk), hbm, hbm, hbm],
            out_specs=pl.BlockSpec((MOE_ROWS,) + tile, blk),
            scratch_shapes=[pltpu.VMEM((2, d, e_dim), F32), pltpu.VMEM((2, d, e_dim), F32),
                            pltpu.VMEM((2, e_dim, d), F32),
                            pltpu.VMEM((d, e_dim), BF16), pltpu.VMEM((d, e_dim), BF16),
                            pltpu.VMEM((e_dim, d), BF16),
                            pltpu.SemaphoreType.DMA((2, 3))]),
        out_shape=jax.ShapeDtypeStruct(xs.shape, F32),
        compiler_params=_params("arbitrary"),
        name="moe_experts",
    )(*tables, xs, w1, w3, w2)


def _combine_kernel(pos_ref, x_ref, gate_ref, g2_ref, fw_ref, y_hbm, o_ref, buf, sems, *,
                    tok_offset, n_tok, final_norm):
    i = pl.program_id(0)
    n = pl.num_programs(0)
    tm = x_ref.shape[0]

    def issue(step, slot):
        def body(r, carry):
            tok = tok_offset + step * tm + r
            for k in range(TOP_K):
                pltpu.make_async_copy(y_hbm.at[pos_ref[k * n_tok + tok]],
                                      buf.at[slot, k, r], sems.at[slot]).start()
            return carry
        lax.fori_loop(0, tm, body, 0, unroll=8)

    @pl.when(i == 0)
    def _():
        issue(0, 0)

    @pl.when(i + 1 < n)
    def _():
        issue(i + 1, (i + 1) % 2)

    slot = i % 2
    pltpu.make_async_copy(buf.at[slot], buf.at[slot], sems.at[slot]).wait()
    gate = gate_ref[...]
    f = (_tiles_to_rows(buf[slot, 0]) * gate[:, 0:1] + _tiles_to_rows(buf[slot, 1]) * gate[:, 1:2])
    out = x_ref[...] + g2_ref[...] * f
    if final_norm:
        out = _rms(out) * fw_ref[...]
    o_ref[...] = out


def _combine(pos, x2d, seq_len, gate, g2, fw, y, tok_offset, final_norm):
    rows, d = x2d.shape
    n_tok = gate.shape[0]
    tm = min(256, seq_len)
    tiles_per_seq = seq_len // tm
    off = tok_offset // tm
    nb = g2.shape[0]
    mod_idx = ((lambda i, p: (i // tiles_per_seq, 0, 0)) if nb > 1 else (lambda i, p: (0, 0, 0)))
    return pl.pallas_call(
        functools.partial(_combine_kernel, tok_offset=tok_offset, n_tok=n_tok,
                          final_norm=final_norm),
        grid_spec=pltpu.PrefetchScalarGridSpec(
            num_scalar_prefetch=1,
            grid=(rows // tm,),
            in_specs=[pl.BlockSpec((tm, d), lambda i, p: (i, 0)),
                      pl.BlockSpec((tm, TOP_K), lambda i, p: (i + off, 0)),
                      pl.BlockSpec((None, 1, d), mod_idx),
                      pl.BlockSpec((1, d), lambda i, p: (0, 0)),
                      pl.BlockSpec(memory_space=pl.ANY)],
            out_specs=pl.BlockSpec((tm, d), lambda i, p: (i, 0)),
            scratch_shapes=[pltpu.VMEM((2, TOP_K, tm) + y.shape[1:], F32),
                            pltpu.SemaphoreType.DMA((2,))]),
        out_shape=jax.ShapeDtypeStruct((rows, d), F32),
        compiler_params=_params("arbitrary"),
        name="moe_combine",
    )(pos, x2d, gate, g2, fw, y)


def _rope_tables(seq_len):
    rows = seq_len // GRID_W
    row = np.repeat(np.arange(rows), GRID_W).astype(np.float64)
    col = np.tile(np.arange(GRID_W), rows).astype(np.float64)
    n = HEAD_DIM // 4
    inv = (ROPE_THETA ** (-np.arange(n, dtype=np.float32) / n)).astype(np.float64)
    ang = np.concatenate([row[:, None] * inv, col[:, None] * inv], axis=-1)
    cos = np.cos(ang.astype(np.float32).astype(np.float64))
    sin = np.sin(ang.astype(np.float32).astype(np.float64))
    cos2 = np.concatenate([cos, cos], axis=-1).astype(np.float32)
    sin2 = np.concatenate([-sin, sin], axis=-1).astype(np.float32)
    return jnp.asarray(cos2), jnp.asarray(sin2)


def kernel(x, c, ctx, c_ctx, w_ada, b_ada, norm1_w, norm2_w, w_in, q_norm_w, k_norm_w,
           hy_conv_w, hy_conv_b, hy_pe_w1, hy_pe_b1, hy_freq1, hy_pe_w2, hy_pe_b2, hy_freq2,
           hy_pe_w3, hy_skip, w_att_proj, w_hy_proj, w_out, router_w, router_b,
           exp_w1, exp_w3, exp_w2, final_norm_w):
    B, S, D = x.shape
    C = ctx.shape[1]
    depth = w_ada.shape[0]
    n_lat = B * S
    n_ctx = B * C

    cos2, sin2 = _rope_tables(S)
    cos_id = jnp.ones((C, HEAD_DIM), F32)
    sin_id = jnp.zeros((C, HEAD_DIM), F32)
    dft_xf = _dft_matrices(S, False, S // 2)
    dft_cf = _dft_matrices(C, False, C // 2)
    dft_x = _dft_matrices(S // 2, True)
    dft_c = _dft_matrices(C // 2, True)

    mod_rows = 16
    cs = jnp.zeros((mod_rows, D), F32).at[:B].set(c).at[B].set(c_ctx)
    mods = _adaln(cs, w_ada, b_ada)

    r_hi = router_w.T.astype(BF16)
    r_lo = (router_w.T - r_hi.astype(F32)).astype(BF16)
    fw = final_norm_w.reshape(1, D)

    x2d = x.reshape(n_lat, D)
    c2d = ctx.reshape(n_ctx, D)
    for i in range(depth):
        last = i == depth - 1
        m_lat = [mods[i, :B, j * D:(j + 1) * D].reshape(B, 1, D) for j in range(6)]
        m_ctx = [mods[i, B:B + 1, j * D:(j + 1) * D].reshape(1, 1, D) for j in range(6)]
        wb = w_in[i].astype(BF16)
        n1 = norm1_w[i].reshape(1, D)
        n2 = norm2_w[i].reshape(1, D)
        qn = q_norm_w[i].reshape(1, HEAD_DIM)
        kn = k_norm_w[i].reshape(1, HEAD_DIM)
        wa = w_att_proj[i].astype(BF16)
        wh = w_hy_proj[i].astype(BF16)
        wo = w_out[i].astype(BF16)

        q, k, v, u, g = _inproj(x2d, S, m_lat[0], m_lat[1], n1, wb, qn, kn, cos2, sin2,
                                ("q", "k", "v", "u", "g"))
        if last:
            kc, vc = _inproj(c2d, C, m_ctx[0], m_ctx[1], n1, wb, qn, kn, cos_id, sin_id, ("k", "v"))
        else:
            qc, kc, vc, uc, gc = _inproj(c2d, C, m_ctx[0], m_ctx[1], n1, wb, qn, kn, cos_id, sin_id,
                                         ("q", "k", "v", "u", "g"))
        ya = _attention(q, [(k, v, S), (kc, vc, C)], B, S)

        filt = (hy_pe_w1[i], hy_pe_b1[i], hy_freq1[i], hy_pe_w2[i], hy_pe_b2[i], hy_freq2[i],
                hy_pe_w3[i])
        kpack = _hy_spectrum(dft_xf[0], dft_xf[1], *_hy_filter(S, *filt))
        yh = _hyena(u, B, S, dft_x, kpack, hy_conv_w[i], hy_conv_b[i], hy_skip[i])

        n_tok = n_lat if last else n_lat + n_ctx
        x2d, hx, logits = _merge(ya, yh, g, x2d, S, m_lat[2], m_lat[3], m_lat[4], n2,
                                 wa, wh, wo, r_hi, r_lo, n_tok, 0)
        if not last:
            ya_c = _attention(qc, [(kc, vc, C)], B, C)
            kpack = _hy_spectrum(dft_cf[0], dft_cf[1], *_hy_filter(C, *filt))
            yh_c = _hyena(uc, B, C, dft_c, kpack, hy_conv_w[i], hy_conv_b[i], hy_skip[i])
            c2d, hx, logits = _merge(ya_c, yh_c, gc, c2d, C, m_ctx[2], m_ctx[3], m_ctx[4], n2,
                                     wa, wh, wo, r_hi, r_lo, n_tok, n_lat, prev=(hx, logits))

        idx, gate, rank, counts = _route(logits, router_b)
        n_blocks = -(-(n_tok * TOP_K) // MOE_ROWS) + N_EXPERTS
        pad_start, pad_lo, pad_hi, tables = _layout(counts, n_blocks)
        dest = _slots(idx, rank, pad_start).reshape(TOP_K * n_tok)
        xs = _scatter_rows(dest, pad_lo, pad_hi, hx, n_blocks)
        y = _experts(tables, xs, exp_w1, exp_w3, exp_w2, i)
        gate_t = gate.T
        x2d = _combine(dest, x2d, S, gate_t, m_lat[5], fw, y, 0, last)
        if not last:
            c2d = _combine(dest, c2d, C, gate_t, m_ctx[5], fw, y, n_lat, False)
    return x2d.reshape(B, S, D)
```

```python
import functools
import math

import numpy as np
import jax
import jax.numpy as jnp
from jax import lax
from jax.experimental import pallas as pl
from jax.experimental.pallas import tpu as pltpu

F32 = jnp.float32
BF16 = jnp.bfloat16

D_MODEL = 1024
DEPTH = 2
GRID_W = 64
NORM_EPS = 1e-6
N_HEADS = 8
N_KV_HEADS = 2
HEAD_DIM = 128
ATT_WIDTH = N_HEADS * HEAD_DIM
KV_WIDTH = N_KV_HEADS * HEAD_DIM
ROPE_THETA = 10000.0
ATT_SCALE = HEAD_DIM ** -0.5
LOG2_E = math.log2(math.e)
HY_WIDTH = D_MODEL // 2
HY_ORDER = 2
HY_FILTER_HIDDEN = 64
HY_BANDS = 16
HY_PE_DIM = 1 + 2 * HY_BANDS
HY_PE_PAD = 128
HY_FAST_DECAY = 0.3
HY_SLOW_DECAY = 1.5
HY_DECAY_TARGET = 1e-2
N_EXPERTS = 64
N_GROUPS = 8
EXPERTS_PER_GROUP = N_EXPERTS // N_GROUPS
TOP_K = 2
EXPERT_DIM = 512
IN_WIDTH = ATT_WIDTH + 2 * KV_WIDTH + 3 * HY_WIDTH + 2 * D_MODEL
COL_Q = 0
COL_K = ATT_WIDTH
COL_V = ATT_WIDTH + KV_WIDTH
COL_U = ATT_WIDTH + 2 * KV_WIDTH
COL_G = COL_U + 3 * HY_WIDTH

MXU_COLS = 256
SUBLANES = 8
LANES = 128
ROW_TILE = 512
ATT_Q_TILE = 512
ATT_KEY_CHUNK = 512
DFT_ROWS = 64
HY_CHAINS = 4
MOE_ROWS = 256
VMEM_LIMIT = 56 * 1024 * 1024


def _dot(a, b):
    return jnp.dot(a, b, preferred_element_type=F32)


def _split(a):
    hi = a.astype(BF16)
    lo = (a - hi.astype(F32)).astype(BF16)
    return hi, lo


def _dot3(a, b):
    ah, al = _split(a)
    bh, bl = _split(b)
    return _dot(ah, bh) + (_dot(al, bh) + _dot(ah, bl))


def _tiles_to_rows(t):
    c = pltpu.einshape("tjl->jtl", t)
    return jnp.concatenate([c[j] for j in range(c.shape[0])], axis=1)


def _rows_to_tiles(x):
    c = jnp.stack([x[:, j * LANES:(j + 1) * LANES] for j in range(x.shape[1] // LANES)], axis=0)
    return pltpu.einshape("jtl->tjl", c)


def _rms(t):
    return t * lax.rsqrt(jnp.mean(t * t, axis=-1, keepdims=True) + NORM_EPS)


def _params(*sem):
    return pltpu.CompilerParams(dimension_semantics=sem, vmem_limit_bytes=VMEM_LIMIT)


def _adaln_kernel(c_ref, w_ref, b_ref, o_ref):
    c = c_ref[...]
    o_ref[...] = _dot3(c * jax.nn.sigmoid(c), w_ref[...]) + b_ref[...]


def _adaln(cs, w_ada, b_ada):
    depth, d, n = w_ada.shape
    rows = cs.shape[0]
    tn = 1536
    return pl.pallas_call(
        _adaln_kernel,
        grid=(depth, n // tn),
        in_specs=[pl.BlockSpec((rows, d), lambda l, j: (0, 0)),
                  pl.BlockSpec((None, d, tn), lambda l, j: (l, 0, j)),
                  pl.BlockSpec((None, 1, tn), lambda l, j: (l, 0, j))],
        out_specs=pl.BlockSpec((None, rows, tn), lambda l, j: (l, 0, j)),
        out_shape=jax.ShapeDtypeStruct((depth, rows, n), F32),
        compiler_params=_params("parallel", "parallel"),
        name="adaln",
    )(cs, w_ada, b_ada.reshape(depth, 1, n))


def _inproj_kernel(x_ref, sh_ref, sc_ref, nw_ref, w_ref, qn_ref, kn_ref, cos_ref, sin_ref,
                   *outs, sections):
    h = _rms(x_ref[...]) * nw_ref[...]
    h = h * (1.0 + sc_ref[...]) + sh_ref[...]
    hb = h.astype(BF16)
    cos = cos_ref[...]
    sin = sin_ref[...]
    o = dict(zip(sections, outs))

    def head_cols(col0, n_cols, norm_w, scale, out_ref):
        for c in range(n_cols // MXU_COLS):
            acc = _dot(hb, w_ref[:, col0 + c * MXU_COLS:col0 + (c + 1) * MXU_COLS])
            for j in range(MXU_COLS // HEAD_DIM):
                t = _rms(acc[:, j * HEAD_DIM:(j + 1) * HEAD_DIM]) * norm_w
                t = t * cos + pltpu.roll(t, HEAD_DIM // 2, 1) * sin
                lo = c * MXU_COLS + j * HEAD_DIM
                out_ref[:, lo:lo + HEAD_DIM] = (t * scale).astype(out_ref.dtype)

    if "q" in o:
        head_cols(COL_Q, ATT_WIDTH, qn_ref[...], ATT_SCALE * LOG2_E, o["q"])
    if "k" in o:
        head_cols(COL_K, KV_WIDTH, kn_ref[...], 1.0, o["k"])
    if "v" in o:
        acc = _dot(hb, w_ref[:, COL_V:COL_V + KV_WIDTH]).astype(BF16)
        ones = jnp.ones((acc.shape[0], HEAD_DIM), BF16)
        for j in range(N_KV_HEADS):
            o["v"][:, 2 * j * HEAD_DIM:(2 * j + 1) * HEAD_DIM] = acc[:, j * HEAD_DIM:(j + 1) * HEAD_DIM]
            o["v"][:, (2 * j + 1) * HEAD_DIM:(2 * j + 2) * HEAD_DIM] = ones
    if "u" in o:
        for c in range(3):
            o["u"][:, c * HY_WIDTH:(c + 1) * HY_WIDTH] = _dot(
                hb, w_ref[:, COL_U + c * HY_WIDTH:COL_U + (c + 1) * HY_WIDTH]).astype(BF16)
    if "g" in o:
        for c in range(4):
            acc = _dot(hb, w_ref[:, COL_G + c * 512:COL_G + (c + 1) * 512])
            o["g"][:, c * 512:(c + 1) * 512] = jax.nn.sigmoid(acc).astype(BF16)


_SECTION_SHAPES = {"q": (ATT_WIDTH, BF16), "k": (KV_WIDTH, BF16), "v": (2 * KV_WIDTH, BF16),
                   "u": (3 * HY_WIDTH, BF16), "g": (2 * D_MODEL, BF16)}


def _inproj(x2d, seq_len, sh, sc, nw, w_bf16, qn, kn, cos2, sin2, sections):
    rows, d = x2d.shape
    tm = min(ROW_TILE, seq_len)
    tiles_per_seq = seq_len // tm
    nb = sh.shape[0]
    mod_idx = (lambda i: (i // tiles_per_seq, 0, 0)) if nb > 1 else (lambda i: (0, 0, 0))
    const2 = lambda i: (0, 0)
    out_shape = [jax.ShapeDtypeStruct((rows, _SECTION_SHAPES[s][0]), _SECTION_SHAPES[s][1])
                 for s in sections]
    out_specs = [pl.BlockSpec((tm, _SECTION_SHAPES[s][0]), lambda i: (i, 0)) for s in sections]
    return pl.pallas_call(
        functools.partial(_inproj_kernel, sections=tuple(sections)),
        grid=(rows // tm,),
        in_specs=[pl.BlockSpec((tm, d), lambda i: (i, 0)),
                  pl.BlockSpec((None, 1, d), mod_idx),
                  pl.BlockSpec((None, 1, d), mod_idx),
                  pl.BlockSpec((1, d), const2),
                  pl.BlockSpec((d, IN_WIDTH), const2),
                  pl.BlockSpec((1, HEAD_DIM), const2),
                  pl.BlockSpec((1, HEAD_DIM), const2),
                  pl.BlockSpec((tm, HEAD_DIM), lambda i: (i % tiles_per_seq, 0)),
                  pl.BlockSpec((tm, HEAD_DIM), lambda i: (i % tiles_per_seq, 0))],
        out_specs=out_specs,
        out_shape=out_shape,
        compiler_params=_params("parallel"),
        name="inproj",
    )(x2d, sh, sc, nw, w_bf16, qn, kn, cos2, sin2)


def _attn_kernel(q_ref, *refs, n_kv_sets):
    kv = refs[:2 * n_kv_sets]
    o_ref = refs[2 * n_kv_sets]
    nt = (((1,), (1,)), ((), ()))
    for g in range(N_HEADS // N_KV_HEADS):
        q = q_ref[:, g * HEAD_DIM:(g + 1) * HEAD_DIM]
        m = None
        acc = None
        for s_idx in range(n_kv_sets):
            k_ref, v_ref = kv[2 * s_idx], kv[2 * s_idx + 1]
            lk = k_ref.shape[0]
            for c0 in range(0, lk, ATT_KEY_CHUNK):
                c1 = min(c0 + ATT_KEY_CHUNK, lk)
                s = lax.dot_general(q, k_ref[c0:c1, :], nt, preferred_element_type=F32)
                mc = jnp.max(s, axis=1, keepdims=True)
                if m is None:
                    m = mc
                    acc = _dot(jnp.exp2(s - m).astype(BF16), v_ref[c0:c1, :])
                else:
                    m_new = jnp.maximum(m, mc)
                    acc = jnp.exp2(m - m_new) * acc + _dot(jnp.exp2(s - m_new).astype(BF16),
                                                           v_ref[c0:c1, :])
                    m = m_new
        o_ref[:, g * HEAD_DIM:(g + 1) * HEAD_DIM] = (
            acc[:, :HEAD_DIM] / acc[:, HEAD_DIM:]).astype(o_ref.dtype)


def _attention(q, kv_sets, batch, seq_len):
    tq = min(ATT_Q_TILE, seq_len)
    nq = seq_len // tq
    grp = (N_HEADS // N_KV_HEADS) * HEAD_DIM
    in_specs = [pl.BlockSpec((tq, grp), lambda b, j, i: (b * nq + i, j))]
    args = [q]
    for k, v, lk in kv_sets:
        in_specs += [pl.BlockSpec((lk, HEAD_DIM), lambda b, j, i: (b, j)),
                     pl.BlockSpec((lk, 2 * HEAD_DIM), lambda b, j, i: (b, j))]
        args += [k, v]
    return pl.pallas_call(
        functools.partial(_attn_kernel, n_kv_sets=len(kv_sets)),
        grid=(batch, N_KV_HEADS, nq),
        in_specs=in_specs,
        out_specs=pl.BlockSpec((tq, grp), lambda b, j, i: (b * nq + i, j)),
        out_shape=jax.ShapeDtypeStruct(q.shape, BF16),
        compiler_params=_params("parallel", "parallel", "arbitrary"),
        name="attention",
    )(*args)


def _hy_filter_kernel(z_ref, t_ref, w1_ref, b1_ref, f1_ref, w2_ref, b2_ref, f2_ref, w3_ref,
                      dl_ref, sp_ref, sm_ref, spa_ref, sma_ref, mid_ref):
    i = pl.program_id(0)
    tl = z_ref.shape[0]
    h = jnp.sin(f1_ref[...] * (_dot3(z_ref[...], w1_ref[...]) + b1_ref[...]))
    h = jnp.sin(f2_ref[...] * (_dot3(h, w2_ref[...]) + b2_ref[...]))
    h = _dot3(h, w3_ref[...])
    window = jnp.exp(-t_ref[...] * dl_ref[...])
    row = i * tl + lax.broadcasted_iota(jnp.int32, (tl, HY_WIDTH), 0)
    alt = (1 - 2 * (row & 1)).astype(F32)
    quarter = row & 3
    cos4 = jnp.where(quarter == 0, 1.0, jnp.where(quarter == 2, -1.0, 0.0))
    sin4 = jnp.where(quarter == 1, 1.0, jnp.where(quarter == 3, -1.0, 0.0))

    @pl.when(i == 0)
    def _():
        mid_ref[...] = jnp.zeros_like(mid_ref)

    for o in range(HY_ORDER):
        cols = slice(o * HY_WIDTH, (o + 1) * HY_WIDTH)
        hf = h[:, (2 * o) * HY_WIDTH:(2 * o + 1) * HY_WIDTH] * window
        hb = h[:, (2 * o + 1) * HY_WIDTH:(2 * o + 2) * HY_WIDTH] * window
        hb = jnp.where(row == 0, 0.0, hb)
        plus = hf + hb
        minus = hf - hb
        sp_ref[:, cols] = plus.astype(BF16)
        sm_ref[:, cols] = minus.astype(BF16)
        spa_ref[:, cols] = (alt * plus).astype(BF16)
        sma_ref[:, cols] = (alt * minus).astype(BF16)
        mid_ref[0:1, cols] += jnp.sum(cos4 * plus, axis=0, keepdims=True)
        mid_ref[1:2, cols] += jnp.sum(sin4 * minus, axis=0, keepdims=True)


def _hy_filter(seq_len, pe_w1, pe_b1, freq1, pe_w2, pe_b2, freq2, pe_w3):
    t01 = np.linspace(0.0, 1.0, seq_len)[:, None]
    pos = np.arange(seq_len, dtype=np.float64)[:, None]
    bands = np.linspace(1e-4, HY_BANDS - 1, HY_BANDS)[None, :]
    f = 2.0 * math.pi * pos * bands / seq_len
    z = np.zeros((seq_len, HY_PE_PAD), np.float32)
    z[:, :HY_PE_DIM] = np.concatenate([t01, np.cos(f), -np.sin(f)], axis=-1)
    max_decay = math.log(HY_DECAY_TARGET) / HY_FAST_DECAY
    min_decay = math.log(HY_DECAY_TARGET) / HY_SLOW_DECAY
    deltas = np.abs(np.linspace(min_decay, max_decay, HY_WIDTH))[None, :].astype(np.float32)
    w1p = jnp.zeros((HY_PE_PAD, HY_FILTER_HIDDEN), F32).at[:HY_PE_DIM].set(pe_w1)
    tl = min(512, seq_len)
    hid = HY_FILTER_HIDDEN
    n_out = HY_ORDER * HY_WIDTH
    c2 = lambda i: (0, 0)
    return pl.pallas_call(
        _hy_filter_kernel,
        grid=(seq_len // tl,),
        in_specs=[pl.BlockSpec((tl, HY_PE_PAD), lambda i: (i, 0)),
                  pl.BlockSpec((tl, 1), lambda i: (i, 0)),
                  pl.BlockSpec((HY_PE_PAD, hid), c2), pl.BlockSpec((1, hid), c2),
                  pl.BlockSpec((1, hid), c2), pl.BlockSpec((hid, hid), c2),
                  pl.BlockSpec((1, hid), c2), pl.BlockSpec((1, hid), c2),
                  pl.BlockSpec((hid, 2 * n_out), c2), pl.BlockSpec((1, HY_WIDTH), c2)],
        out_specs=[pl.BlockSpec((tl, n_out), lambda i: (i, 0))] * 4
                  + [pl.BlockSpec((SUBLANES, n_out), c2)],
        out_shape=[jax.ShapeDtypeStruct((seq_len, n_out), BF16)] * 4
                  + [jax.ShapeDtypeStruct((SUBLANES, n_out), F32)],
        compiler_params=_params("arbitrary"),
        name="hy_filter",
    )(jnp.asarray(z), jnp.asarray(t01.astype(np.float32)), w1p, pe_b1.reshape(1, hid),
      freq1.reshape(1, hid), pe_w2, pe_b2.reshape(1, hid), freq2.reshape(1, hid), pe_w3,
      jnp.asarray(deltas))


def _dft_kernel(ca_ref, sa_ref, cb_ref, sb_ref, alt_ref, wc_ref, cm_ref, sm_ref, *inverse_refs):
    i = pl.program_id(0)
    ca = ca_ref[...]
    sa = sa_ref[...]
    cb = cb_ref[...]
    sb = sb_ref[...]
    c = ca * cb - sa * sb
    s = sa * cb + ca * sb
    rows = lax.broadcasted_iota(jnp.int32, c.shape, 0)
    cols = lax.broadcasted_iota(jnp.int32, c.shape, 1)
    wc = wc_ref[...]
    cm_ref[...] = c.astype(BF16)
    sm_ref[...] = jnp.where((rows == 0) & (i == 0), alt_ref[...], s).astype(BF16)
    if inverse_refs:
        ci_ref, si_ref = inverse_refs
        alt_t = (1 - 2 * (rows & 1)).astype(F32)
        ci_ref[...] = (c * wc).astype(BF16)
        si_ref[...] = (jnp.where(cols == 0, alt_t, s) * wc).astype(BF16)


def _dft_matrices(seq_len, inverse, n_rows=None):
    n = 2 * seq_len
    n_rows = seq_len if n_rows is None else n_rows
    idx = np.arange(seq_len, dtype=np.int64)[None, :]
    r1 = np.arange(n_rows // DFT_ROWS, dtype=np.int64)[:, None] * DFT_ROWS
    r0 = np.arange(DFT_ROWS, dtype=np.int64)[:, None]
    ang_a = ((r1 * idx) % n).astype(np.float64) * (2.0 * math.pi / n)
    ang_b = ((r0 * idx) % n).astype(np.float64) * (2.0 * math.pi / n)
    tab = lambda a: jnp.asarray(a.astype(np.float32))
    ca = tab(np.cos(ang_a)).reshape(-1, 1, seq_len)
    sa = tab(np.sin(ang_a)).reshape(-1, 1, seq_len)
    alt = tab(1.0 - 2.0 * (idx % 2))
    wc = tab(np.where(idx == 0, 1.0, 2.0) / n)
    row_blk = pl.BlockSpec((None, 1, seq_len), lambda i: (i, 0, 0))
    full = lambda r: pl.BlockSpec((r, seq_len), lambda i: (0, 0))
    out_blk = pl.BlockSpec((DFT_ROWS, seq_len), lambda i: (i, 0))
    n_out = 4 if inverse else 2
    mats = pl.pallas_call(
        _dft_kernel,
        grid=(n_rows // DFT_ROWS,),
        in_specs=[row_blk, row_blk, full(DFT_ROWS), full(DFT_ROWS), full(1), full(1)],
        out_specs=[out_blk] * n_out,
        out_shape=[jax.ShapeDtypeStruct((n_rows, seq_len), BF16)] * n_out,
        compiler_params=_params("parallel"),
        name="dft_matrices",
    )(ca, sa, tab(np.cos(ang_b)), tab(np.sin(ang_b)), alt, wc)
    if not inverse:
        return tuple(mats)
    tw = np.arange(seq_len, dtype=np.float64)[:, None] * (2.0 * math.pi / (2 * n))
    rep = lambda a: tab(np.broadcast_to(a, (seq_len, LANES)))
    return tuple(mats) + (rep(np.cos(tw)), rep(np.sin(tw)))


def _hy_spec_kernel(cm_ref, sm_ref, sp_ref, smn_ref, spa_ref, sma_ref, mid_ref,
                    klr_ref, kli_ref, khr_ref, khi_ref):
    i = pl.program_id(1)
    cm = cm_ref[...]
    sm = sm_ref[...]
    klr_ref[...] = _dot(cm, sp_ref[...])
    khr_ref[...] = _dot(cm, spa_ref[...])
    lo_q = _dot(sm, smn_ref[...])
    hi_q = _dot(sm, sma_ref[...])
    rows = lax.broadcasted_iota(jnp.int32, lo_q.shape, 0)
    edge = (rows == 0) & (i == 0)
    kli_ref[...] = jnp.where(edge, mid_ref[0:1, :], -lo_q)
    khi_ref[...] = jnp.where(edge, -mid_ref[1:2, :], hi_q)


def _hy_spectrum(cm, sm, splus, sminus, splus_alt, sminus_alt, mid):
    half, seq_len = cm.shape
    tf = min(256, half)
    mat = pl.BlockSpec((tf, seq_len), lambda o, i: (i, 0))
    sig = pl.BlockSpec((seq_len, HY_WIDTH), lambda o, i: (0, o))
    out = pl.BlockSpec((tf, HY_WIDTH), lambda o, i: (i, o))
    return pl.pallas_call(
        _hy_spec_kernel,
        grid=(HY_ORDER, half // tf),
        in_specs=[mat, mat, sig, sig, sig, sig,
                  pl.BlockSpec((SUBLANES, HY_WIDTH), lambda o, i: (0, o))],
        out_specs=[out] * 4,
        out_shape=[jax.ShapeDtypeStruct((half, HY_ORDER * HY_WIDTH), F32)] * 4,
        compiler_params=_params("parallel", "arbitrary"),
        name="hy_spectrum",
    )(cm, sm, splus, sminus, splus_alt, sminus_alt, mid)


def _hy_sconv_kernel(u_ref, w_ref, b_ref, ue_ref, uo_ref):
    half = u_ref.shape[0] // 2
    words = pltpu.bitcast(u_ref[...], jnp.uint32)
    xe = lax.bitcast_convert_type(words << 16, F32)
    xo = lax.bitcast_convert_type(words & jnp.uint32(0xFFFF0000), F32)
    rows = lax.broadcasted_iota(jnp.int32, xe.shape, 0)
    xo_prev = jnp.where(rows == 0, 0.0, pltpu.roll(xo, 1, 0))
    xe_next = jnp.where(rows == half - 1, 0.0, pltpu.roll(xe, half - 1, 0))
    w0, w1, w2 = w_ref[0:1, :], w_ref[1:2, :], w_ref[2:3, :]
    ye = xo_prev * w0 + xe * w1 + xo * w2 + b_ref[...]
    yo = xe * w0 + xo * w1 + xe_next * w2 + b_ref[...]
    ue_ref[...] = ye.astype(ue_ref.dtype)
    uo_ref[...] = yo.astype(uo_ref.dtype)


def _hy_sconv(u, batch, seq_len, conv_w, conv_b):
    ch = u.shape[1]
    tc = 256
    half = seq_len // 2
    u3 = u.reshape(batch, seq_len, ch)
    plane = pl.BlockSpec((None, half, tc), lambda b, j: (b, 0, j))
    ue, uo = pl.pallas_call(
        _hy_sconv_kernel,
        grid=(batch, ch // tc),
        in_specs=[pl.BlockSpec((None, seq_len, tc), lambda b, j: (b, 0, j)),
                  pl.BlockSpec((3, tc), lambda b, j: (0, j)),
                  pl.BlockSpec((1, tc), lambda b, j: (0, j))],
        out_specs=[plane, plane],
        out_shape=[jax.ShapeDtypeStruct((batch, half, ch), BF16)] * 2,
        compiler_params=_params("parallel", "parallel"),
        name="hy_sconv",
    )(u3, conv_w, conv_b.reshape(1, ch))
    return ue, uo


def _hy_fwd_kernel(cm_ref, sm_ref, tc_ref, ts_ref, ve_ref, vo_ref, klr_ref, kli_ref, khr_ref,
                   khi_ref, zp_ref, zq_ref):
    i = pl.program_id(0)
    w = ve_ref.shape[1]
    wc = w // HY_CHAINS
    reps = wc // LANES
    tc = jnp.concatenate([tc_ref[...]] * reps, axis=1)
    ts = jnp.concatenate([ts_ref[...]] * reps, axis=1)
    rows = lax.broadcasted_iota(jnp.int32, (cm_ref.shape[0], wc), 0)
    edge = (rows == 0) & (i == 0)
    cm = cm_ref[...]
    sm = sm_ref[...]
    for c in range(HY_CHAINS):
        lo, hi = c * wc, (c + 1) * wc
        v = jnp.concatenate([ve_ref[:, lo:hi], vo_ref[:, lo:hi]], axis=1)
        p = _dot(cm, v)
        q = _dot(sm, v)
        pe, po, qe, qo = p[:, :wc], p[:, wc:], q[:, :wc], q[:, wc:]

        a = tc * po - ts * qo
        b = tc * qo + ts * po
        xlr = pe + a
        xhr = pe - a
        xli = jnp.where(edge, qe, -(qe + b))
        xhi = jnp.where(edge, -qo, qe - b)

        klr, kli, khr, khi = klr_ref[:, lo:hi], kli_ref[:, lo:hi], khr_ref[:, lo:hi], khi_ref[:, lo:hi]
        ylr = xlr * klr - jnp.where(edge, 0.0, xli * kli)
        yhr = xhr * khr - jnp.where(edge, 0.0, xhi * khi)
        yli = jnp.where(edge, xli * kli - xhi * khi, xlr * kli + xli * klr)
        yhi = jnp.where(edge, xli * khi + xhi * kli, xhr * khi + xhi * khr)

        dr = ylr - yhr
        di = yli + yhi
        zp_ref[:, lo:hi] = (0.5 * (ylr + yhr)).astype(BF16)
        zp_ref[:, w + lo:w + hi] = (0.5 * (tc * dr - ts * di)).astype(BF16)
        zq_ref[:, lo:hi] = jnp.where(edge, yli, -0.5 * (yli - yhi)).astype(BF16)
        zq_ref[:, w + lo:w + hi] = jnp.where(edge, -yhi, -0.5 * (tc * di + ts * dr)).astype(BF16)


def _hy_forward(dfth, v_planes, v_blk, kpack, order):
    cm, sm, _, _, tc, ts = dfth
    batch, half, _ = v_planes[0].shape
    w = HY_WIDTH
    tf = min(512, half)
    mat = pl.BlockSpec((tf, half), lambda i, b: (i, 0))
    tw = pl.BlockSpec((tf, LANES), lambda i, b: (i, 0))
    spec = pl.BlockSpec((tf, w), lambda i, b: (i, order))
    out = pl.BlockSpec((None, tf, 2 * w), lambda i, b: (b, i, 0))
    return pl.pallas_call(
        _hy_fwd_kernel,
        grid=(half // tf, batch),
        in_specs=[mat, mat, tw, tw,
                  pl.BlockSpec((None, half, w), lambda i, b: (b, 0, v_blk)),
                  pl.BlockSpec((None, half, w), lambda i, b: (b, 0, v_blk)),
                  spec, spec, spec, spec],
        out_specs=[out, out],
        out_shape=[jax.ShapeDtypeStruct((batch, half, 2 * w), BF16)] * 2,
        compiler_params=_params("parallel", "arbitrary"),
        name="hy_forward",
    )(cm, sm, tc, ts, v_planes[0], v_planes[1], *kpack)


def _hy_inv_kernel(ci_ref, si_ref, zp_ref, zq_ref, ae_ref, ao_ref, ce_ref, co_ref, skip_ref, *outs,
                   interleave):
    w = ae_ref.shape[1]
    tt = ae_ref.shape[0]
    y = _dot(ci_ref[...], zp_ref[...]) + _dot(si_ref[...], zq_ref[...])
    skip = skip_ref[...]
    ze = ae_ref[...].astype(F32) * (y[:, :w] + skip * ce_ref[...].astype(F32))
    zo = ao_ref[...].astype(F32) * (y[:, w:] + skip * co_ref[...].astype(F32))
    if interleave:
        (o_ref,) = outs
        bits = lambda t: lax.bitcast_convert_type(t.astype(BF16).astype(F32), jnp.uint32)
        words = (bits(ze) >> 16) | (bits(zo) & jnp.uint32(0xFFFF0000))
        o_ref[...] = pltpu.bitcast(words, BF16)
    else:
        for k in range(len(outs) // 2):
            outs[2 * k][...] = ze.astype(outs[2 * k].dtype)
            outs[2 * k + 1][...] = zo.astype(outs[2 * k + 1].dtype)


def _hy_inverse(dfth, zp, zq, a_planes, a_blk, c_planes, c_blk, skip, out_dtypes, interleave):
    _, _, ci, si, _, _ = dfth
    batch, half, w2 = zp.shape
    w = w2 // 2
    tt = min(512, half)
    mat = pl.BlockSpec((tt, half), lambda i, b: (i, 0))
    sig = pl.BlockSpec((None, half, w2), lambda i, b: (b, 0, 0))
    a_spec = pl.BlockSpec((None, tt, w), lambda i, b: (b, i, a_blk))
    c_spec = pl.BlockSpec((None, tt, w), lambda i, b: (b, i, c_blk))
    plane = pl.BlockSpec((None, tt, w), lambda i, b: (b, i, 0))
    if interleave:
        (dt,) = out_dtypes
        out_specs = [pl.BlockSpec((None, 2 * tt, w), lambda i, b: (b, i, 0))]
        out_shape = [jax.ShapeDtypeStruct((batch, 2 * half, w), dt)]
    else:
        out_specs = [plane] * (2 * len(out_dtypes))
        out_shape = [jax.ShapeDtypeStruct((batch, half, w), dt) for dt in out_dtypes for _ in (0, 1)]
    outs = pl.pallas_call(
        functools.partial(_hy_inv_kernel, interleave=interleave),
        grid=(half // tt, batch),
        in_specs=[mat, mat, sig, sig, a_spec, a_spec, c_spec, c_spec,
                  pl.BlockSpec((1, w), lambda i, b: (0, 0))],
        out_specs=out_specs,
        out_shape=out_shape,
        compiler_params=_params("parallel", "arbitrary"),
        name="hy_inverse",
    )(ci, si, zp, zq, a_planes[0], a_planes[1], c_planes[0], c_planes[1], skip)
    if interleave:
        return outs[0]
    return [(outs[2 * k], outs[2 * k + 1]) for k in range(len(out_dtypes))]


def _hyena(u, batch, seq_len, dfth, kpack, conv_w, conv_b, skip):
    uc = _hy_sconv(u, batch, seq_len, conv_w, conv_b)
    zp, zq = _hy_forward(dfth, uc, 2, kpack, 0)
    z, zb = _hy_inverse(dfth, zp, zq, uc, 0, uc, 2, skip[0:1], (F32, BF16), False)
    zp, zq = _hy_forward(dfth, zb, 0, kpack, 1)
    y = _hy_inverse(dfth, zp, zq, uc, 1, z, 0, skip[1:2], (BF16,), True)
    return y.reshape(batch * seq_len, HY_WIDTH)


def _merge_kernel(ya_ref, yh_ref, g_ref, x_ref, g1_ref, sh_ref, sc_ref, nw_ref,
                  wa_ref, wh_ref, wo_ref, rh_ref, rl_ref, *rest):
    xo_ref, hx_ref, lg_ref = rest[-3:]
    a = _dot(ya_ref[...], wa_ref[...])
    h = _dot(yh_ref[...], wh_ref[...])
    m = g_ref[:, :D_MODEL].astype(F32) * a + g_ref[:, D_MODEL:].astype(F32) * h
    xn = x_ref[...] + g1_ref[...] * _dot(m.astype(BF16), wo_ref[...])
    xo_ref[...] = xn
    hx = _rms(xn) * nw_ref[...]
    hx = hx * (1.0 + sc_ref[...]) + sh_ref[...]
    hx_ref[...] = _rows_to_tiles(hx)
    hh, hl = _split(hx)
    nt = (((1,), (1,)), ((), ()))
    dn = lambda a, b: lax.dot_general(a, b, nt, preferred_element_type=F32)
    lg_ref[...] = dn(rh_ref[...], hh) + (dn(rh_ref[...], hl) + dn(rl_ref[...], hh))


def _merge(ya, yh, g, x2d, seq_len, g1, sh2, sc2, nw, wa, wh, wo, r_hi, r_lo,
           total_rows, row_offset, prev=None):
    rows, d = x2d.shape
    tm = min(ROW_TILE, seq_len)
    tiles_per_seq = seq_len // tm
    off = row_offset // tm
    nb = g1.shape[0]
    mod_idx = (lambda i: (i // tiles_per_seq, 0, 0)) if nb > 1 else (lambda i: (0, 0, 0))
    c2 = lambda i: (0, 0)
    row = lambda w: pl.BlockSpec((tm, w), lambda i: (i, 0))
    in_specs = [row(ATT_WIDTH), row(HY_WIDTH), row(2 * d), row(d),
                pl.BlockSpec((None, 1, d), mod_idx), pl.BlockSpec((None, 1, d), mod_idx),
                pl.BlockSpec((None, 1, d), mod_idx), pl.BlockSpec((1, d), c2),
                pl.BlockSpec((ATT_WIDTH, d), c2), pl.BlockSpec((HY_WIDTH, d), c2),
                pl.BlockSpec((d, d), c2), pl.BlockSpec((N_EXPERTS, d), c2),
                pl.BlockSpec((N_EXPERTS, d), c2)]
    args = [ya, yh, g, x2d, g1, sh2, sc2, nw, wa, wh, wo, r_hi, r_lo]
    aliases = {}
    if prev is not None:
        in_specs += [pl.BlockSpec(memory_space=pl.ANY)] * 2
        aliases = {len(args): 1, len(args) + 1: 2}
        args += list(prev)
    return pl.pallas_call(
        _merge_kernel,
        grid=(rows // tm,),
        in_specs=in_specs,
        out_specs=[row(d), pl.BlockSpec((tm, d // LANES, LANES), lambda i: (i + off, 0, 0)),
                   pl.BlockSpec((N_EXPERTS, tm), lambda i: (0, i + off))],
        out_shape=[jax.ShapeDtypeStruct((rows, d), F32),
                   jax.ShapeDtypeStruct((total_rows, d // LANES, LANES), F32),
                   jax.ShapeDtypeStruct((N_EXPERTS, total_rows), F32)],
        input_output_aliases=aliases,
        compiler_params=_params("arbitrary"),
        name="merge",
    )(*args)


def _route_kernel(lg_ref, bias_ref, idx_ref, gate_ref, rank_ref, cnt_ref, run_ref):
    step = pl.program_id(0)
    tn = lg_ref.shape[2]
    shape = (N_GROUPS, EXPERTS_PER_GROUP, tn)
    s = jax.nn.sigmoid(lg_ref[...])
    b = s + bias_ref[...]
    mem = lax.broadcasted_iota(jnp.int32, shape, 1).astype(F32)
    grp = lax.broadcasted_iota(jnp.int32, (N_GROUPS, 1, tn), 0).astype(F32)
    big = float(N_EXPERTS)

    m1 = jnp.max(b, axis=1, keepdims=True)
    i1 = jnp.min(jnp.where(b == m1, mem, big), axis=1, keepdims=True)
    b2 = jnp.where(mem == i1, -jnp.inf, b)
    m2 = jnp.max(b2, axis=1, keepdims=True)
    i2 = jnp.min(jnp.where(b2 == m2, mem, big), axis=1, keepdims=True)
    gs = m1 + m2
    gmax = jnp.max(gs, axis=0, keepdims=True)
    gsel = jnp.min(jnp.where(gs == gmax, grp, big), axis=0, keepdims=True)
    selg = grp == gsel

    @pl.when(step == 0)
    def _():
        run_ref[...] = jnp.zeros_like(run_ref)

    tri = (lax.broadcasted_iota(jnp.int32, (tn, tn), 0)
           < lax.broadcasted_iota(jnp.int32, (tn, tn), 1)).astype(BF16)
    run = run_ref[...]
    ws = []
    for k, ik in enumerate((i1, i2)):
        hit = selg & (mem == ik)
        ws.append(jnp.sum(jnp.where(hit, s, 0.0), axis=(0, 1), keepdims=True))
        e_loc = jnp.sum(jnp.where(selg, ik, 0.0), axis=0, keepdims=True)
        idx_ref[k:k + 1, :] = (gsel * EXPERTS_PER_GROUP + e_loc).reshape(1, tn).astype(jnp.int32)
        oh = jnp.where(hit, 1.0, 0.0).reshape(N_EXPERTS, tn)
        before = run + _dot(oh.astype(BF16), tri)
        rank_ref[k:k + 1, :] = jnp.sum(oh * before, axis=0, keepdims=True).astype(jnp.int32)
        run = run + jnp.sum(oh, axis=1, keepdims=True)
    run_ref[...] = run
    cnt_ref[...] = run
    tot = ws[0] + ws[1]
    for k in range(TOP_K):
        gate_ref[k:k + 1, :] = (ws[k] / tot).reshape(1, tn)


def _route(logits_t, router_b):
    n_tok = logits_t.shape[1]
    tn = 512
    lg3 = logits_t.reshape(N_GROUPS, EXPERTS_PER_GROUP, n_tok)
    row2 = pl.BlockSpec((TOP_K, tn), lambda i: (0, i))
    return pl.pallas_call(
        _route_kernel,
        grid=(n_tok // tn,),
        in_specs=[pl.BlockSpec((N_GROUPS, EXPERTS_PER_GROUP, tn), lambda i: (0, 0, i)),
                  pl.BlockSpec((N_GROUPS, EXPERTS_PER_GROUP, 1), lambda i: (0, 0, 0))],
        out_specs=[row2, row2, row2, pl.BlockSpec((N_EXPERTS, 1), lambda i: (0, 0))],
        out_shape=[jax.ShapeDtypeStruct((TOP_K, n_tok), jnp.int32),
                   jax.ShapeDtypeStruct((TOP_K, n_tok), F32),
                   jax.ShapeDtypeStruct((TOP_K, n_tok), jnp.int32),
                   jax.ShapeDtypeStruct((N_EXPERTS, 1), F32)],
        scratch_shapes=[pltpu.VMEM((N_EXPERTS, 1), F32)],
        compiler_params=_params("arbitrary"),
        name="moe_route",
    )(lg3, router_b.astype(F32).reshape(N_GROUPS, EXPERTS_PER_GROUP, 1))


def _slot_kernel(idx_ref, rank_ref, start_ref, dest_ref):
    tn = idx_ref.shape[1]
    e = lax.broadcasted_iota(jnp.int32, (N_EXPERTS, tn), 0)
    for k in range(TOP_K):
        base = jnp.sum(jnp.where(e == idx_ref[k:k + 1, :], start_ref[...], 0.0), axis=0,
                       keepdims=True)
        dest_ref[k:k + 1, :] = base.astype(jnp.int32) + rank_ref[k:k + 1, :]


def _slots(idx, rank, pad_start):
    n_tok = idx.shape[1]
    tn = 512
    row2 = pl.BlockSpec((TOP_K, tn), lambda i: (0, i))
    return pl.pallas_call(
        _slot_kernel,
        grid=(n_tok // tn,),
        in_specs=[row2, row2, pl.BlockSpec((N_EXPERTS, 1), lambda i: (0, 0))],
        out_specs=row2,
        out_shape=jax.ShapeDtypeStruct((TOP_K, n_tok), jnp.int32),
        compiler_params=_params("parallel"),
        name="moe_slots",
    )(idx, rank, pad_start.astype(F32).reshape(N_EXPERTS, 1))


def _layout(counts, n_blocks):
    counts = counts.reshape(N_EXPERTS).astype(jnp.int32)
    padded = (counts + MOE_ROWS - 1) // MOE_ROWS * MOE_ROWS
    pad_end = jnp.cumsum(padded)
    pad_start = pad_end - padded
    blk_row = jnp.arange(n_blocks, dtype=jnp.int32)[:, None] * MOE_ROWS
    block_e = jnp.minimum(jnp.sum((pad_end[None, :] <= blk_row).astype(jnp.int32), axis=1),
                          N_EXPERTS - 1)
    prev_e = jnp.concatenate([jnp.full((1,), -1, jnp.int32), block_e[:-1]])
    block_new = (block_e != prev_e).astype(jnp.int32)
    block_slot = (jnp.cumsum(block_new) - 1) & 1
    e_ids = jnp.arange(N_EXPERTS, dtype=jnp.int32)
    later = (e_ids[None, :] > e_ids[:, None]) & (counts[None, :] > 0)
    next_e = jnp.min(jnp.where(later, e_ids[None, :], N_EXPERTS), axis=1)
    next_e = jnp.where(next_e == N_EXPERTS, -1, next_e)
    block_next = jnp.sum(jnp.where(block_e[:, None] == e_ids[None, :], next_e[None, :], 0), axis=1)
    n_used = (pad_end[-1] // MOE_ROWS).astype(jnp.int32).reshape(1)
    tables = (block_e, block_new, block_slot.astype(jnp.int32), block_next.astype(jnp.int32), n_used)
    return pad_start, pad_start + counts, pad_end, tables


_PAD_CHUNKS = tuple(2 ** p for p in range(int(math.log2(MOE_ROWS)) - 1, -1, -1))


def _scatter_kernel(dest_ref, lo_ref, hi_ref, x_ref, o_hbm, zeros, sem, zsem, *, n_tok):
    i = pl.program_id(0)
    tm = x_ref.shape[0]

    def pad_copies(act):
        def per_expert(e, carry):
            off = lo_ref[e]
            n = hi_ref[e] - off
            for c in _PAD_CHUNKS:
                @pl.when((n & c) != 0)
                def _():
                    act(pltpu.make_async_copy(zeros.at[pl.ds(0, c)], o_hbm.at[pl.ds(off, c)], zsem))
                off = off + (n & c)
            return carry
        lax.fori_loop(0, N_EXPERTS, per_expert, 0)

    @pl.when(i == 0)
    def _():
        zeros[...] = jnp.zeros_like(zeros)
        pad_copies(lambda cp: cp.start())

    def issue(r, carry):
        for k in range(TOP_K):
            pltpu.make_async_copy(x_ref.at[r], o_hbm.at[dest_ref[k * n_tok + i * tm + r]], sem).start()
        return carry
    lax.fori_loop(0, tm, issue, 0, unroll=8)
    for k in range(TOP_K):
        pltpu.make_async_copy(x_ref, o_hbm.at[pl.ds(0, tm)], sem).wait()

    @pl.when(i == 0)
    def _():
        pad_copies(lambda cp: cp.wait())


def _scatter_rows(dest_flat, pad_lo, pad_hi, h, n_blocks):
    n_tok = h.shape[0]
    tile = h.shape[1:]
    tm = 256
    return pl.pallas_call(
        functools.partial(_scatter_kernel, n_tok=n_tok),
        grid_spec=pltpu.PrefetchScalarGridSpec(
            num_scalar_prefetch=3,
            grid=(n_tok // tm,),
            in_specs=[pl.BlockSpec((tm,) + tile, lambda i, *_: (i, 0, 0))],
            out_specs=pl.BlockSpec(memory_space=pl.ANY),
            scratch_shapes=[pltpu.VMEM((MOE_ROWS // 2,) + tile, h.dtype),
                            pltpu.SemaphoreType.DMA(()), pltpu.SemaphoreType.DMA(())]),
        out_shape=jax.ShapeDtypeStruct((n_blocks * MOE_ROWS,) + tile, h.dtype),
        compiler_params=pltpu.CompilerParams(dimension_semantics=("arbitrary",),
                                             vmem_limit_bytes=VMEM_LIMIT, has_side_effects=True),
        name="moe_scatter",
    )(dest_flat, pad_lo, pad_hi, h)


def _expert_kernel(be_ref, new_ref, slot_ref, next_ref, nused_ref, x_ref, w1_hbm, w3_hbm, w2_hbm,
                   y_ref, w1f, w3f, w2f, w1b, w3b, w2b, sems, *, layer):
    i = pl.program_id(0)
    hbm = (w1_hbm, w3_hbm, w2_hbm)
    f32_bufs = (w1f, w3f, w2f)
    bf16_bufs = (w1b, w3b, w2b)

    def copies(e, s):
        return [pltpu.make_async_copy(hbm[k].at[layer, e], f32_bufs[k].at[s], sems.at[s, k])
                for k in range(3)]

    @pl.when(i < nused_ref[0])
    def _():
        @pl.when(new_ref[i] == 1)
        def _():
            s = slot_ref[i]

            @pl.when(i == 0)
            def _():
                for cp in copies(be_ref[0], 0):
                    cp.start()

            for cp in copies(be_ref[i], s):
                cp.wait()

            @pl.when(next_ref[i] >= 0)
            def _():
                for cp in copies(next_ref[i], 1 - s):
                    cp.start()

            for k in range(3):
                bf16_bufs[k][...] = f32_bufs[k][s].astype(BF16)

        x = _tiles_to_rows(x_ref[...]).astype(BF16)
        a = _dot(x, w1b[...])
        b = _dot(x, w3b[...])
        h = (a * jax.nn.sigmoid(a)) * b
        y_ref[...] = _rows_to_tiles(_dot(h.astype(BF16), w2b[...]))


def _experts(tables, xs, w1, w3, w2, layer):
    n_slots = xs.shape[0]
    tile = xs.shape[1:]
    d = tile[0] * tile[1]
    n_blocks = n_slots // MOE_ROWS
    e_dim = w1.shape[3]
    blk = lambda i, be, bn, bs, bx, nu: (jnp.minimum(i, nu[0] - 1), 0, 0)
    hbm = pl.BlockSpec(memory_space=pl.ANY)
    return pl.pallas_call(
        functools.partial(_expert_kernel, layer=layer),
        grid_spec=pltpu.PrefetchScalarGridSpec(
            num_scalar_prefetch=5,
            grid=(n_blocks,),
            in_specs=[pl.BlockSpec((MOE_ROWS,) + tile, blk), hbm, hbm, hbm],
            out_specs=pl.BlockSpec((MOE_ROWS,) + tile, blk),
            scratch_shapes=[pltpu.VMEM((2, d, e_dim), F32), pltpu.VMEM((2, d, e_dim), F32),
                            pltpu.VMEM((2, e_dim, d), F32),
                            pltpu.VMEM((d, e_dim), BF16), pltpu.VMEM((d, e_dim), BF16),
                            pltpu.VMEM((e_dim, d), BF16),
                            pltpu.SemaphoreType.DMA((2, 3))]),
        out_shape=jax.ShapeDtypeStruct(xs.shape, F32),
        compiler_params=_params("arbitrary"),
        name="moe_experts",
    )(*tables, xs, w1, w3, w2)


def _combine_kernel(pos_ref, x_ref, gate_ref, g2_ref, fw_ref, y_hbm, o_ref, buf, sems, *,
                    tok_offset, n_tok, final_norm):
    i = pl.program_id(0)
    n = pl.num_programs(0)
    tm = x_ref.shape[0]

    def issue(step, slot):
        def body(r, carry):
            tok = tok_offset + step * tm + r
            for k in range(TOP_K):
                pltpu.make_async_copy(y_hbm.at[pos_ref[k * n_tok + tok]],
                                      buf.at[slot, k, r], sems.at[slot]).start()
            return carry
        lax.fori_loop(0, tm, body, 0, unroll=8)

    @pl.when(i == 0)
    def _():
        issue(0, 0)

    @pl.when(i + 1 < n)
    def _():
        issue(i + 1, (i + 1) % 2)

    slot = i % 2
    pltpu.make_async_copy(buf.at[slot], buf.at[slot], sems.at[slot]).wait()
    gate = gate_ref[...]
    f = (_tiles_to_rows(buf[slot, 0]) * gate[:, 0:1] + _tiles_to_rows(buf[slot, 1]) * gate[:, 1:2])
    out = x_ref[...] + g2_ref[...] * f
    if final_norm:
        out = _rms(out) * fw_ref[...]
    o_ref[...] = out


def _combine(pos, x2d, seq_len, gate, g2, fw, y, tok_offset, final_norm):
    rows, d = x2d.shape
    n_tok = gate.shape[0]
    tm = min(256, seq_len)
    tiles_per_seq = seq_len // tm
    off = tok_offset // tm
    nb = g2.shape[0]
    mod_idx = ((lambda i, p: (i // tiles_per_seq, 0, 0)) if nb > 1 else (lambda i, p: (0, 0, 0)))
    return pl.pallas_call(
        functools.partial(_combine_kernel, tok_offset=tok_offset, n_tok=n_tok,
                          final_norm=final_norm),
        grid_spec=pltpu.PrefetchScalarGridSpec(
            num_scalar_prefetch=1,
            grid=(rows // tm,),
            in_specs=[pl.BlockSpec((tm, d), lambda i, p: (i, 0)),
                      pl.BlockSpec((tm, TOP_K), lambda i, p: (i + off, 0)),
                      pl.BlockSpec((None, 1, d), mod_idx),
                      pl.BlockSpec((1, d), lambda i, p: (0, 0)),
                      pl.BlockSpec(memory_space=pl.ANY)],
            out_specs=pl.BlockSpec((tm, d), lambda i, p: (i, 0)),
            scratch_shapes=[pltpu.VMEM((2, TOP_K, tm) + y.shape[1:], F32),
                            pltpu.SemaphoreType.DMA((2,))]),
        out_shape=jax.ShapeDtypeStruct((rows, d), F32),
        compiler_params=_params("arbitrary"),
        name="moe_combine",
    )(pos, x2d, gate, g2, fw, y)


def _rope_tables(seq_len):
    rows = seq_len // GRID_W
    row = np.repeat(np.arange(rows), GRID_W).astype(np.float64)
    col = np.tile(np.arange(GRID_W), rows).astype(np.float64)
    n = HEAD_DIM // 4
    inv = (ROPE_THETA ** (-np.arange(n, dtype=np.float32) / n)).astype(np.float64)
    ang = np.concatenate([row[:, None] * inv, col[:, None] * inv], axis=-1)
    cos = np.cos(ang.astype(np.float32).astype(np.float64))
    sin = np.sin(ang.astype(np.float32).astype(np.float64))
    cos2 = np.concatenate([cos, cos], axis=-1).astype(np.float32)
    sin2 = np.concatenate([-sin, sin], axis=-1).astype(np.float32)
    return jnp.asarray(cos2), jnp.asarray(sin2)


def kernel(x, c, ctx, c_ctx, w_ada, b_ada, norm1_w, norm2_w, w_in, q_norm_w, k_norm_w,
           hy_conv_w, hy_conv_b, hy_pe_w1, hy_pe_b1, hy_freq1, hy_pe_w2, hy_pe_b2, hy_freq2,
           hy_pe_w3, hy_skip, w_att_proj, w_hy_proj, w_out, router_w, router_b,
           exp_w1, exp_w3, exp_w2, final_norm_w):
    B, S, D = x.shape
    C = ctx.shape[1]
    depth = w_ada.shape[0]
    n_lat = B * S
    n_ctx = B * C

    cos2, sin2 = _rope_tables(S)
    cos_id = jnp.ones((C, HEAD_DIM), F32)
    sin_id = jnp.zeros((C, HEAD_DIM), F32)
    dft_xf = _dft_matrices(S, False, S // 2)
    dft_cf = _dft_matrices(C, False, C // 2)
    dft_x = _dft_matrices(S // 2, True)
    dft_c = _dft_matrices(C // 2, True)

    mod_rows = 16
    cs = jnp.zeros((mod_rows, D), F32).at[:B].set(c).at[B].set(c_ctx)
    mods = _adaln(cs, w_ada, b_ada)

    r_hi = router_w.T.astype(BF16)
    r_lo = (router_w.T - r_hi.astype(F32)).astype(BF16)
    fw = final_norm_w.reshape(1, D)

    x2d = x.reshape(n_lat, D)
    c2d = ctx.reshape(n_ctx, D)
    for i in range(depth):
        last = i == depth - 1
        m_lat = [mods[i, :B, j * D:(j + 1) * D].reshape(B, 1, D) for j in range(6)]
        m_ctx = [mods[i, B:B + 1, j * D:(j + 1) * D].reshape(1, 1, D) for j in range(6)]
        wb = w_in[i].astype(BF16)
        n1 = norm1_w[i].reshape(1, D)
        n2 = norm2_w[i].reshape(1, D)
        qn = q_norm_w[i].reshape(1, HEAD_DIM)
        kn = k_norm_w[i].reshape(1, HEAD_DIM)
        wa = w_att_proj[i].astype(BF16)
        wh = w_hy_proj[i].astype(BF16)
        wo = w_out[i].astype(BF16)

        q, k, v, u, g = _inproj(x2d, S, m_lat[0], m_lat[1], n1, wb, qn, kn, cos2, sin2,
                                ("q", "k", "v", "u", "g"))
        if last:
            kc, vc = _inproj(c2d, C, m_ctx[0], m_ctx[1], n1, wb, qn, kn, cos_id, sin_id, ("k", "v"))
        else:
            qc, kc, vc, uc, gc = _inproj(c2d, C, m_ctx[0], m_ctx[1], n1, wb, qn, kn, cos_id, sin_id,
                                         ("q", "k", "v", "u", "g"))
        ya = _attention(q, [(k, v, S), (kc, vc, C)], B, S)

        filt = (hy_pe_w1[i], hy_pe_b1[i], hy_freq1[i], hy_pe_w2[i], hy_pe_b2[i], hy_freq2[i],
                hy_pe_w3[i])
        kpack = _hy_spectrum(dft_xf[0], dft_xf[1], *_hy_filter(S, *filt))
        yh = _hyena(u, B, S, dft_x, kpack, hy_conv_w[i], hy_conv_b[i], hy_skip[i])

        n_tok = n_lat if last else n_lat + n_ctx
        x2d, hx, logits = _merge(ya, yh, g, x2d, S, m_lat[2], m_lat[3], m_lat[4], n2,
                                 wa, wh, wo, r_hi, r_lo, n_tok, 0)
        if not last:
            ya_c = _attention(qc, [(kc, vc, C)], B, C)
            kpack = _hy_spectrum(dft_cf[0], dft_cf[1], *_hy_filter(C, *filt))
            yh_c = _hyena(uc, B, C, dft_c, kpack, hy_conv_w[i], hy_conv_b[i], hy_skip[i])
            c2d, hx, logits = _merge(ya_c, yh_c, gc, c2d, C, m_ctx[2], m_ctx[3], m_ctx[4], n2,
                                     wa, wh, wo, r_hi, r_lo, n_tok, n_lat, prev=(hx, logits))

        idx, gate, rank, counts = _route(logits, router_b)
        n_blocks = -(-(n_tok * TOP_K) // MOE_ROWS) + N_EXPERTS
        pad_start, pad_lo, pad_hi, tables = _layout(counts, n_blocks)
        dest = _slots(idx, rank, pad_start).reshape(TOP_K * n_tok)
        xs = _scatter_rows(dest, pad_lo, pad_hi, hx, n_blocks)
        y = _experts(tables, xs, exp_w1, exp_w3, exp_w2, i)
        gate_t = gate.T
        x2d = _combine(dest, x2d, S, gate_t, m_lat[5], fw, y, 0, last)
        if not last:
            c2d = _combine(dest, c2d, C, gate_t, m_ctx[5], fw, y, n_lat, False)
    return x2d.reshape(B, S, D)
```

```python
import functools
import math

import numpy as np
import jax
import jax.numpy as jnp
from jax import lax
from jax.experimental import pallas as pl
from jax.experimental.pallas import tpu as pltpu

F32 = jnp.float32
BF16 = jnp.bfloat16

D_MODEL = 1024
DEPTH = 2
GRID_W = 64
NORM_EPS = 1e-6
N_HEADS = 8
N_KV_HEADS = 2
HEAD_DIM = 128
ATT_WIDTH = N_HEADS * HEAD_DIM
KV_WIDTH = N_KV_HEADS * HEAD_DIM
ROPE_THETA = 10000.0
ATT_SCALE = HEAD_DIM ** -0.5
LOG2_E = math.log2(math.e)
HY_WIDTH = D_MODEL // 2
HY_ORDER = 2
HY_FILTER_HIDDEN = 64
HY_BANDS = 16
HY_PE_DIM = 1 + 2 * HY_BANDS
HY_PE_PAD = 128
HY_FAST_DECAY = 0.3
HY_SLOW_DECAY = 1.5
HY_DECAY_TARGET = 1e-2
N_EXPERTS = 64
N_GROUPS = 8
EXPERTS_PER_GROUP = N_EXPERTS // N_GROUPS
TOP_K = 2
EXPERT_DIM = 512
IN_WIDTH = ATT_WIDTH + 2 * KV_WIDTH + 3 * HY_WIDTH + 2 * D_MODEL
COL_Q = 0
COL_K = ATT_WIDTH
COL_V = ATT_WIDTH + KV_WIDTH
COL_U = ATT_WIDTH + 2 * KV_WIDTH
COL_G = COL_U + 3 * HY_WIDTH

MXU_COLS = 256
SUBLANES = 8
LANES = 128
ROW_TILE = 512
ATT_Q_TILE = 512
ATT_KEY_CHUNK = 512
DFT_ROWS = 64
HY_CHAINS = 4
MOE_ROWS = 256
VMEM_LIMIT = 56 * 1024 * 1024


def _dot(a, b):
    return jnp.dot(a, b, preferred_element_type=F32)


def _split(a):
    hi = a.astype(BF16)
    lo = (a - hi.astype(F32)).astype(BF16)
    return hi, lo


def _dot3(a, b):
    ah, al = _split(a)
    bh, bl = _split(b)
    return _dot(ah, bh) + (_dot(al, bh) + _dot(ah, bl))


def _tiles_to_rows(t):
    c = pltpu.einshape("tjl->jtl", t)
    return jnp.concatenate([c[j] for j in range(c.shape[0])], axis=1)


def _rows_to_tiles(x):
    c = jnp.stack([x[:, j * LANES:(j + 1) * LANES] for j in range(x.shape[1] // LANES)], axis=0)
    return pltpu.einshape("jtl->tjl", c)


def _rms(t):
    return t * lax.rsqrt(jnp.mean(t * t, axis=-1, keepdims=True) + NORM_EPS)


def _params(*sem):
    return pltpu.CompilerParams(dimension_semantics=sem, vmem_limit_bytes=VMEM_LIMIT)


def _adaln_kernel(c_ref, w_ref, b_ref, o_ref):
    c = c_ref[...]
    o_ref[...] = _dot3(c * jax.nn.sigmoid(c), w_ref[...]) + b_ref[...]


def _adaln(cs, w_ada, b_ada):
    depth, d, n = w_ada.shape
    rows = cs.shape[0]
    tn = 1536
    return pl.pallas_call(
        _adaln_kernel,
        grid=(depth, n // tn),
        in_specs=[pl.BlockSpec((rows, d), lambda l, j: (0, 0)),
                  pl.BlockSpec((None, d, tn), lambda l, j: (l, 0, j)),
                  pl.BlockSpec((None, 1, tn), lambda l, j: (l, 0, j))],
        out_specs=pl.BlockSpec((None, rows, tn), lambda l, j: (l, 0, j)),
        out_shape=jax.ShapeDtypeStruct((depth, rows, n), F32),
        compiler_params=_params("parallel", "parallel"),
        name="adaln",
    )(cs, w_ada, b_ada.reshape(depth, 1, n))


def _inproj_kernel(x_ref, sh_ref, sc_ref, nw_ref, w_ref, qn_ref, kn_ref, cos_ref, sin_ref,
                   *outs, sections):
    h = _rms(x_ref[...]) * nw_ref[...]
    h = h * (1.0 + sc_ref[...]) + sh_ref[...]
    hb = h.astype(BF16)
    cos = cos_ref[...]
    sin = sin_ref[...]
    o = dict(zip(sections, outs))

    def head_cols(col0, n_cols, norm_w, scale, out_ref):
        for c in range(n_cols // MXU_COLS):
            acc = _dot(hb, w_ref[:, col0 + c * MXU_COLS:col0 + (c + 1) * MXU_COLS])
            for j in range(MXU_COLS // HEAD_DIM):
                t = _rms(acc[:, j * HEAD_DIM:(j + 1) * HEAD_DIM]) * norm_w
                t = t * cos + pltpu.roll(t, HEAD_DIM // 2, 1) * sin
                lo = c * MXU_COLS + j * HEAD_DIM
                out_ref[:, lo:lo + HEAD_DIM] = (t * scale).astype(out_ref.dtype)

    if "q" in o:
        head_cols(COL_Q, ATT_WIDTH, qn_ref[...], ATT_SCALE * LOG2_E, o["q"])
    if "k" in o:
        head_cols(COL_K, KV_WIDTH, kn_ref[...], 1.0, o["k"])
    if "v" in o:
        acc = _dot(hb, w_ref[:, COL_V:COL_V + KV_WIDTH]).astype(BF16)
        ones = jnp.ones((acc.shape[0], HEAD_DIM), BF16)
        for j in range(N_KV_HEADS):
            o["v"][:, 2 * j * HEAD_DIM:(2 * j + 1) * HEAD_DIM] = acc[:, j * HEAD_DIM:(j + 1) * HEAD_DIM]
            o["v"][:, (2 * j + 1) * HEAD_DIM:(2 * j + 2) * HEAD_DIM] = ones
    if "u" in o:
        for c in range(3):
            o["u"][:, c * HY_WIDTH:(c + 1) * HY_WIDTH] = _dot(
                hb, w_ref[:, COL_U + c * HY_WIDTH:COL_U + (c + 1) * HY_WIDTH]).astype(BF16)
    if "g" in o:
        for c in range(4):
            acc = _dot(hb, w_ref[:, COL_G + c * 512:COL_G + (c + 1) * 512])
            o["g"][:, c * 512:(c + 1) * 512] = jax.nn.sigmoid(acc).astype(BF16)


_SECTION_SHAPES = {"q": (ATT_WIDTH, BF16), "k": (KV_WIDTH, BF16), "v": (2 * KV_WIDTH, BF16),
                   "u": (3 * HY_WIDTH, BF16), "g": (2 * D_MODEL, BF16)}


def _inproj(x2d, seq_len, sh, sc, nw, w_bf16, qn, kn, cos2, sin2, sections):
    rows, d = x2d.shape
    tm = min(ROW_TILE, seq_len)
    tiles_per_seq = seq_len // tm
    nb = sh.shape[0]
    mod_idx = (lambda i: (i // tiles_per_seq, 0, 0)) if nb > 1 else (lambda i: (0, 0, 0))
    const2 = lambda i: (0, 0)
    out_shape = [jax.ShapeDtypeStruct((rows, _SECTION_SHAPES[s][0]), _SECTION_SHAPES[s][1])
                 for s in sections]
    out_specs = [pl.BlockSpec((tm, _SECTION_SHAPES[s][0]), lambda i: (i, 0)) for s in sections]
    return pl.pallas_call(
        functools.partial(_inproj_kernel, sections=tuple(sections)),
        grid=(rows // tm,),
        in_specs=[pl.BlockSpec((tm, d), lambda i: (i, 0)),
                  pl.BlockSpec((None, 1, d), mod_idx),
                  pl.BlockSpec((None, 1, d), mod_idx),
                  pl.BlockSpec((1, d), const2),
                  pl.BlockSpec((d, IN_WIDTH), const2),
                  pl.BlockSpec((1, HEAD_DIM), const2),
                  pl.BlockSpec((1, HEAD_DIM), const2),
                  pl.BlockSpec((tm, HEAD_DIM), lambda i: (i % tiles_per_seq, 0)),
                  pl.BlockSpec((tm, HEAD_DIM), lambda i: (i % tiles_per_seq, 0))],
        out_specs=out_specs,
        out_shape=out_shape,
        compiler_params=_params("parallel"),
        name="inproj",
    )(x2d, sh, sc, nw, w_bf16, qn, kn, cos2, sin2)


def _attn_kernel(q_ref, *refs, n_kv_sets):
    kv = refs[:2 * n_kv_sets]
    o_ref = refs[2 * n_kv_sets]
    nt = (((1,), (1,)), ((), ()))
    for g in range(N_HEADS // N_KV_HEADS):
        q = q_ref[:, g * HEAD_DIM:(g + 1) * HEAD_DIM]
        m = None
        acc = None
        for s_idx in range(n_kv_sets):
            k_ref, v_ref = kv[2 * s_idx], kv[2 * s_idx + 1]
            lk = k_ref.shape[0]
            for c0 in range(0, lk, ATT_KEY_CHUNK):
                c1 = min(c0 + ATT_KEY_CHUNK, lk)
                s = lax.dot_general(q, k_ref[c0:c1, :], nt, preferred_element_type=F32)
                mc = jnp.max(s, axis=1, keepdims=True)
                if m is None:
                    m = mc
                    acc = _dot(jnp.exp2(s - m).astype(BF16), v_ref[c0:c1, :])
                else:
                    m_new = jnp.maximum(m, mc)
                    acc = jnp.exp2(m - m_new) * acc + _dot(jnp.exp2(s - m_new).astype(BF16),
                                                           v_ref[c0:c1, :])
                    m = m_new
        o_ref[:, g * HEAD_DIM:(g + 1) * HEAD_DIM] = (
            acc[:, :HEAD_DIM] / acc[:, HEAD_DIM:]).astype(o_ref.dtype)


def _attention(q, kv_sets, batch, seq_len):
    tq = min(ATT_Q_TILE, seq_len)
    nq = seq_len // tq
    grp = (N_HEADS // N_KV_HEADS) * HEAD_DIM
    in_specs = [pl.BlockSpec((tq, grp), lambda b, j, i: (b * nq + i, j))]
    args = [q]
    for k, v, lk in kv_sets:
        in_specs += [pl.BlockSpec((lk, HEAD_DIM), lambda b, j, i: (b, j)),
                     pl.BlockSpec((lk, 2 * HEAD_DIM), lambda b, j, i: (b, j))]
        args += [k, v]
    return pl.pallas_call(
        functools.partial(_attn_kernel, n_kv_sets=len(kv_sets)),
        grid=(batch, N_KV_HEADS, nq),
        in_specs=in_specs,
        out_specs=pl.BlockSpec((tq, grp), lambda b, j, i: (b * nq + i, j)),
        out_shape=jax.ShapeDtypeStruct(q.shape, BF16),
        compiler_params=_params("parallel", "parallel", "arbitrary"),
        name="attention",
    )(*args)


def _hy_filter_kernel(z_ref, t_ref, w1_ref, b1_ref, f1_ref, w2_ref, b2_ref, f2_ref, w3_ref,
                      dl_ref, sp_ref, sm_ref, spa_ref, sma_ref, mid_ref):
    i = pl.program_id(0)
    tl = z_ref.shape[0]
    h = jnp.sin(f1_ref[...] * (_dot3(z_ref[...], w1_ref[...]) + b1_ref[...]))
    h = jnp.sin(f2_ref[...] * (_dot3(h, w2_ref[...]) + b2_ref[...]))
    h = _dot3(h, w3_ref[...])
    window = jnp.exp(-t_ref[...] * dl_ref[...])
    row = i * tl + lax.broadcasted_iota(jnp.int32, (tl, HY_WIDTH), 0)
    alt = (1 - 2 * (row & 1)).astype(F32)
    quarter = row & 3
    cos4 = jnp.where(quarter == 0, 1.0, jnp.where(quarter == 2, -1.0, 0.0))
    sin4 = jnp.where(quarter == 1, 1.0, jnp.where(quarter == 3, -1.0, 0.0))

    @pl.when(i == 0)
    def _():
        mid_ref[...] = jnp.zeros_like(mid_ref)

    for o in range(HY_ORDER):
        cols = slice(o * HY_WIDTH, (o + 1) * HY_WIDTH)
        hf = h[:, (2 * o) * HY_WIDTH:(2 * o + 1) * HY_WIDTH] * window
        hb = h[:, (2 * o + 1) * HY_WIDTH:(2 * o + 2) * HY_WIDTH] * window
        hb = jnp.where(row == 0, 0.0, hb)
        plus = hf + hb
        minus = hf - hb
        sp_ref[:, cols] = plus.astype(BF16)
        sm_ref[:, cols] = minus.astype(BF16)
        spa_ref[:, cols] = (alt * plus).astype(BF16)
        sma_ref[:, cols] = (alt * minus).astype(BF16)
        mid_ref[0:1, cols] += jnp.sum(cos4 * plus, axis=0, keepdims=True)
        mid_ref[1:2, cols] += jnp.sum(sin4 * minus, axis=0, keepdims=True)


def _hy_filter(seq_len, pe_w1, pe_b1, freq1, pe_w2, pe_b2, freq2, pe_w3):
    t01 = np.linspace(0.0, 1.0, seq_len)[:, None]
    pos = np.arange(seq_len, dtype=np.float64)[:, None]
    bands = np.linspace(1e-4, HY_BANDS - 1, HY_BANDS)[None, :]
    f = 2.0 * math.pi * pos * bands / seq_len
    z = np.zeros((seq_len, HY_PE_PAD), np.float32)
    z[:, :HY_PE_DIM] = np.concatenate([t01, np.cos(f), -np.sin(f)], axis=-1)
    max_decay = math.log(HY_DECAY_TARGET) / HY_FAST_DECAY
    min_decay = math.log(HY_DECAY_TARGET) / HY_SLOW_DECAY
    deltas = np.abs(np.linspace(min_decay, max_decay, HY_WIDTH))[None, :].astype(np.float32)
    w1p = jnp.zeros((HY_PE_PAD, HY_FILTER_HIDDEN), F32).at[:HY_PE_DIM].set(pe_w1)
    tl = min(512, seq_len)
    hid = HY_FILTER_HIDDEN
    n_out = HY_ORDER * HY_WIDTH
    c2 = lambda i: (0, 0)
    return pl.pallas_call(
        _hy_filter_kernel,
        grid=(seq_len // tl,),
        in_specs=[pl.BlockSpec((tl, HY_PE_PAD), lambda i: (i, 0)),
                  pl.BlockSpec((tl, 1), lambda i: (i, 0)),
                  pl.BlockSpec((HY_PE_PAD, hid), c2), pl.BlockSpec((1, hid), c2),
                  pl.BlockSpec((1, hid), c2), pl.BlockSpec((hid, hid), c2),
                  pl.BlockSpec((1, hid), c2), pl.BlockSpec((1, hid), c2),
                  pl.BlockSpec((hid, 2 * n_out), c2), pl.BlockSpec((1, HY_WIDTH), c2)],
        out_specs=[pl.BlockSpec((tl, n_out), lambda i: (i, 0))] * 4
                  + [pl.BlockSpec((SUBLANES, n_out), c2)],
        out_shape=[jax.ShapeDtypeStruct((seq_len, n_out), BF16)] * 4
                  + [jax.ShapeDtypeStruct((SUBLANES, n_out), F32)],
        compiler_params=_params("arbitrary"),
        name="hy_filter",
    )(jnp.asarray(z), jnp.asarray(t01.astype(np.float32)), w1p, pe_b1.reshape(1, hid),
      freq1.reshape(1, hid), pe_w2, pe_b2.reshape(1, hid), freq2.reshape(1, hid), pe_w3,
      jnp.asarray(deltas))


def _dft_kernel(ca_ref, sa_ref, cb_ref, sb_ref, alt_ref, wc_ref, cm_ref, sm_ref, *inverse_refs):
    i = pl.program_id(0)
    ca = ca_ref[...]
    sa = sa_ref[...]
    cb = cb_ref[...]
    sb = sb_ref[...]
    c = ca * cb - sa * sb
    s = sa * cb + ca * sb
    rows = lax.broadcasted_iota(jnp.int32, c.shape, 0)
    cols = lax.broadcasted_iota(jnp.int32, c.shape, 1)
    wc = wc_ref[...]
    cm_ref[...] = c.astype(BF16)
    sm_ref[...] = jnp.where((rows == 0) & (i == 0), alt_ref[...], s).astype(BF16)
    if inverse_refs:
        ci_ref, si_ref = inverse_refs
        alt_t = (1 - 2 * (rows & 1)).astype(F32)
        ci_ref[...] = (c * wc).astype(BF16)
        si_ref[...] = (jnp.where(cols == 0, alt_t, s) * wc).astype(BF16)


def _dft_matrices(seq_len, inverse, n_rows=None):
    n = 2 * seq_len
    n_rows = seq_len if n_rows is None else n_rows
    idx = np.arange(seq_len, dtype=np.int64)[None, :]
    r1 = np.arange(n_rows // DFT_ROWS, dtype=np.int64)[:, None] * DFT_ROWS
    r0 = np.arange(DFT_ROWS, dtype=np.int64)[:, None]
    ang_a = ((r1 * idx) % n).astype(np.float64) * (2.0 * math.pi / n)
    ang_b = ((r0 * idx) % n).astype(np.float64) * (2.0 * math.pi / n)
    tab = lambda a: jnp.asarray(a.astype(np.float32))
    ca = tab(np.cos(ang_a)).reshape(-1, 1, seq_len)
    sa = tab(np.sin(ang_a)).reshape(-1, 1, seq_len)
    alt = tab(1.0 - 2.0 * (idx % 2))
    wc = tab(np.where(idx == 0, 1.0, 2.0) / n)
    row_blk = pl.BlockSpec((None, 1, seq_len), lambda i: (i, 0, 0))
    full = lambda r: pl.BlockSpec((r, seq_len), lambda i: (0, 0))
    out_blk = pl.BlockSpec((DFT_ROWS, seq_len), lambda i: (i, 0))
    n_out = 4 if inverse else 2
    mats = pl.pallas_call(
        _dft_kernel,
        grid=(n_rows // DFT_ROWS,),
        in_specs=[row_blk, row_blk, full(DFT_ROWS), full(DFT_ROWS), full(1), full(1)],
        out_specs=[out_blk] * n_out,
        out_shape=[jax.ShapeDtypeStruct((n_rows, seq_len), BF16)] * n_out,
        compiler_params=_params("parallel"),
        name="dft_matrices",
    )(ca, sa, tab(np.cos(ang_b)), tab(np.sin(ang_b)), alt, wc)
    if not inverse:
        return tuple(mats)
    tw = np.arange(seq_len, dtype=np.float64)[:, None] * (2.0 * math.pi / (2 * n))
    rep = lambda a: tab(np.broadcast_to(a, (seq_len, LANES)))
    return tuple(mats) + (rep(np.cos(tw)), rep(np.sin(tw)))


def _hy_spec_kernel(cm_ref, sm_ref, sp_ref, smn_ref, spa_ref, sma_ref, mid_ref,
                    klr_ref, kli_ref, khr_ref, khi_ref):
    i = pl.program_id(1)
    cm = cm_ref[...]
    sm = sm_ref[...]
    klr_ref[...] = _dot(cm, sp_ref[...])
    khr_ref[...] = _dot(cm, spa_ref[...])
    lo_q = _dot(sm, smn_ref[...])
    hi_q = _dot(sm, sma_ref[...])
    rows = lax.broadcasted_iota(jnp.int32, lo_q.shape, 0)
    edge = (rows == 0) & (i == 0)
    kli_ref[...] = jnp.where(edge, mid_ref[0:1, :], -lo_q)
    khi_ref[...] = jnp.where(edge, -mid_ref[1:2, :], hi_q)


def _hy_spectrum(cm, sm, splus, sminus, splus_alt, sminus_alt, mid):
    half, seq_len = cm.shape
    tf = min(256, half)
    mat = pl.BlockSpec((tf, seq_len), lambda o, i: (i, 0))
    sig = pl.BlockSpec((seq_len, HY_WIDTH), lambda o, i: (0, o))
    out = pl.BlockSpec((tf, HY_WIDTH), lambda o, i: (i, o))
    return pl.pallas_call(
        _hy_spec_kernel,
        grid=(HY_ORDER, half // tf),
        in_specs=[mat, mat, sig, sig, sig, sig,
                  pl.BlockSpec((SUBLANES, HY_WIDTH), lambda o, i: (0, o))],
        out_specs=[out] * 4,
        out_shape=[jax.ShapeDtypeStruct((half, HY_ORDER * HY_WIDTH), F32)] * 4,
        compiler_params=_params("parallel", "arbitrary"),
        name="hy_spectrum",
    )(cm, sm, splus, sminus, splus_alt, sminus_alt, mid)


def _hy_sconv_kernel(u_ref, w_ref, b_ref, ue_ref, uo_ref):
    half = u_ref.shape[0] // 2
    words = pltpu.bitcast(u_ref[...], jnp.uint32)
    xe = lax.bitcast_convert_type(words << 16, F32)
    xo = lax.bitcast_convert_type(words & jnp.uint32(0xFFFF0000), F32)
    rows = lax.broadcasted_iota(jnp.int32, xe.shape, 0)
    xo_prev = jnp.where(rows == 0, 0.0, pltpu.roll(xo, 1, 0))
    xe_next = jnp.where(rows == half - 1, 0.0, pltpu.roll(xe, half - 1, 0))
    w0, w1, w2 = w_ref[0:1, :], w_ref[1:2, :], w_ref[2:3, :]
    ye = xo_prev * w0 + xe * w1 + xo * w2 + b_ref[...]
    yo = xe * w0 + xo * w1 + xe_next * w2 + b_ref[...]
    ue_ref[...] = ye.astype(ue_ref.dtype)
    uo_ref[...] = yo.astype(uo_ref.dtype)


def _hy_sconv(u, batch, seq_len, conv_w, conv_b):
    ch = u.shape[1]
    tc = 256
    half = seq_len // 2
    u3 = u.reshape(batch, seq_len, ch)
    plane = pl.BlockSpec((None, half, tc), lambda b, j: (b, 0, j))
    ue, uo = pl.pallas_call(
        _hy_sconv_kernel,
        grid=(batch, ch // tc),
        in_specs=[pl.BlockSpec((None, seq_len, tc), lambda b, j: (b, 0, j)),
                  pl.BlockSpec((3, tc), lambda b, j: (0, j)),
                  pl.BlockSpec((1, tc), lambda b, j: (0, j))],
        out_specs=[plane, plane],
        out_shape=[jax.ShapeDtypeStruct((batch, half, ch), BF16)] * 2,
        compiler_params=_params("parallel", "parallel"),
        name="hy_sconv",
    )(u3, conv_w, conv_b.reshape(1, ch))
    return ue, uo


def _hy_fwd_kernel(cm_ref, sm_ref, tc_ref, ts_ref, ve_ref, vo_ref, klr_ref, kli_ref, khr_ref,
                   khi_ref, zp_ref, zq_ref):
    i = pl.program_id(0)
    w = ve_ref.shape[1]
    wc = w // HY_CHAINS
    reps = wc // LANES
    tc = jnp.concatenate([tc_ref[...]] * reps, axis=1)
    ts = jnp.concatenate([ts_ref[...]] * reps, axis=1)
    rows = lax.broadcasted_iota(jnp.int32, (cm_ref.shape[0], wc), 0)
    edge = (rows == 0) & (i == 0)
    cm = cm_ref[...]
    sm = sm_ref[...]
    for c in range(HY_CHAINS):
        lo, hi = c * wc, (c + 1) * wc
        v = jnp.concatenate([ve_ref[:, lo:hi], vo_ref[:, lo:hi]], axis=1)
        p = _dot(cm, v)
        q = _dot(sm, v)
        pe, po, qe, qo = p[:, :wc], p[:, wc:], q[:, :wc], q[:, wc:]

        a = tc * po - ts * qo
        b = tc * qo + ts * po
        xlr = pe + a
        xhr = pe - a
        xli = jnp.where(edge, qe, -(qe + b))
        xhi = jnp.where(edge, -qo, qe - b)

        klr, kli, khr, khi = klr_ref[:, lo:hi], kli_ref[:, lo:hi], khr_ref[:, lo:hi], khi_ref[:, lo:hi]
        ylr = xlr * klr - jnp.where(edge, 0.0, xli * kli)
        yhr = xhr * khr - jnp.where(edge, 0.0, xhi * khi)
        yli = jnp.where(edge, xli * kli - xhi * khi, xlr * kli + xli * klr)
        yhi = jnp.where(edge, xli * khi + xhi * kli, xhr * khi + xhi * khr)

        dr = ylr - yhr
        di = yli + yhi
        zp_ref[:, lo:hi] = (0.5 * (ylr + yhr)).astype(BF16)
        zp_ref[:, w + lo:w + hi] = (0.5 * (tc * dr - ts * di)).astype(BF16)
        zq_ref[:, lo:hi] = jnp.where(edge, yli, -0.5 * (yli - yhi)).astype(BF16)
        zq_ref[:, w + lo:w + hi] = jnp.where(edge, -yhi, -0.5 * (tc * di + ts * dr)).astype(BF16)


def _hy_forward(dfth, v_planes, v_blk, kpack, order):
    cm, sm, _, _, tc, ts = dfth
    batch, half, _ = v_planes[0].shape
    w = HY_WIDTH
    tf = min(512, half)
    mat = pl.BlockSpec((tf, half), lambda i, b: (i, 0))
    tw = pl.BlockSpec((tf, LANES), lambda i, b: (i, 0))
    spec = pl.BlockSpec((tf, w), lambda i, b: (i, order))
    out = pl.BlockSpec((None, tf, 2 * w), lambda i, b: (b, i, 0))
    return pl.pallas_call(
        _hy_fwd_kernel,
        grid=(half // tf, batch),
        in_specs=[mat, mat, tw, tw,
                  pl.BlockSpec((None, half, w), lambda i, b: (b, 0, v_blk)),
                  pl.BlockSpec((None, half, w), lambda i, b: (b, 0, v_blk)),
                  spec, spec, spec, spec],
        out_specs=[out, out],
        out_shape=[jax.ShapeDtypeStruct((batch, half, 2 * w), BF16)] * 2,
        compiler_params=_params("parallel", "arbitrary"),
        name="hy_forward",
    )(cm, sm, tc, ts, v_planes[0], v_planes[1], *kpack)


def _hy_inv_kernel(ci_ref, si_ref, zp_ref, zq_ref, ae_ref, ao_ref, ce_ref, co_ref, skip_ref, *outs,
                   interleave):
    w = ae_ref.shape[1]
    tt = ae_ref.shape[0]
    y = _dot(ci_ref[...], zp_ref[...]) + _dot(si_ref[...], zq_ref[...])
    skip = skip_ref[...]
    ze = ae_ref[...].astype(F32) * (y[:, :w] + skip * ce_ref[...].astype(F32))
    zo = ao_ref[...].astype(F32) * (y[:, w:] + skip * co_ref[...].astype(F32))
    if interleave:
        (o_ref,) = outs
        bits = lambda t: lax.bitcast_convert_type(t.astype(BF16).astype(F32), jnp.uint32)
        words = (bits(ze) >> 16) | (bits(zo) & jnp.uint32(0xFFFF0000))
        o_ref[...] = pltpu.bitcast(words, BF16)
    else:
        for k in range(len(outs) // 2):
            outs[2 * k][...] = ze.astype(outs[2 * k].dtype)
            outs[2 * k + 1][...] = zo.astype(outs[2 * k + 1].dtype)


def _hy_inverse(dfth, zp, zq, a_planes, a_blk, c_planes, c_blk, skip, out_dtypes, interleave):
    _, _, ci, si, _, _ = dfth
    batch, half, w2 = zp.shape
    w = w2 // 2
    tt = min(512, half)
    mat = pl.BlockSpec((tt, half), lambda i, b: (i, 0))
    sig = pl.BlockSpec((None, half, w2), lambda i, b: (b, 0, 0))
    a_spec = pl.BlockSpec((None, tt, w), lambda i, b: (b, i, a_blk))
    c_spec = pl.BlockSpec((None, tt, w), lambda i, b: (b, i, c_blk))
    plane = pl.BlockSpec((None, tt, w), lambda i, b: (b, i, 0))
    if interleave:
        (dt,) = out_dtypes
        out_specs = [pl.BlockSpec((None, 2 * tt, w), lambda i, b: (b, i, 0))]
        out_shape = [jax.ShapeDtypeStruct((batch, 2 * half, w), dt)]
    else:
        out_specs = [plane] * (2 * len(out_dtypes))
        out_shape = [jax.ShapeDtypeStruct((batch, half, w), dt) for dt in out_dtypes for _ in (0, 1)]
    outs = pl.pallas_call(
        functools.partial(_hy_inv_kernel, interleave=interleave),
        grid=(half // tt, batch),
        in_specs=[mat, mat, sig, sig, a_spec, a_spec, c_spec, c_spec,
                  pl.BlockSpec((1, w), lambda i, b: (0, 0))],
        out_specs=out_specs,
        out_shape=out_shape,
        compiler_params=_params("parallel", "arbitrary"),
        name="hy_inverse",
    )(ci, si, zp, zq, a_planes[0], a_planes[1], c_planes[0], c_planes[1], skip)
    if interleave:
        return outs[0]
    return [(outs[2 * k], outs[2 * k + 1]) for k in range(len(out_dtypes))]


def _hyena(u, batch, seq_len, dfth, kpack, conv_w, conv_b, skip):
    uc = _hy_sconv(u, batch, seq_len, conv_w, conv_b)
    zp, zq = _hy_forward(dfth, uc, 2, kpack, 0)
    z, zb = _hy_inverse(dfth, zp, zq, uc, 0, uc, 2, skip[0:1], (F32, BF16), False)
    zp, zq = _hy_forward(dfth, zb, 0, kpack, 1)
    y = _hy_inverse(dfth, zp, zq, uc, 1, z, 0, skip[1:2], (BF16,), True)
    return y.reshape(batch * seq_len, HY_WIDTH)


def _merge_kernel(ya_ref, yh_ref, g_ref, x_ref, g1_ref, sh_ref, sc_ref, nw_ref,
                  wa_ref, wh_ref, wo_ref, rh_ref, rl_ref, *rest):
    xo_ref, hx_ref, lg_ref = rest[-3:]
    a = _dot(ya_ref[...], wa_ref[...])
    h = _dot(yh_ref[...], wh_ref[...])
    m = g_ref[:, :D_MODEL].astype(F32) * a + g_ref[:, D_MODEL:].astype(F32) * h
    xn = x_ref[...] + g1_ref[...] * _dot(m.astype(BF16), wo_ref[...])
    xo_ref[...] = xn
    hx = _rms(xn) * nw_ref[...]
    hx = hx * (1.0 + sc_ref[...]) + sh_ref[...]
    hx_ref[...] = _rows_to_tiles(hx)
    hh, hl = _split(hx)
    nt = (((1,), (1,)), ((), ()))
    dn = lambda a, b: lax.dot_general(a, b, nt, preferred_element_type=F32)
    lg_ref[...] = dn(rh_ref[...], hh) + (dn(rh_ref[...], hl) + dn(rl_ref[...], hh))


def _merge(ya, yh, g, x2d, seq_len, g1, sh2, sc2, nw, wa, wh, wo, r_hi, r_lo,
           total_rows, row_offset, prev=None):
    rows, d = x2d.shape
    tm = min(ROW_TILE, seq_len)
    tiles_per_seq = seq_len // tm
    off = row_offset // tm
    nb = g1.shape[0]
    mod_idx = (lambda i: (i // tiles_per_seq, 0, 0)) if nb > 1 else (lambda i: (0, 0, 0))
    c2 = lambda i: (0, 0)
    row = lambda w: pl.BlockSpec((tm, w), lambda i: (i, 0))
    in_specs = [row(ATT_WIDTH), row(HY_WIDTH), row(2 * d), row(d),
                pl.BlockSpec((None, 1, d), mod_idx), pl.BlockSpec((None, 1, d), mod_idx),
                pl.BlockSpec((None, 1, d), mod_idx), pl.BlockSpec((1, d), c2),
                pl.BlockSpec((ATT_WIDTH, d), c2), pl.BlockSpec((HY_WIDTH, d), c2),
                pl.BlockSpec((d, d), c2), pl.BlockSpec((N_EXPERTS, d), c2),
                pl.BlockSpec((N_EXPERTS, d), c2)]
    args = [ya, yh, g, x2d, g1, sh2, sc2, nw, wa, wh, wo, r_hi, r_lo]
    aliases = {}
    if prev is not None:
        in_specs += [pl.BlockSpec(memory_space=pl.ANY)] * 2
        aliases = {len(args): 1, len(args) + 1: 2}
        args += list(prev)
    return pl.pallas_call(
        _merge_kernel,
        grid=(rows // tm,),
        in_specs=in_specs,
        out_specs=[row(d), pl.BlockSpec((tm, d // LANES, LANES), lambda i: (i + off, 0, 0)),
                   pl.BlockSpec((N_EXPERTS, tm), lambda i: (0, i + off))],
        out_shape=[jax.ShapeDtypeStruct((rows, d), F32),
                   jax.ShapeDtypeStruct((total_rows, d // LANES, LANES), F32),
                   jax.ShapeDtypeStruct((N_EXPERTS, total_rows), F32)],
        input_output_aliases=aliases,
        compiler_params=_params("arbitrary"),
        name="merge",
    )(*args)


def _route_kernel(lg_ref, bias_ref, idx_ref, gate_ref, rank_ref, cnt_ref, run_ref):
    step = pl.program_id(0)
    tn = lg_ref.shape[2]
    shape = (N_GROUPS, EXPERTS_PER_GROUP, tn)
    s = jax.nn.sigmoid(lg_ref[...])
    b = s + bias_ref[...]
    mem = lax.broadcasted_iota(jnp.int32, shape, 1).astype(F32)
    grp = lax.broadcasted_iota(jnp.int32, (N_GROUPS, 1, tn), 0).astype(F32)
    big = float(N_EXPERTS)

    m1 = jnp.max(b, axis=1, keepdims=True)
    i1 = jnp.min(jnp.where(b == m1, mem, big), axis=1, keepdims=True)
    b2 = jnp.where(mem == i1, -jnp.inf, b)
    m2 = jnp.max(b2, axis=1, keepdims=True)
    i2 = jnp.min(jnp.where(b2 == m2, mem, big), axis=1, keepdims=True)
    gs = m1 + m2
    gmax = jnp.max(gs, axis=0, keepdims=True)
    gsel = jnp.min(jnp.where(gs == gmax, grp, big), axis=0, keepdims=True)
    selg = grp == gsel

    @pl.when(step == 0)
    def _():
        run_ref[...] = jnp.zeros_like(run_ref)

    tri = (lax.broadcasted_iota(jnp.int32, (tn, tn), 0)
           < lax.broadcasted_iota(jnp.int32, (tn, tn), 1)).astype(BF16)
    run = run_ref[...]
    ws = []
    for k, ik in enumerate((i1, i2)):
        hit = selg & (mem == ik)
        ws.append(jnp.sum(jnp.where(hit, s, 0.0), axis=(0, 1), keepdims=True))
        e_loc = jnp.sum(jnp.where(selg, ik, 0.0), axis=0, keepdims=True)
        idx_ref[k:k + 1, :] = (gsel * EXPERTS_PER_GROUP + e_loc).reshape(1, tn).astype(jnp.int32)
        oh = jnp.where(hit, 1.0, 0.0).reshape(N_EXPERTS, tn)
        before = run + _dot(oh.astype(BF16), tri)
        rank_ref[k:k + 1, :] = jnp.sum(oh * before, axis=0, keepdims=True).astype(jnp.int32)
        run = run + jnp.sum(oh, axis=1, keepdims=True)
    run_ref[...] = run
    cnt_ref[...] = run
    tot = ws[0] + ws[1]
    for k in range(TOP_K):
        gate_ref[k:k + 1, :] = (ws[k] / tot).reshape(1, tn)


def _route(logits_t, router_b):
    n_tok = logits_t.shape[1]
    tn = 512
    lg3 = logits_t.reshape(N_GROUPS, EXPERTS_PER_GROUP, n_tok)
    row2 = pl.BlockSpec((TOP_K, tn), lambda i: (0, i))
    return pl.pallas_call(
        _route_kernel,
        grid=(n_tok // tn,),
        in_specs=[pl.BlockSpec((N_GROUPS, EXPERTS_PER_GROUP, tn), lambda i: (0, 0, i)),
                  pl.BlockSpec((N_GROUPS, EXPERTS_PER_GROUP, 1), lambda i: (0, 0, 0))],
        out_specs=[row2, row2, row2, pl.BlockSpec((N_EXPERTS, 1), lambda i: (0, 0))],
        out_shape=[jax.ShapeDtypeStruct((TOP_K, n_tok), jnp.int32),
                   jax.ShapeDtypeStruct((TOP_K, n_tok), F32),
                   jax.ShapeDtypeStruct((TOP_K, n_tok), jnp.int32),
                   jax.ShapeDtypeStruct((N_EXPERTS, 1), F32)],
        scratch_shapes=[pltpu.VMEM((N_EXPERTS, 1), F32)],
        compiler_params=_params("arbitrary"),
        name="moe_route",
    )(lg3, router_b.astype(F32).reshape(N_GROUPS, EXPERTS_PER_GROUP, 1))


def _slot_kernel(idx_ref, rank_ref, start_ref, dest_ref):
    tn = idx_ref.shape[1]
    e = lax.broadcasted_iota(jnp.int32, (N_EXPERTS, tn), 0)
    for k in range(TOP_K):
        base = jnp.sum(jnp.where(e == idx_ref[k:k + 1, :], start_ref[...], 0.0), axis=0,
                       keepdims=True)
        dest_ref[k:k + 1, :] = base.astype(jnp.int32) + rank_ref[k:k + 1, :]


def _slots(idx, rank, pad_start):
    n_tok = idx.shape[1]
    tn = 512
    row2 = pl.BlockSpec((TOP_K, tn), lambda i: (0, i))
    return pl.pallas_call(
        _slot_kernel,
        grid=(n_tok // tn,),
        in_specs=[row2, row2, pl.BlockSpec((N_EXPERTS, 1), lambda i: (0, 0))],
        out_specs=row2,
        out_shape=jax.ShapeDtypeStruct((TOP_K, n_tok), jnp.int32),
        compiler_params=_params("parallel"),
        name="moe_slots",
    )(idx, rank, pad_start.astype(F32).reshape(N_EXPERTS, 1))


def _layout(counts, n_blocks):
    counts = counts.reshape(N_EXPERTS).astype(jnp.int32)
    padded = (counts + MOE_ROWS - 1) // MOE_ROWS * MOE_ROWS
    pad_end = jnp.cumsum(padded)
    pad_start = pad_end - padded
    blk_row = jnp.arange(n_blocks, dtype=jnp.int32)[:, None] * MOE_ROWS
    block_e = jnp.minimum(jnp.sum((pad_end[None, :] <= blk_row).astype(jnp.int32), axis=1),
                          N_EXPERTS - 1)
    prev_e = jnp.concatenate([jnp.full((1,), -1, jnp.int32), block_e[:-1]])
    block_new = (block_e != prev_e).astype(jnp.int32)
    block_slot = (jnp.cumsum(block_new) - 1) & 1
    e_ids = jnp.arange(N_EXPERTS, dtype=jnp.int32)
    later = (e_ids[None, :] > e_ids[:, None]) & (counts[None, :] > 0)
    next_e = jnp.min(jnp.where(later, e_ids[None, :], N_EXPERTS), axis=1)
    next_e = jnp.where(next_e == N_EXPERTS, -1, next_e)
    block_next = jnp.sum(jnp.where(block_e[:, None] == e_ids[None, :], next_e[None, :], 0), axis=1)
    n_used = (pad_end[-1] // MOE_ROWS).astype(jnp.int32).reshape(1)
    tables = (block_e, block_new, block_slot.astype(jnp.int32), block_next.astype(jnp.int32), n_used)
    return pad_start, pad_start + counts, pad_end, tables


_PAD_CHUNKS = tuple(2 ** p for p in range(int(math.log2(MOE_ROWS)) - 1, -1, -1))


def _scatter_kernel(dest_ref, lo_ref, hi_ref, x_ref, o_hbm, zeros, sem, zsem, *, n_tok):
    i = pl.program_id(0)
    tm = x_ref.shape[0]

    def pad_copies(act):
        def per_expert(e, carry):
            off = lo_ref[e]
            n = hi_ref[e] - off
            for c in _PAD_CHUNKS:
                @pl.when((n & c) != 0)
                def _():
                    act(pltpu.make_async_copy(zeros.at[pl.ds(0, c)], o_hbm.at[pl.ds(off, c)], zsem))
                off = off + (n & c)
            return carry
        lax.fori_loop(0, N_EXPERTS, per_expert, 0)

    @pl.when(i == 0)
    def _():
        zeros[...] = jnp.zeros_like(zeros)
        pad_copies(lambda cp: cp.start())

    def issue(r, carry):
        for k in range(TOP_K):
            pltpu.make_async_copy(x_ref.at[r], o_hbm.at[dest_ref[k * n_tok + i * tm + r]],
                                  sem).start(priority=k)
        return carry
    lax.fori_loop(0, tm, issue, 0, unroll=8)
    for k in range(TOP_K):
        pltpu.make_async_copy(x_ref, o_hbm.at[pl.ds(0, tm)], sem).wait()

    @pl.when(i == 0)
    def _():
        pad_copies(lambda cp: cp.wait())


def _scatter_rows(dest_flat, pad_lo, pad_hi, h, n_blocks):
    n_tok = h.shape[0]
    tile = h.shape[1:]
    tm = 256
    return pl.pallas_call(
        functools.partial(_scatter_kernel, n_tok=n_tok),
        grid_spec=pltpu.PrefetchScalarGridSpec(
            num_scalar_prefetch=3,
            grid=(n_tok // tm,),
            in_specs=[pl.BlockSpec((tm,) + tile, lambda i, *_: (i, 0, 0))],
            out_specs=pl.BlockSpec(memory_space=pl.ANY),
            scratch_shapes=[pltpu.VMEM((MOE_ROWS // 2,) + tile, h.dtype),
                            pltpu.SemaphoreType.DMA(()), pltpu.SemaphoreType.DMA(())]),
        out_shape=jax.ShapeDtypeStruct((n_blocks * MOE_ROWS,) + tile, h.dtype),
        compiler_params=pltpu.CompilerParams(dimension_semantics=("arbitrary",),
                                             vmem_limit_bytes=VMEM_LIMIT, has_side_effects=True),
        name="moe_scatter",
    )(dest_flat, pad_lo, pad_hi, h)


def _expert_kernel(be_ref, new_ref, slot_ref, next_ref, nused_ref, x_ref, w1_hbm, w3_hbm, w2_hbm,
                   y_ref, w1f, w3f, w2f, w1b, w3b, w2b, sems, *, layer):
    i = pl.program_id(0)
    hbm = (w1_hbm, w3_hbm, w2_hbm)
    f32_bufs = (w1f, w3f, w2f)
    bf16_bufs = (w1b, w3b, w2b)

    def copies(e, s):
        return [pltpu.make_async_copy(hbm[k].at[layer, e], f32_bufs[k].at[s], sems.at[s, k])
                for k in range(3)]

    @pl.when(i < nused_ref[0])
    def _():
        @pl.when(new_ref[i] == 1)
        def _():
            s = slot_ref[i]

            @pl.when(i == 0)
            def _():
                for cp in copies(be_ref[0], 0):
                    cp.start()

            for cp in copies(be_ref[i], s):
                cp.wait()

            @pl.when(next_ref[i] >= 0)
            def _():
                for cp in copies(next_ref[i], 1 - s):
                    cp.start()

            for k in range(3):
                bf16_bufs[k][...] = f32_bufs[k][s].astype(BF16)

        x = _tiles_to_rows(x_ref[...]).astype(BF16)
        a = _dot(x, w1b[...])
        b = _dot(x, w3b[...])
        h = (a * jax.nn.sigmoid(a)) * b
        y_ref[...] = _rows_to_tiles(_dot(h.astype(BF16), w2b[...]))


def _experts(tables, xs, w1, w3, w2, layer):
    n_slots = xs.shape[0]
    tile = xs.shape[1:]
    d = tile[0] * tile[1]
    n_blocks = n_slots // MOE_ROWS
    e_dim = w1.shape[3]
    blk = lambda i, be, bn, bs, bx, nu: (jnp.minimum(i, nu[0] - 1), 0, 0)
    hbm = pl.BlockSpec(memory_space=pl.ANY)
    return pl.pallas_call(
        functools.partial(_expert_kernel, layer=layer),
        grid_spec=pltpu.PrefetchScalarGridSpec(
            num_scalar_prefetch=5,
            grid=(n_blocks,),
            in_specs=[pl.BlockSpec((MOE_ROWS,) + tile, blk), hbm, hbm, hbm],
            out_specs=pl.BlockSpec((MOE_ROWS,) + tile, blk),
            scratch_shapes=[pltpu.VMEM((2, d, e_dim), F32), pltpu.VMEM((2, d, e_dim), F32),
                            pltpu.VMEM((2, e_dim, d), F32),
                            pltpu.VMEM((d, e_dim), BF16), pltpu.VMEM((d, e_dim), BF16),
                            pltpu.VMEM((e_dim, d), BF16),
                            pltpu.SemaphoreType.DMA((2, 3))]),
        out_shape=jax.ShapeDtypeStruct(xs.shape, F32),
        compiler_params=_params("arbitrary"),
        name="moe_experts",
    )(*tables, xs, w1, w3, w2)


def _combine_kernel(pos_ref, x_ref, gate_ref, g2_ref, fw_ref, y_hbm, o_ref, buf, sems, *,
                    tok_offset, n_tok, final_norm):
    i = pl.program_id(0)
    n = pl.num_programs(0)
    tm = x_ref.shape[0]

    def issue(step, slot):
        def body(r, carry):
            tok = tok_offset + step * tm + r
            for k in range(TOP_K):
                pltpu.make_async_copy(y_hbm.at[pos_ref[k * n_tok + tok]],
                                      buf.at[slot, k, r], sems.at[slot]).start(priority=k)
            return carry
        lax.fori_loop(0, tm, body, 0, unroll=8)

    @pl.when(i == 0)
    def _():
        issue(0, 0)

    @pl.when(i + 1 < n)
    def _():
        issue(i + 1, (i + 1) % 2)

    slot = i % 2
    pltpu.make_async_copy(buf.at[slot], buf.at[slot], sems.at[slot]).wait()
    gate = gate_ref[...]
    f = (_tiles_to_rows(buf[slot, 0]) * gate[:, 0:1] + _tiles_to_rows(buf[slot, 1]) * gate[:, 1:2])
    out = x_ref[...] + g2_ref[...] * f
    if final_norm:
        out = _rms(out) * fw_ref[...]
    o_ref[...] = out


def _combine(pos, x2d, seq_len, gate, g2, fw, y, tok_offset, final_norm):
    rows, d = x2d.shape
    n_tok = gate.shape[0]
    tm = min(256, seq_len)
    tiles_per_seq = seq_len // tm
    off = tok_offset // tm
    nb = g2.shape[0]
    mod_idx = ((lambda i, p: (i // tiles_per_seq, 0, 0)) if nb > 1 else (lambda i, p: (0, 0, 0)))
    return pl.pallas_call(
        functools.partial(_combine_kernel, tok_offset=tok_offset, n_tok=n_tok,
                          final_norm=final_norm),
        grid_spec=pltpu.PrefetchScalarGridSpec(
            num_scalar_prefetch=1,
            grid=(rows // tm,),
            in_specs=[pl.BlockSpec((tm, d), lambda i, p: (i, 0)),
                      pl.BlockSpec((tm, TOP_K), lambda i, p: (i + off, 0)),
                      pl.BlockSpec((None, 1, d), mod_idx),
                      pl.BlockSpec((1, d), lambda i, p: (0, 0)),
                      pl.BlockSpec(memory_space=pl.ANY)],
            out_specs=pl.BlockSpec((tm, d), lambda i, p: (i, 0)),
            scratch_shapes=[pltpu.VMEM((2, TOP_K, tm) + y.shape[1:], F32),
                            pltpu.SemaphoreType.DMA((2,))]),
        out_shape=jax.ShapeDtypeStruct((rows, d), F32),
        compiler_params=_params("arbitrary"),
        name="moe_combine",
    )(pos, x2d, gate, g2, fw, y)


def _rope_tables(seq_len):
    rows = seq_len // GRID_W
    row = np.repeat(np.arange(rows), GRID_W).astype(np.float64)
    col = np.tile(np.arange(GRID_W), rows).astype(np.float64)
    n = HEAD_DIM // 4
    inv = (ROPE_THETA ** (-np.arange(n, dtype=np.float32) / n)).astype(np.float64)
    ang = np.concatenate([row[:, None] * inv, col[:, None] * inv], axis=-1)
    cos = np.cos(ang.astype(np.float32).astype(np.float64))
    sin = np.sin(ang.astype(np.float32).astype(np.float64))
    cos2 = np.concatenate([cos, cos], axis=-1).astype(np.float32)
    sin2 = np.concatenate([-sin, sin], axis=-1).astype(np.float32)
    return jnp.asarray(cos2), jnp.asarray(sin2)


def kernel(x, c, ctx, c_ctx, w_ada, b_ada, norm1_w, norm2_w, w_in, q_norm_w, k_norm_w,
           hy_conv_w, hy_conv_b, hy_pe_w1, hy_pe_b1, hy_freq1, hy_pe_w2, hy_pe_b2, hy_freq2,
           hy_pe_w3, hy_skip, w_att_proj, w_hy_proj, w_out, router_w, router_b,
           exp_w1, exp_w3, exp_w2, final_norm_w):
    B, S, D = x.shape
    C = ctx.shape[1]
    depth = w_ada.shape[0]
    n_lat = B * S
    n_ctx = B * C

    cos2, sin2 = _rope_tables(S)
    cos_id = jnp.ones((C, HEAD_DIM), F32)
    sin_id = jnp.zeros((C, HEAD_DIM), F32)
    dft_xf = _dft_matrices(S, False, S // 2)
    dft_cf = _dft_matrices(C, False, C // 2)
    dft_x = _dft_matrices(S // 2, True)
    dft_c = _dft_matrices(C // 2, True)

    mod_rows = 16
    cs = jnp.zeros((mod_rows, D), F32).at[:B].set(c).at[B].set(c_ctx)
    mods = _adaln(cs, w_ada, b_ada)

    r_hi = router_w.T.astype(BF16)
    r_lo = (router_w.T - r_hi.astype(F32)).astype(BF16)
    fw = final_norm_w.reshape(1, D)

    x2d = x.reshape(n_lat, D)
    c2d = ctx.reshape(n_ctx, D)
    for i in range(depth):
        last = i == depth - 1
        m_lat = [mods[i, :B, j * D:(j + 1) * D].reshape(B, 1, D) for j in range(6)]
        m_ctx = [mods[i, B:B + 1, j * D:(j + 1) * D].reshape(1, 1, D) for j in range(6)]
        wb = w_in[i].astype(BF16)
        n1 = norm1_w[i].reshape(1, D)
        n2 = norm2_w[i].reshape(1, D)
        qn = q_norm_w[i].reshape(1, HEAD_DIM)
        kn = k_norm_w[i].reshape(1, HEAD_DIM)
        wa = w_att_proj[i].astype(BF16)
        wh = w_hy_proj[i].astype(BF16)
        wo = w_out[i].astype(BF16)

        q, k, v, u, g = _inproj(x2d, S, m_lat[0], m_lat[1], n1, wb, qn, kn, cos2, sin2,
                                ("q", "k", "v", "u", "g"))
        if last:
            kc, vc = _inproj(c2d, C, m_ctx[0], m_ctx[1], n1, wb, qn, kn, cos_id, sin_id, ("k", "v"))
        else:
            qc, kc, vc, uc, gc = _inproj(c2d, C, m_ctx[0], m_ctx[1], n1, wb, qn, kn, cos_id, sin_id,
                                         ("q", "k", "v", "u", "g"))
        ya = _attention(q, [(k, v, S), (kc, vc, C)], B, S)

        filt = (hy_pe_w1[i], hy_pe_b1[i], hy_freq1[i], hy_pe_w2[i], hy_pe_b2[i], hy_freq2[i],
                hy_pe_w3[i])
        kpack = _hy_spectrum(dft_xf[0], dft_xf[1], *_hy_filter(S, *filt))
        yh = _hyena(u, B, S, dft_x, kpack, hy_conv_w[i], hy_conv_b[i], hy_skip[i])

        n_tok = n_lat if last else n_lat + n_ctx
        x2d, hx, logits = _merge(ya, yh, g, x2d, S, m_lat[2], m_lat[3], m_lat[4], n2,
                                 wa, wh, wo, r_hi, r_lo, n_tok, 0)
        if not last:
            ya_c = _attention(qc, [(kc, vc, C)], B, C)
            kpack = _hy_spectrum(dft_cf[0], dft_cf[1], *_hy_filter(C, *filt))
            yh_c = _hyena(uc, B, C, dft_c, kpack, hy_conv_w[i], hy_conv_b[i], hy_skip[i])
            c2d, hx, logits = _merge(ya_c, yh_c, gc, c2d, C, m_ctx[2], m_ctx[3], m_ctx[4], n2,
                                     wa, wh, wo, r_hi, r_lo, n_tok, n_lat, prev=(hx, logits))

        idx, gate, rank, counts = _route(logits, router_b)
        n_blocks = -(-(n_tok * TOP_K) // MOE_ROWS) + N_EXPERTS
        pad_start, pad_lo, pad_hi, tables = _layout(counts, n_blocks)
        dest = _slots(idx, rank, pad_start).reshape(TOP_K * n_tok)
        xs = _scatter_rows(dest, pad_lo, pad_hi, hx, n_blocks)
        y = _experts(tables, xs, exp_w1, exp_w3, exp_w2, i)
        gate_t = gate.T
        x2d = _combine(dest, x2d, S, gate_t, m_lat[5], fw, y, 0, last)
        if not last:
            c2d = _combine(dest, c2d, C, gate_t, m_ctx[5], fw, y, n_lat, False)
    return x2d.reshape(B, S, D)
```

```python
import functools
import math

import numpy as np
import jax
import jax.numpy as jnp
from jax import lax
from jax.experimental import pallas as pl
from jax.experimental.pallas import tpu as pltpu

F32 = jnp.float32
BF16 = jnp.bfloat16

D_MODEL = 1024
DEPTH = 2
GRID_W = 64
NORM_EPS = 1e-6
N_HEADS = 8
N_KV_HEADS = 2
HEAD_DIM = 128
ATT_WIDTH = N_HEADS * HEAD_DIM
KV_WIDTH = N_KV_HEADS * HEAD_DIM
ROPE_THETA = 10000.0
ATT_SCALE = HEAD_DIM ** -0.5
LOG2_E = math.log2(math.e)
HY_WIDTH = D_MODEL // 2
HY_ORDER = 2
HY_FILTER_HIDDEN = 64
HY_BANDS = 16
HY_PE_DIM = 1 + 2 * HY_BANDS
HY_PE_PAD = 128
HY_FAST_DECAY = 0.3
HY_SLOW_DECAY = 1.5
HY_DECAY_TARGET = 1e-2
N_EXPERTS = 64
N_GROUPS = 8
EXPERTS_PER_GROUP = N_EXPERTS // N_GROUPS
TOP_K = 2
EXPERT_DIM = 512
IN_WIDTH = ATT_WIDTH + 2 * KV_WIDTH + 3 * HY_WIDTH + 2 * D_MODEL
COL_Q = 0
COL_K = ATT_WIDTH
COL_V = ATT_WIDTH + KV_WIDTH
COL_U = ATT_WIDTH + 2 * KV_WIDTH
COL_G = COL_U + 3 * HY_WIDTH

MXU_COLS = 256
SUBLANES = 8
LANES = 128
ROW_TILE = 512
ATT_Q_TILE = 512
ATT_KEY_CHUNK = 256
DFT_ROWS = 64
HY_CHAINS = 4
MOE_ROWS = 256
VMEM_LIMIT = 56 * 1024 * 1024


def _dot(a, b):
    return jnp.dot(a, b, preferred_element_type=F32)


def _split(a):
    hi = a.astype(BF16)
    lo = (a - hi.astype(F32)).astype(BF16)
    return hi, lo


def _dot3(a, b):
    ah, al = _split(a)
    bh, bl = _split(b)
    return _dot(ah, bh) + (_dot(al, bh) + _dot(ah, bl))


def _tiles_to_rows(t):
    c = pltpu.einshape("tjl->jtl", t)
    return jnp.concatenate([c[j] for j in range(c.shape[0])], axis=1)


def _rows_to_tiles(x):
    c = jnp.stack([x[:, j * LANES:(j + 1) * LANES] for j in range(x.shape[1] // LANES)], axis=0)
    return pltpu.einshape("jtl->tjl", c)


def _rms(t):
    return t * lax.rsqrt(jnp.mean(t * t, axis=-1, keepdims=True) + NORM_EPS)


def _params(*sem):
    return pltpu.CompilerParams(dimension_semantics=sem, vmem_limit_bytes=VMEM_LIMIT)


def _adaln_kernel(c_ref, w_ref, b_ref, o_ref):
    c = c_ref[...]
    o_ref[...] = _dot3(c * jax.nn.sigmoid(c), w_ref[...]) + b_ref[...]


def _adaln(cs, w_ada, b_ada):
    depth, d, n = w_ada.shape
    rows = cs.shape[0]
    tn = 1536
    return pl.pallas_call(
        _adaln_kernel,
        grid=(depth, n // tn),
        in_specs=[pl.BlockSpec((rows, d), lambda l, j: (0, 0)),
                  pl.BlockSpec((None, d, tn), lambda l, j: (l, 0, j)),
                  pl.BlockSpec((None, 1, tn), lambda l, j: (l, 0, j))],
        out_specs=pl.BlockSpec((None, rows, tn), lambda l, j: (l, 0, j)),
        out_shape=jax.ShapeDtypeStruct((depth, rows, n), F32),
        compiler_params=_params("parallel", "parallel"),
        name="adaln",
    )(cs, w_ada, b_ada.reshape(depth, 1, n))


def _inproj_kernel(x_ref, sh_ref, sc_ref, nw_ref, w_ref, qn_ref, kn_ref, cos_ref, sin_ref,
                   *outs, sections):
    h = _rms(x_ref[...]) * nw_ref[...]
    h = h * (1.0 + sc_ref[...]) + sh_ref[...]
    hb = h.astype(BF16)
    cos = cos_ref[...]
    sin = sin_ref[...]
    o = dict(zip(sections, outs))

    def head_cols(col0, n_cols, norm_w, scale, out_ref):
        for c in range(n_cols // MXU_COLS):
            acc = _dot(hb, w_ref[:, col0 + c * MXU_COLS:col0 + (c + 1) * MXU_COLS])
            for j in range(MXU_COLS // HEAD_DIM):
                t = _rms(acc[:, j * HEAD_DIM:(j + 1) * HEAD_DIM]) * norm_w
                t = t * cos + pltpu.roll(t, HEAD_DIM // 2, 1) * sin
                lo = c * MXU_COLS + j * HEAD_DIM
                out_ref[:, lo:lo + HEAD_DIM] = (t * scale).astype(out_ref.dtype)

    if "q" in o:
        head_cols(COL_Q, ATT_WIDTH, qn_ref[...], ATT_SCALE * LOG2_E, o["q"])
    if "k" in o:
        head_cols(COL_K, KV_WIDTH, kn_ref[...], 1.0, o["k"])
    if "v" in o:
        acc = _dot(hb, w_ref[:, COL_V:COL_V + KV_WIDTH]).astype(BF16)
        ones = jnp.ones((acc.shape[0], HEAD_DIM), BF16)
        for j in range(N_KV_HEADS):
            o["v"][:, 2 * j * HEAD_DIM:(2 * j + 1) * HEAD_DIM] = acc[:, j * HEAD_DIM:(j + 1) * HEAD_DIM]
            o["v"][:, (2 * j + 1) * HEAD_DIM:(2 * j + 2) * HEAD_DIM] = ones
    if "u" in o:
        for c in range(3):
            o["u"][:, c * HY_WIDTH:(c + 1) * HY_WIDTH] = _dot(
                hb, w_ref[:, COL_U + c * HY_WIDTH:COL_U + (c + 1) * HY_WIDTH]).astype(BF16)
    if "g" in o:
        for c in range(4):
            acc = _dot(hb, w_ref[:, COL_G + c * 512:COL_G + (c + 1) * 512])
            o["g"][:, c * 512:(c + 1) * 512] = jax.nn.sigmoid(acc).astype(BF16)


_SECTION_SHAPES = {"q": (ATT_WIDTH, BF16), "k": (KV_WIDTH, BF16), "v": (2 * KV_WIDTH, BF16),
                   "u": (3 * HY_WIDTH, BF16), "g": (2 * D_MODEL, BF16)}


def _inproj(x2d, seq_len, sh, sc, nw, w_bf16, qn, kn, cos2, sin2, sections):
    rows, d = x2d.shape
    tm = min(ROW_TILE, seq_len)
    tiles_per_seq = seq_len // tm
    nb = sh.shape[0]
    mod_idx = (lambda i: (i // tiles_per_seq, 0, 0)) if nb > 1 else (lambda i: (0, 0, 0))
    const2 = lambda i: (0, 0)
    out_shape = [jax.ShapeDtypeStruct((rows, _SECTION_SHAPES[s][0]), _SECTION_SHAPES[s][1])
                 for s in sections]
    out_specs = [pl.BlockSpec((tm, _SECTION_SHAPES[s][0]), lambda i: (i, 0)) for s in sections]
    return pl.pallas_call(
        functools.partial(_inproj_kernel, sections=tuple(sections)),
        grid=(rows // tm,),
        in_specs=[pl.BlockSpec((tm, d), lambda i: (i, 0)),
                  pl.BlockSpec((None, 1, d), mod_idx),
                  pl.BlockSpec((None, 1, d), mod_idx),
                  pl.BlockSpec((1, d), const2),
                  pl.BlockSpec((d, IN_WIDTH), const2),
                  pl.BlockSpec((1, HEAD_DIM), const2),
                  pl.BlockSpec((1, HEAD_DIM), const2),
                  pl.BlockSpec((tm, HEAD_DIM), lambda i: (i % tiles_per_seq, 0)),
                  pl.BlockSpec((tm, HEAD_DIM), lambda i: (i % tiles_per_seq, 0))],
        out_specs=out_specs,
        out_shape=out_shape,
        compiler_params=_params("parallel"),
        name="inproj",
    )(x2d, sh, sc, nw, w_bf16, qn, kn, cos2, sin2)


def _attn_kernel(q_ref, *refs, n_kv_sets):
    kv = refs[:2 * n_kv_sets]
    o_ref = refs[2 * n_kv_sets]
    nt = (((1,), (1,)), ((), ()))
    for g in range(N_HEADS // N_KV_HEADS):
        q = q_ref[:, g * HEAD_DIM:(g + 1) * HEAD_DIM]
        m = None
        acc = None
        for s_idx in range(n_kv_sets):
            k_ref, v_ref = kv[2 * s_idx], kv[2 * s_idx + 1]
            lk = k_ref.shape[0]
            for c0 in range(0, lk, ATT_KEY_CHUNK):
                c1 = min(c0 + ATT_KEY_CHUNK, lk)
                s = lax.dot_general(q, k_ref[c0:c1, :], nt, preferred_element_type=F32)
                mc = jnp.max(s, axis=1, keepdims=True)
                if m is None:
                    m = mc
                    acc = _dot(jnp.exp2(s - m).astype(BF16), v_ref[c0:c1, :])
                else:
                    m_new = jnp.maximum(m, mc)
                    acc = jnp.exp2(m - m_new) * acc + _dot(jnp.exp2(s - m_new).astype(BF16),
                                                           v_ref[c0:c1, :])
                    m = m_new
        o_ref[:, g * HEAD_DIM:(g + 1) * HEAD_DIM] = (
            acc[:, :HEAD_DIM] / acc[:, HEAD_DIM:]).astype(o_ref.dtype)


def _attention(q, kv_sets, batch, seq_len):
    tq = min(ATT_Q_TILE, seq_len)
    nq = seq_len // tq
    grp = (N_HEADS // N_KV_HEADS) * HEAD_DIM
    in_specs = [pl.BlockSpec((tq, grp), lambda b, j, i: (b * nq + i, j))]
    args = [q]
    for k, v, lk in kv_sets:
        in_specs += [pl.BlockSpec((lk, HEAD_DIM), lambda b, j, i: (b, j)),
                     pl.BlockSpec((lk, 2 * HEAD_DIM), lambda b, j, i: (b, j))]
        args += [k, v]
    return pl.pallas_call(
        functools.partial(_attn_kernel, n_kv_sets=len(kv_sets)),
        grid=(batch, N_KV_HEADS, nq),
        in_specs=in_specs,
        out_specs=pl.BlockSpec((tq, grp), lambda b, j, i: (b * nq + i, j)),
        out_shape=jax.ShapeDtypeStruct(q.shape, BF16),
        compiler_params=_params("parallel", "parallel", "arbitrary"),
        name="attention",
    )(*args)


def _hy_filter_kernel(z_ref, t_ref, w1_ref, b1_ref, f1_ref, w2_ref, b2_ref, f2_ref, w3_ref,
                      dl_ref, sp_ref, sm_ref, spa_ref, sma_ref, mid_ref):
    i = pl.program_id(0)
    tl = z_ref.shape[0]
    h = jnp.sin(f1_ref[...] * (_dot3(z_ref[...], w1_ref[...]) + b1_ref[...]))
    h = jnp.sin(f2_ref[...] * (_dot3(h, w2_ref[...]) + b2_ref[...]))
    h = _dot3(h, w3_ref[...])
    window = jnp.exp(-t_ref[...] * dl_ref[...])
    row = i * tl + lax.broadcasted_iota(jnp.int32, (tl, HY_WIDTH), 0)
    alt = (1 - 2 * (row & 1)).astype(F32)
    quarter = row & 3
    cos4 = jnp.where(quarter == 0, 1.0, jnp.where(quarter == 2, -1.0, 0.0))
    sin4 = jnp.where(quarter == 1, 1.0, jnp.where(quarter == 3, -1.0, 0.0))

    @pl.when(i == 0)
    def _():
        mid_ref[...] = jnp.zeros_like(mid_ref)

    for o in range(HY_ORDER):
        cols = slice(o * HY_WIDTH, (o + 1) * HY_WIDTH)
        hf = h[:, (2 * o) * HY_WIDTH:(2 * o + 1) * HY_WIDTH] * window
        hb = h[:, (2 * o + 1) * HY_WIDTH:(2 * o + 2) * HY_WIDTH] * window
        hb = jnp.where(row == 0, 0.0, hb)
        plus = hf + hb
        minus = hf - hb
        sp_ref[:, cols] = plus.astype(BF16)
        sm_ref[:, cols] = minus.astype(BF16)
        spa_ref[:, cols] = (alt * plus).astype(BF16)
        sma_ref[:, cols] = (alt * minus).astype(BF16)
        mid_ref[0:1, cols] += jnp.sum(cos4 * plus, axis=0, keepdims=True)
        mid_ref[1:2, cols] += jnp.sum(sin4 * minus, axis=0, keepdims=True)


def _hy_filter(seq_len, pe_w1, pe_b1, freq1, pe_w2, pe_b2, freq2, pe_w3):
    t01 = np.linspace(0.0, 1.0, seq_len)[:, None]
    pos = np.arange(seq_len, dtype=np.float64)[:, None]
    bands = np.linspace(1e-4, HY_BANDS - 1, HY_BANDS)[None, :]
    f = 2.0 * math.pi * pos * bands / seq_len
    z = np.zeros((seq_len, HY_PE_PAD), np.float32)
    z[:, :HY_PE_DIM] = np.concatenate([t01, np.cos(f), -np.sin(f)], axis=-1)
    max_decay = math.log(HY_DECAY_TARGET) / HY_FAST_DECAY
    min_decay = math.log(HY_DECAY_TARGET) / HY_SLOW_DECAY
    deltas = np.abs(np.linspace(min_decay, max_decay, HY_WIDTH))[None, :].astype(np.float32)
    w1p = jnp.zeros((HY_PE_PAD, HY_FILTER_HIDDEN), F32).at[:HY_PE_DIM].set(pe_w1)
    tl = min(512, seq_len)
    hid = HY_FILTER_HIDDEN
    n_out = HY_ORDER * HY_WIDTH
    c2 = lambda i: (0, 0)
    return pl.pallas_call(
        _hy_filter_kernel,
        grid=(seq_len // tl,),
        in_specs=[pl.BlockSpec((tl, HY_PE_PAD), lambda i: (i, 0)),
                  pl.BlockSpec((tl, 1), lambda i: (i, 0)),
                  pl.BlockSpec((HY_PE_PAD, hid), c2), pl.BlockSpec((1, hid), c2),
                  pl.BlockSpec((1, hid), c2), pl.BlockSpec((hid, hid), c2),
                  pl.BlockSpec((1, hid), c2), pl.BlockSpec((1, hid), c2),
                  pl.BlockSpec((hid, 2 * n_out), c2), pl.BlockSpec((1, HY_WIDTH), c2)],
        out_specs=[pl.BlockSpec((tl, n_out), lambda i: (i, 0))] * 4
                  + [pl.BlockSpec((SUBLANES, n_out), c2)],
        out_shape=[jax.ShapeDtypeStruct((seq_len, n_out), BF16)] * 4
                  + [jax.ShapeDtypeStruct((SUBLANES, n_out), F32)],
        compiler_params=_params("arbitrary"),
        name="hy_filter",
    )(jnp.asarray(z), jnp.asarray(t01.astype(np.float32)), w1p, pe_b1.reshape(1, hid),
      freq1.reshape(1, hid), pe_w2, pe_b2.reshape(1, hid), freq2.reshape(1, hid), pe_w3,
      jnp.asarray(deltas))


def _dft_kernel(ca_ref, sa_ref, cb_ref, sb_ref, alt_ref, wc_ref, cm_ref, sm_ref, *inverse_refs):
    i = pl.program_id(0)
    ca = ca_ref[...]
    sa = sa_ref[...]
    cb = cb_ref[...]
    sb = sb_ref[...]
    c = ca * cb - sa * sb
    s = sa * cb + ca * sb
    rows = lax.broadcasted_iota(jnp.int32, c.shape, 0)
    cols = lax.broadcasted_iota(jnp.int32, c.shape, 1)
    wc = wc_ref[...]
    cm_ref[...] = c.astype(BF16)
    sm_ref[...] = jnp.where((rows == 0) & (i == 0), alt_ref[...], s).astype(BF16)
    if inverse_refs:
        ci_ref, si_ref = inverse_refs
        alt_t = (1 - 2 * (rows & 1)).astype(F32)
        ci_ref[...] = (c * wc).astype(BF16)
        si_ref[...] = (jnp.where(cols == 0, alt_t, s) * wc).astype(BF16)


def _dft_matrices(seq_len, inverse, n_rows=None):
    n = 2 * seq_len
    n_rows = seq_len if n_rows is None else n_rows
    idx = np.arange(seq_len, dtype=np.int64)[None, :]
    r1 = np.arange(n_rows // DFT_ROWS, dtype=np.int64)[:, None] * DFT_ROWS
    r0 = np.arange(DFT_ROWS, dtype=np.int64)[:, None]
    ang_a = ((r1 * idx) % n).astype(np.float64) * (2.0 * math.pi / n)
    ang_b = ((r0 * idx) % n).astype(np.float64) * (2.0 * math.pi / n)
    tab = lambda a: jnp.asarray(a.astype(np.float32))
    ca = tab(np.cos(ang_a)).reshape(-1, 1, seq_len)
    sa = tab(np.sin(ang_a)).reshape(-1, 1, seq_len)
    alt = tab(1.0 - 2.0 * (idx % 2))
    wc = tab(np.where(idx == 0, 1.0, 2.0) / n)
    row_blk = pl.BlockSpec((None, 1, seq_len), lambda i: (i, 0, 0))
    full = lambda r: pl.BlockSpec((r, seq_len), lambda i: (0, 0))
    out_blk = pl.BlockSpec((DFT_ROWS, seq_len), lambda i: (i, 0))
    n_out = 4 if inverse else 2
    mats = pl.pallas_call(
        _dft_kernel,
        grid=(n_rows // DFT_ROWS,),
        in_specs=[row_blk, row_blk, full(DFT_ROWS), full(DFT_ROWS), full(1), full(1)],
        out_specs=[out_blk] * n_out,
        out_shape=[jax.ShapeDtypeStruct((n_rows, seq_len), BF16)] * n_out,
        compiler_params=_params("parallel"),
        name="dft_matrices",
    )(ca, sa, tab(np.cos(ang_b)), tab(np.sin(ang_b)), alt, wc)
    if not inverse:
        return tuple(mats)
    tw = np.arange(seq_len, dtype=np.float64)[:, None] * (2.0 * math.pi / (2 * n))
    rep = lambda a: tab(np.broadcast_to(a, (seq_len, LANES)))
    return tuple(mats) + (rep(np.cos(tw)), rep(np.sin(tw)))


def _hy_spec_kernel(cm_ref, sm_ref, sp_ref, smn_ref, spa_ref, sma_ref, mid_ref,
                    klr_ref, kli_ref, khr_ref, khi_ref):
    i = pl.program_id(1)
    cm = cm_ref[...]
    sm = sm_ref[...]
    klr_ref[...] = _dot(cm, sp_ref[...])
    khr_ref[...] = _dot(cm, spa_ref[...])
    lo_q = _dot(sm, smn_ref[...])
    hi_q = _dot(sm, sma_ref[...])
    rows = lax.broadcasted_iota(jnp.int32, lo_q.shape, 0)
    edge = (rows == 0) & (i == 0)
    kli_ref[...] = jnp.where(edge, mid_ref[0:1, :], -lo_q)
    khi_ref[...] = jnp.where(edge, -mid_ref[1:2, :], hi_q)


def _hy_spectrum(cm, sm, splus, sminus, splus_alt, sminus_alt, mid):
    half, seq_len = cm.shape
    tf = min(256, half)
    mat = pl.BlockSpec((tf, seq_len), lambda o, i: (i, 0))
    sig = pl.BlockSpec((seq_len, HY_WIDTH), lambda o, i: (0, o))
    out = pl.BlockSpec((tf, HY_WIDTH), lambda o, i: (i, o))
    return pl.pallas_call(
        _hy_spec_kernel,
        grid=(HY_ORDER, half // tf),
        in_specs=[mat, mat, sig, sig, sig, sig,
                  pl.BlockSpec((SUBLANES, HY_WIDTH), lambda o, i: (0, o))],
        out_specs=[out] * 4,
        out_shape=[jax.ShapeDtypeStruct((half, HY_ORDER * HY_WIDTH), F32)] * 4,
        compiler_params=_params("parallel", "arbitrary"),
        name="hy_spectrum",
    )(cm, sm, splus, sminus, splus_alt, sminus_alt, mid)


def _hy_sconv_kernel(u_ref, w_ref, b_ref, ue_ref, uo_ref):
    half = u_ref.shape[0] // 2
    words = pltpu.bitcast(u_ref[...], jnp.uint32)
    xe = lax.bitcast_convert_type(words << 16, F32)
    xo = lax.bitcast_convert_type(words & jnp.uint32(0xFFFF0000), F32)
    rows = lax.broadcasted_iota(jnp.int32, xe.shape, 0)
    xo_prev = jnp.where(rows == 0, 0.0, pltpu.roll(xo, 1, 0))
    xe_next = jnp.where(rows == half - 1, 0.0, pltpu.roll(xe, half - 1, 0))
    w0, w1, w2 = w_ref[0:1, :], w_ref[1:2, :], w_ref[2:3, :]
    ye = xo_prev * w0 + xe * w1 + xo * w2 + b_ref[...]
    yo = xe * w0 + xo * w1 + xe_next * w2 + b_ref[...]
    ue_ref[...] = ye.astype(ue_ref.dtype)
    uo_ref[...] = yo.astype(uo_ref.dtype)


def _hy_sconv(u, batch, seq_len, conv_w, conv_b):
    ch = u.shape[1]
    tc = 256
    half = seq_len // 2
    u3 = u.reshape(batch, seq_len, ch)
    plane = pl.BlockSpec((None, half, tc), lambda b, j: (b, 0, j))
    ue, uo = pl.pallas_call(
        _hy_sconv_kernel,
        grid=(batch, ch // tc),
        in_specs=[pl.BlockSpec((None, seq_len, tc), lambda b, j: (b, 0, j)),
                  pl.BlockSpec((3, tc), lambda b, j: (0, j)),
                  pl.BlockSpec((1, tc), lambda b, j: (0, j))],
        out_specs=[plane, plane],
        out_shape=[jax.ShapeDtypeStruct((batch, half, ch), BF16)] * 2,
        compiler_params=_params("parallel", "parallel"),
        name="hy_sconv",
    )(u3, conv_w, conv_b.reshape(1, ch))
    return ue, uo


def _hy_fwd_kernel(cm_ref, sm_ref, tc_ref, ts_ref, ve_ref, vo_ref, klr_ref, kli_ref, khr_ref,
                   khi_ref, zp_ref, zq_ref):
    i = pl.program_id(0)
    w = ve_ref.shape[1]
    wc = w // HY_CHAINS
    reps = wc // LANES
    tc = jnp.concatenate([tc_ref[...]] * reps, axis=1)
    ts = jnp.concatenate([ts_ref[...]] * reps, axis=1)
    rows = lax.broadcasted_iota(jnp.int32, (cm_ref.shape[0], wc), 0)
    edge = (rows == 0) & (i == 0)
    cm = cm_ref[...]
    sm = sm_ref[...]
    for c in range(HY_CHAINS):
        lo, hi = c * wc, (c + 1) * wc
        v = jnp.concatenate([ve_ref[:, lo:hi], vo_ref[:, lo:hi]], axis=1)
        p = _dot(cm, v)
        q = _dot(sm, v)
        pe, po, qe, qo = p[:, :wc], p[:, wc:], q[:, :wc], q[:, wc:]

        a = tc * po - ts * qo
        b = tc * qo + ts * po
        xlr = pe + a
        xhr = pe - a
        xli = jnp.where(edge, qe, -(qe + b))
        xhi = jnp.where(edge, -qo, qe - b)

        klr, kli, khr, khi = klr_ref[:, lo:hi], kli_ref[:, lo:hi], khr_ref[:, lo:hi], khi_ref[:, lo:hi]
        ylr = xlr * klr - jnp.where(edge, 0.0, xli * kli)
        yhr = xhr * khr - jnp.where(edge, 0.0, xhi * khi)
        yli = jnp.where(edge, xli * kli - xhi * khi, xlr * kli + xli * klr)
        yhi = jnp.where(edge, xli * khi + xhi * kli, xhr * khi + xhi * khr)

        dr = ylr - yhr
        di = yli + yhi
        zp_ref[:, lo:hi] = (0.5 * (ylr + yhr)).astype(BF16)
        zp_ref[:, w + lo:w + hi] = (0.5 * (tc * dr - ts * di)).astype(BF16)
        zq_ref[:, lo:hi] = jnp.where(edge, yli, -0.5 * (yli - yhi)).astype(BF16)
        zq_ref[:, w + lo:w + hi] = jnp.where(edge, -yhi, -0.5 * (tc * di + ts * dr)).astype(BF16)


def _hy_forward(dfth, v_planes, v_blk, kpack, order):
    cm, sm, _, _, tc, ts = dfth
    batch, half, _ = v_planes[0].shape
    w = HY_WIDTH
    tf = min(512, half)
    mat = pl.BlockSpec((tf, half), lambda i, b: (i, 0))
    tw = pl.BlockSpec((tf, LANES), lambda i, b: (i, 0))
    spec = pl.BlockSpec((tf, w), lambda i, b: (i, order))
    out = pl.BlockSpec((None, tf, 2 * w), lambda i, b: (b, i, 0))
    return pl.pallas_call(
        _hy_fwd_kernel,
        grid=(half // tf, batch),
        in_specs=[mat, mat, tw, tw,
                  pl.BlockSpec((None, half, w), lambda i, b: (b, 0, v_blk)),
                  pl.BlockSpec((None, half, w), lambda i, b: (b, 0, v_blk)),
                  spec, spec, spec, spec],
        out_specs=[out, out],
        out_shape=[jax.ShapeDtypeStruct((batch, half, 2 * w), BF16)] * 2,
        compiler_params=_params("parallel", "arbitrary"),
        name="hy_forward",
    )(cm, sm, tc, ts, v_planes[0], v_planes[1], *kpack)


def _hy_inv_kernel(ci_ref, si_ref, zp_ref, zq_ref, ae_ref, ao_ref, ce_ref, co_ref, skip_ref, *outs,
                   interleave):
    w = ae_ref.shape[1]
    tt = ae_ref.shape[0]
    y = _dot(ci_ref[...], zp_ref[...]) + _dot(si_ref[...], zq_ref[...])
    skip = skip_ref[...]
    ze = ae_ref[...].astype(F32) * (y[:, :w] + skip * ce_ref[...].astype(F32))
    zo = ao_ref[...].astype(F32) * (y[:, w:] + skip * co_ref[...].astype(F32))
    if interleave:
        (o_ref,) = outs
        bits = lambda t: lax.bitcast_convert_type(t.astype(BF16).astype(F32), jnp.uint32)
        words = (bits(ze) >> 16) | (bits(zo) & jnp.uint32(0xFFFF0000))
        o_ref[...] = pltpu.bitcast(words, BF16)
    else:
        for k in range(len(outs) // 2):
            outs[2 * k][...] = ze.astype(outs[2 * k].dtype)
            outs[2 * k + 1][...] = zo.astype(outs[2 * k + 1].dtype)


def _hy_inverse(dfth, zp, zq, a_planes, a_blk, c_planes, c_blk, skip, out_dtypes, interleave):
    _, _, ci, si, _, _ = dfth
    batch, half, w2 = zp.shape
    w = w2 // 2
    tt = min(512, half)
    mat = pl.BlockSpec((tt, half), lambda i, b: (i, 0))
    sig = pl.BlockSpec((None, half, w2), lambda i, b: (b, 0, 0))
    a_spec = pl.BlockSpec((None, tt, w), lambda i, b: (b, i, a_blk))
    c_spec = pl.BlockSpec((None, tt, w), lambda i, b: (b, i, c_blk))
    plane = pl.BlockSpec((None, tt, w), lambda i, b: (b, i, 0))
    if interleave:
        (dt,) = out_dtypes
        out_specs = [pl.BlockSpec((None, 2 * tt, w), lambda i, b: (b, i, 0))]
        out_shape = [jax.ShapeDtypeStruct((batch, 2 * half, w), dt)]
    else:
        out_specs = [plane] * (2 * len(out_dtypes))
        out_shape = [jax.ShapeDtypeStruct((batch, half, w), dt) for dt in out_dtypes for _ in (0, 1)]
    outs = pl.pallas_call(
        functools.partial(_hy_inv_kernel, interleave=interleave),
        grid=(half // tt, batch),
        in_specs=[mat, mat, sig, sig, a_spec, a_spec, c_spec, c_spec,
                  pl.BlockSpec((1, w), lambda i, b: (0, 0))],
        out_specs=out_specs,
        out_shape=out_shape,
        compiler_params=_params("parallel", "arbitrary"),
        name="hy_inverse",
    )(ci, si, zp, zq, a_planes[0], a_planes[1], c_planes[0], c_planes[1], skip)
    if interleave:
        return outs[0]
    return [(outs[2 * k], outs[2 * k + 1]) for k in range(len(out_dtypes))]


def _hyena(u, batch, seq_len, dfth, kpack, conv_w, conv_b, skip):
    uc = _hy_sconv(u, batch, seq_len, conv_w, conv_b)
    zp, zq = _hy_forward(dfth, uc, 2, kpack, 0)
    z, zb = _hy_inverse(dfth, zp, zq, uc, 0, uc, 2, skip[0:1], (F32, BF16), False)
    zp, zq = _hy_forward(dfth, zb, 0, kpack, 1)
    y = _hy_inverse(dfth, zp, zq, uc, 1, z, 0, skip[1:2], (BF16,), True)
    return y.reshape(batch * seq_len, HY_WIDTH)


def _merge_kernel(ya_ref, yh_ref, g_ref, x_ref, g1_ref, sh_ref, sc_ref, nw_ref,
                  wa_ref, wh_ref, wo_ref, rh_ref, rl_ref, *rest):
    xo_ref, hx_ref, lg_ref = rest[-3:]
    a = _dot(ya_ref[...], wa_ref[...])
    h = _dot(yh_ref[...], wh_ref[...])
    m = g_ref[:, :D_MODEL].astype(F32) * a + g_ref[:, D_MODEL:].astype(F32) * h
    xn = x_ref[...] + g1_ref[...] * _dot(m.astype(BF16), wo_ref[...])
    xo_ref[...] = xn
    hx = _rms(xn) * nw_ref[...]
    hx = hx * (1.0 + sc_ref[...]) + sh_ref[...]
    hx_ref[...] = _rows_to_tiles(hx)
    hh, hl = _split(hx)
    nt = (((1,), (1,)), ((), ()))
    dn = lambda a, b: lax.dot_general(a, b, nt, preferred_element_type=F32)
    lg_ref[...] = dn(rh_ref[...], hh) + (dn(rh_ref[...], hl) + dn(rl_ref[...], hh))


def _merge(ya, yh, g, x2d, seq_len, g1, sh2, sc2, nw, wa, wh, wo, r_hi, r_lo,
           total_rows, row_offset, prev=None):
    rows, d = x2d.shape
    tm = min(ROW_TILE, seq_len)
    tiles_per_seq = seq_len // tm
    off = row_offset // tm
    nb = g1.shape[0]
    mod_idx = (lambda i: (i // tiles_per_seq, 0, 0)) if nb > 1 else (lambda i: (0, 0, 0))
    c2 = lambda i: (0, 0)
    row = lambda w: pl.BlockSpec((tm, w), lambda i: (i, 0))
    in_specs = [row(ATT_WIDTH), row(HY_WIDTH), row(2 * d), row(d),
                pl.BlockSpec((None, 1, d), mod_idx), pl.BlockSpec((None, 1, d), mod_idx),
                pl.BlockSpec((None, 1, d), mod_idx), pl.BlockSpec((1, d), c2),
                pl.BlockSpec((ATT_WIDTH, d), c2), pl.BlockSpec((HY_WIDTH, d), c2),
                pl.BlockSpec((d, d), c2), pl.BlockSpec((N_EXPERTS, d), c2),
                pl.BlockSpec((N_EXPERTS, d), c2)]
    args = [ya, yh, g, x2d, g1, sh2, sc2, nw, wa, wh, wo, r_hi, r_lo]
    aliases = {}
    if prev is not None:
        in_specs += [pl.BlockSpec(memory_space=pl.ANY)] * 2
        aliases = {len(args): 1, len(args) + 1: 2}
        args += list(prev)
    return pl.pallas_call(
        _merge_kernel,
        grid=(rows // tm,),
        in_specs=in_specs,
        out_specs=[row(d), pl.BlockSpec((tm, d // LANES, LANES), lambda i: (i + off, 0, 0)),
                   pl.BlockSpec((N_EXPERTS, tm), lambda i: (0, i + off))],
        out_shape=[jax.ShapeDtypeStruct((rows, d), F32),
                   jax.ShapeDtypeStruct((total_rows, d // LANES, LANES), F32),
                   jax.ShapeDtypeStruct((N_EXPERTS, total_rows), F32)],
        input_output_aliases=aliases,
        compiler_params=_params("arbitrary"),
        name="merge",
    )(*args)


def _route_kernel(lg_ref, bias_ref, idx_ref, gate_ref, rank_ref, cnt_ref, run_ref):
    step = pl.program_id(0)
    tn = lg_ref.shape[2]
    shape = (N_GROUPS, EXPERTS_PER_GROUP, tn)
    s = jax.nn.sigmoid(lg_ref[...])
    b = s + bias_ref[...]
    mem = lax.broadcasted_iota(jnp.int32, shape, 1).astype(F32)
    grp = lax.broadcasted_iota(jnp.int32, (N_GROUPS, 1, tn), 0).astype(F32)
    big = float(N_EXPERTS)

    m1 = jnp.max(b, axis=1, keepdims=True)
    i1 = jnp.min(jnp.where(b == m1, mem, big), axis=1, keepdims=True)
    b2 = jnp.where(mem == i1, -jnp.inf, b)
    m2 = jnp.max(b2, axis=1, keepdims=True)
    i2 = jnp.min(jnp.where(b2 == m2, mem, big), axis=1, keepdims=True)
    gs = m1 + m2
    gmax = jnp.max(gs, axis=0, keepdims=True)
    gsel = jnp.min(jnp.where(gs == gmax, grp, big), axis=0, keepdims=True)
    selg = grp == gsel

    @pl.when(step == 0)
    def _():
        run_ref[...] = jnp.zeros_like(run_ref)

    tri = (lax.broadcasted_iota(jnp.int32, (tn, tn), 0)
           < lax.broadcasted_iota(jnp.int32, (tn, tn), 1)).astype(BF16)
    run = run_ref[...]
    ws = []
    for k, ik in enumerate((i1, i2)):
        hit = selg & (mem == ik)
        ws.append(jnp.sum(jnp.where(hit, s, 0.0), axis=(0, 1), keepdims=True))
        e_loc = jnp.sum(jnp.where(selg, ik, 0.0), axis=0, keepdims=True)
        idx_ref[k:k + 1, :] = (gsel * EXPERTS_PER_GROUP + e_loc).reshape(1, tn).astype(jnp.int32)
        oh = jnp.where(hit, 1.0, 0.0).reshape(N_EXPERTS, tn)
        before = run + _dot(oh.astype(BF16), tri)
        rank_ref[k:k + 1, :] = jnp.sum(oh * before, axis=0, keepdims=True).astype(jnp.int32)
        run = run + jnp.sum(oh, axis=1, keepdims=True)
    run_ref[...] = run
    cnt_ref[...] = run
    tot = ws[0] + ws[1]
    for k in range(TOP_K):
        gate_ref[k:k + 1, :] = (ws[k] / tot).reshape(1, tn)


def _route(logits_t, router_b):
    n_tok = logits_t.shape[1]
    tn = 1024
    lg3 = logits_t.reshape(N_GROUPS, EXPERTS_PER_GROUP, n_tok)
    row2 = pl.BlockSpec((TOP_K, tn), lambda i: (0, i))
    return pl.pallas_call(
        _route_kernel,
        grid=(n_tok // tn,),
        in_specs=[pl.BlockSpec((N_GROUPS, EXPERTS_PER_GROUP, tn), lambda i: (0, 0, i)),
                  pl.BlockSpec((N_GROUPS, EXPERTS_PER_GROUP, 1), lambda i: (0, 0, 0))],
        out_specs=[row2, row2, row2, pl.BlockSpec((N_EXPERTS, 1), lambda i: (0, 0))],
        out_shape=[jax.ShapeDtypeStruct((TOP_K, n_tok), jnp.int32),
                   jax.ShapeDtypeStruct((TOP_K, n_tok), F32),
                   jax.ShapeDtypeStruct((TOP_K, n_tok), jnp.int32),
                   jax.ShapeDtypeStruct((N_EXPERTS, 1), F32)],
        scratch_shapes=[pltpu.VMEM((N_EXPERTS, 1), F32)],
        compiler_params=_params("arbitrary"),
        name="moe_route",
    )(lg3, router_b.astype(F32).reshape(N_GROUPS, EXPERTS_PER_GROUP, 1))


def _slot_kernel(idx_ref, rank_ref, start_ref, dest_ref):
    tn = idx_ref.shape[1]
    e = lax.broadcasted_iota(jnp.int32, (N_EXPERTS, tn), 0)
    for k in range(TOP_K):
        base = jnp.sum(jnp.where(e == idx_ref[k:k + 1, :], start_ref[...], 0.0), axis=0,
                       keepdims=True)
        dest_ref[k:k + 1, :] = base.astype(jnp.int32) + rank_ref[k:k + 1, :]


def _slots(idx, rank, pad_start):
    n_tok = idx.shape[1]
    tn = min(2048, n_tok)
    row2 = pl.BlockSpec((TOP_K, tn), lambda i: (0, i))
    return pl.pallas_call(
        _slot_kernel,
        grid=(n_tok // tn,),
        in_specs=[row2, row2, pl.BlockSpec((N_EXPERTS, 1), lambda i: (0, 0))],
        out_specs=row2,
        out_shape=jax.ShapeDtypeStruct((TOP_K, n_tok), jnp.int32),
        compiler_params=_params("parallel"),
        name="moe_slots",
    )(idx, rank, pad_start.astype(F32).reshape(N_EXPERTS, 1))


def _layout(counts, n_blocks):
    counts = counts.reshape(N_EXPERTS).astype(jnp.int32)
    padded = (counts + MOE_ROWS - 1) // MOE_ROWS * MOE_ROWS
    pad_end = jnp.cumsum(padded)
    pad_start = pad_end - padded
    blk_row = jnp.arange(n_blocks, dtype=jnp.int32)[:, None] * MOE_ROWS
    block_e = jnp.minimum(jnp.sum((pad_end[None, :] <= blk_row).astype(jnp.int32), axis=1),
                          N_EXPERTS - 1)
    prev_e = jnp.concatenate([jnp.full((1,), -1, jnp.int32), block_e[:-1]])
    block_new = (block_e != prev_e).astype(jnp.int32)
    block_slot = (jnp.cumsum(block_new) - 1) & 1
    e_ids = jnp.arange(N_EXPERTS, dtype=jnp.int32)
    later = (e_ids[None, :] > e_ids[:, None]) & (counts[None, :] > 0)
    next_e = jnp.min(jnp.where(later, e_ids[None, :], N_EXPERTS), axis=1)
    next_e = jnp.where(next_e == N_EXPERTS, -1, next_e)
    block_next = jnp.sum(jnp.where(block_e[:, None] == e_ids[None, :], next_e[None, :], 0), axis=1)
    n_used = (pad_end[-1] // MOE_ROWS).astype(jnp.int32).reshape(1)
    tables = (block_e, block_new, block_slot.astype(jnp.int32), block_next.astype(jnp.int32), n_used)
    return pad_start, pad_start + counts, pad_end, tables


_PAD_CHUNKS = tuple(2 ** p for p in range(int(math.log2(MOE_ROWS)) - 1, -1, -1))


def _scatter_kernel(dest_ref, lo_ref, hi_ref, x_ref, o_hbm, zeros, sem, zsem, *, n_tok):
    i = pl.program_id(0)
    tm = x_ref.shape[0]

    def pad_copies(act):
        def per_expert(e, carry):
            off = lo_ref[e]
            n = hi_ref[e] - off
            for c in _PAD_CHUNKS:
                @pl.when((n & c) != 0)
                def _():
                    act(pltpu.make_async_copy(zeros.at[pl.ds(0, c)], o_hbm.at[pl.ds(off, c)], zsem))
                off = off + (n & c)
            return carry
        lax.fori_loop(0, N_EXPERTS, per_expert, 0)

    @pl.when(i == 0)
    def _():
        zeros[...] = jnp.zeros_like(zeros)
        pad_copies(lambda cp: cp.start())

    def issue(r, carry):
        for k in range(TOP_K):
            pltpu.make_async_copy(x_ref.at[r], o_hbm.at[dest_ref[k * n_tok + i * tm + r]],
                                  sem).start(priority=k)
        return carry
    lax.fori_loop(0, tm, issue, 0, unroll=8)
    for k in range(TOP_K):
        pltpu.make_async_copy(x_ref, o_hbm.at[pl.ds(0, tm)], sem).wait()

    @pl.when(i == 0)
    def _():
        pad_copies(lambda cp: cp.wait())


def _scatter_rows(dest_flat, pad_lo, pad_hi, h, n_blocks):
    n_tok = h.shape[0]
    tile = h.shape[1:]
    tm = 256
    return pl.pallas_call(
        functools.partial(_scatter_kernel, n_tok=n_tok),
        grid_spec=pltpu.PrefetchScalarGridSpec(
            num_scalar_prefetch=3,
            grid=(n_tok // tm,),
            in_specs=[pl.BlockSpec((tm,) + tile, lambda i, *_: (i, 0, 0))],
            out_specs=pl.BlockSpec(memory_space=pl.ANY),
            scratch_shapes=[pltpu.VMEM((MOE_ROWS // 2,) + tile, h.dtype),
                            pltpu.SemaphoreType.DMA(()), pltpu.SemaphoreType.DMA(())]),
        out_shape=jax.ShapeDtypeStruct((n_blocks * MOE_ROWS,) + tile, h.dtype),
        compiler_params=pltpu.CompilerParams(dimension_semantics=("arbitrary",),
                                             vmem_limit_bytes=VMEM_LIMIT, has_side_effects=True),
        name="moe_scatter",
    )(dest_flat, pad_lo, pad_hi, h)


def _expert_kernel(be_ref, new_ref, slot_ref, next_ref, nused_ref, x_ref, w1_hbm, w3_hbm, w2_hbm,
                   y_ref, w1f, w3f, w2f, w1b, w3b, w2b, sems, *, layer):
    i = pl.program_id(0)
    hbm = (w1_hbm, w3_hbm, w2_hbm)
    f32_bufs = (w1f, w3f, w2f)
    bf16_bufs = (w1b, w3b, w2b)

    def copies(e, s):
        return [pltpu.make_async_copy(hbm[k].at[layer, e], f32_bufs[k].at[s], sems.at[s, k])
                for k in range(3)]

    @pl.when(i < nused_ref[0])
    def _():
        @pl.when(new_ref[i] == 1)
        def _():
            s = slot_ref[i]

            @pl.when(i == 0)
            def _():
                for cp in copies(be_ref[0], 0):
                    cp.start()

            for cp in copies(be_ref[i], s):
                cp.wait()

            @pl.when(next_ref[i] >= 0)
            def _():
                for cp in copies(next_ref[i], 1 - s):
                    cp.start()

            for k in range(3):
                bf16_bufs[k][...] = f32_bufs[k][s].astype(BF16)

        x = _tiles_to_rows(x_ref[...]).astype(BF16)
        a = _dot(x, w1b[...])
        b = _dot(x, w3b[...])
        h = (a * jax.nn.sigmoid(a)) * b
        y_ref[...] = _rows_to_tiles(_dot(h.astype(BF16), w2b[...]))


def _experts(tables, xs, w1, w3, w2, layer):
    n_slots = xs.shape[0]
    tile = xs.shape[1:]
    d = tile[0] * tile[1]
    n_blocks = n_slots // MOE_ROWS
    e_dim = w1.shape[3]
    blk = lambda i, be, bn, bs, bx, nu: (jnp.minimum(i, nu[0] - 1), 0, 0)
    hbm = pl.BlockSpec(memory_space=pl.ANY)
    return pl.pallas_call(
        functools.partial(_expert_kernel, layer=layer),
        grid_spec=pltpu.PrefetchScalarGridSpec(
            num_scalar_prefetch=5,
            grid=(n_blocks,),
            in_specs=[pl.BlockSpec((MOE_ROWS,) + tile, blk), hbm, hbm, hbm],
            out_specs=pl.BlockSpec((MOE_ROWS,) + tile, blk),
            scratch_shapes=[pltpu.VMEM((2, d, e_dim), F32), pltpu.VMEM((2, d, e_dim), F32),
                            pltpu.VMEM((2, e_dim, d), F32),
                            pltpu.VMEM((d, e_dim), BF16), pltpu.VMEM((d, e_dim), BF16),
                            pltpu.VMEM((e_dim, d), BF16),
                            pltpu.SemaphoreType.DMA((2, 3))]),
        out_shape=jax.ShapeDtypeStruct(xs.shape, F32),
        compiler_params=_params("arbitrary"),
        name="moe_experts",
    )(*tables, xs, w1, w3, w2)


def _combine_kernel(pos_ref, x_ref, gate_ref, g2_ref, fw_ref, y_hbm, o_ref, buf, sems, *,
                    tok_offset, n_tok, final_norm):
    i = pl.program_id(0)
    n = pl.num_programs(0)
    tm = x_ref.shape[0]

    def issue(step, slot):
        def body(r, carry):
            tok = tok_offset + step * tm + r
            for k in range(TOP_K):
                pltpu.make_async_copy(y_hbm.at[pos_ref[k * n_tok + tok]],
                                      buf.at[slot, k, r], sems.at[slot]).start(priority=k)
            return carry
        lax.fori_loop(0, tm, body, 0, unroll=8)

    @pl.when(i == 0)
    def _():
        issue(0, 0)

    @pl.when(i + 1 < n)
    def _():
        issue(i + 1, (i + 1) % 2)

    slot = i % 2
    pltpu.make_async_copy(buf.at[slot], buf.at[slot], sems.at[slot]).wait()
    gate = gate_ref[...]
    f = (_tiles_to_rows(buf[slot, 0]) * gate[:, 0:1] + _tiles_to_rows(buf[slot, 1]) * gate[:, 1:2])
    out = x_ref[...] + g2_ref[...] * f
    if final_norm:
        out = _rms(out) * fw_ref[...]
    o_ref[...] = out


def _combine(pos, x2d, seq_len, gate, g2, fw, y, tok_offset, final_norm):
    rows, d = x2d.shape
    n_tok = gate.shape[0]
    tm = min(256, seq_len)
    tiles_per_seq = seq_len // tm
    off = tok_offset // tm
    nb = g2.shape[0]
    mod_idx = ((lambda i, p: (i // tiles_per_seq, 0, 0)) if nb > 1 else (lambda i, p: (0, 0, 0)))
    return pl.pallas_call(
        functools.partial(_combine_kernel, tok_offset=tok_offset, n_tok=n_tok,
                          final_norm=final_norm),
        grid_spec=pltpu.PrefetchScalarGridSpec(
            num_scalar_prefetch=1,
            grid=(rows // tm,),
            in_specs=[pl.BlockSpec((tm, d), lambda i, p: (i, 0)),
                      pl.BlockSpec((tm, TOP_K), lambda i, p: (i + off, 0)),
                      pl.BlockSpec((None, 1, d), mod_idx),
                      pl.BlockSpec((1, d), lambda i, p: (0, 0)),
                      pl.BlockSpec(memory_space=pl.ANY)],
            out_specs=pl.BlockSpec((tm, d), lambda i, p: (i, 0)),
            scratch_shapes=[pltpu.VMEM((2, TOP_K, tm) + y.shape[1:], F32),
                            pltpu.SemaphoreType.DMA((2,))]),
        out_shape=jax.ShapeDtypeStruct((rows, d), F32),
        compiler_params=_params("arbitrary"),
        name="moe_combine",
    )(pos, x2d, gate, g2, fw, y)


def _rope_tables(seq_len):
    rows = seq_len // GRID_W
    row = np.repeat(np.arange(rows), GRID_W).astype(np.float64)
    col = np.tile(np.arange(GRID_W), rows).astype(np.float64)
    n = HEAD_DIM // 4
    inv = (ROPE_THETA ** (-np.arange(n, dtype=np.float32) / n)).astype(np.float64)
    ang = np.concatenate([row[:, None] * inv, col[:, None] * inv], axis=-1)
    cos = np.cos(ang.astype(np.float32).astype(np.float64))
    sin = np.sin(ang.astype(np.float32).astype(np.float64))
    cos2 = np.concatenate([cos, cos], axis=-1).astype(np.float32)
    sin2 = np.concatenate([-sin, sin], axis=-1).astype(np.float32)
    return jnp.asarray(cos2), jnp.asarray(sin2)


def kernel(x, c, ctx, c_ctx, w_ada, b_ada, norm1_w, norm2_w, w_in, q_norm_w, k_norm_w,
           hy_conv_w, hy_conv_b, hy_pe_w1, hy_pe_b1, hy_freq1, hy_pe_w2, hy_pe_b2, hy_freq2,
           hy_pe_w3, hy_skip, w_att_proj, w_hy_proj, w_out, router_w, router_b,
           exp_w1, exp_w3, exp_w2, final_norm_w):
    B, S, D = x.shape
    C = ctx.shape[1]
    depth = w_ada.shape[0]
    n_lat = B * S
    n_ctx = B * C

    cos2, sin2 = _rope_tables(S)
    cos_id = jnp.ones((C, HEAD_DIM), F32)
    sin_id = jnp.zeros((C, HEAD_DIM), F32)
    dft_xf = _dft_matrices(S, False, S // 2)
    dft_cf = _dft_matrices(C, False, C // 2)
    dft_x = _dft_matrices(S // 2, True)
    dft_c = _dft_matrices(C // 2, True)

    mod_rows = 16
    cs = jnp.zeros((mod_rows, D), F32).at[:B].set(c).at[B].set(c_ctx)
    mods = _adaln(cs, w_ada, b_ada)

    r_hi = router_w.T.astype(BF16)
    r_lo = (router_w.T - r_hi.astype(F32)).astype(BF16)
    fw = final_norm_w.reshape(1, D)

    x2d = x.reshape(n_lat, D)
    c2d = ctx.reshape(n_ctx, D)
    for i in range(depth):
        last = i == depth - 1
        m_lat = [mods[i, :B, j * D:(j + 1) * D].reshape(B, 1, D) for j in range(6)]
        m_ctx = [mods[i, B:B + 1, j * D:(j + 1) * D].reshape(1, 1, D) for j in range(6)]
        wb = w_in[i].astype(BF16)
        n1 = norm1_w[i].reshape(1, D)
        n2 = norm2_w[i].reshape(1, D)
        qn = q_norm_w[i].reshape(1, HEAD_DIM)
        kn = k_norm_w[i].reshape(1, HEAD_DIM)
        wa = w_att_proj[i].astype(BF16)
        wh = w_hy_proj[i].astype(BF16)
        wo = w_out[i].astype(BF16)

        q, k, v, u, g = _inproj(x2d, S, m_lat[0], m_lat[1], n1, wb, qn, kn, cos2, sin2,
                                ("q", "k", "v", "u", "g"))
        if last:
            kc, vc = _inproj(c2d, C, m_ctx[0], m_ctx[1], n1, wb, qn, kn, cos_id, sin_id, ("k", "v"))
        else:
            qc, kc, vc, uc, gc = _inproj(c2d, C, m_ctx[0], m_ctx[1], n1, wb, qn, kn, cos_id, sin_id,
                                         ("q", "k", "v", "u", "g"))
        ya = _attention(q, [(k, v, S), (kc, vc, C)], B, S)

        filt = (hy_pe_w1[i], hy_pe_b1[i], hy_freq1[i], hy_pe_w2[i], hy_pe_b2[i], hy_freq2[i],
                hy_pe_w3[i])
        kpack = _hy_spectrum(dft_xf[0], dft_xf[1], *_hy_filter(S, *filt))
        yh = _hyena(u, B, S, dft_x, kpack, hy_conv_w[i], hy_conv_b[i], hy_skip[i])

        n_tok = n_lat if last else n_lat + n_ctx
        x2d, hx, logits = _merge(ya, yh, g, x2d, S, m_lat[2], m_lat[3], m_lat[4], n2,
                                 wa, wh, wo, r_hi, r_lo, n_tok, 0)
        if not last:
            ya_c = _attention(qc, [(kc, vc, C)], B, C)
            kpack = _hy_spectrum(dft_cf[0], dft_cf[1], *_hy_filter(C, *filt))
            yh_c = _hyena(uc, B, C, dft_c, kpack, hy_conv_w[i], hy_conv_b[i], hy_skip[i])
            c2d, hx, logits = _merge(ya_c, yh_c, gc, c2d, C, m_ctx[2], m_ctx[3], m_ctx[4], n2,
                                     wa, wh, wo, r_hi, r_lo, n_tok, n_lat, prev=(hx, logits))

        idx, gate, rank, counts = _route(logits, router_b)
        n_blocks = -(-(n_tok * TOP_K) // MOE_ROWS) + N_EXPERTS
        pad_start, pad_lo, pad_hi, tables = _layout(counts, n_blocks)
        dest = _slots(idx, rank, pad_start).reshape(TOP_K * n_tok)
        xs = _scatter_rows(dest, pad_lo, pad_hi, hx, n_blocks)
        y = _experts(tables, xs, exp_w1, exp_w3, exp_w2, i)
        gate_t = gate.T
        x2d = _combine(dest, x2d, S, gate_t, m_lat[5], fw, y, 0, last)
        if not last:
            c2d = _combine(dest, c2d, C, gate_t, m_ctx[5], fw, y, n_lat, False)
    return x2d.reshape(B, S, D)
```

```python
import functools
import math

import numpy as np
import jax
import jax.numpy as jnp
from jax import lax
from jax.experimental import pallas as pl
from jax.experimental.pallas import tpu as pltpu

F32 = jnp.float32
BF16 = jnp.bfloat16

D_MODEL = 1024
DEPTH = 2
GRID_W = 64
NORM_EPS = 1e-6
N_HEADS = 8
N_KV_HEADS = 2
HEAD_DIM = 128
ATT_WIDTH = N_HEADS * HEAD_DIM
KV_WIDTH = N_KV_HEADS * HEAD_DIM
ROPE_THETA = 10000.0
ATT_SCALE = HEAD_DIM ** -0.5
LOG2_E = math.log2(math.e)
HY_WIDTH = D_MODEL // 2
HY_ORDER = 2
HY_FILTER_HIDDEN = 64
HY_BANDS = 16
HY_PE_DIM = 1 + 2 * HY_BANDS
HY_PE_PAD = 128
HY_FAST_DECAY = 0.3
HY_SLOW_DECAY = 1.5
HY_DECAY_TARGET = 1e-2
N_EXPERTS = 64
N_GROUPS = 8
EXPERTS_PER_GROUP = N_EXPERTS // N_GROUPS
TOP_K = 2
EXPERT_DIM = 512
IN_WIDTH = ATT_WIDTH + 2 * KV_WIDTH + 3 * HY_WIDTH + 2 * D_MODEL
COL_Q = 0
COL_K = ATT_WIDTH
COL_V = ATT_WIDTH + KV_WIDTH
COL_U = ATT_WIDTH + 2 * KV_WIDTH
COL_G = COL_U + 3 * HY_WIDTH

MXU_COLS = 256
SUBLANES = 8
LANES = 128
ROW_TILE = 512
ATT_Q_TILE = 512
ATT_KEY_CHUNK = 256
DFT_ROWS = 64
HY_CHAINS = 4
MOE_ROWS = 256
VMEM_LIMIT = 56 * 1024 * 1024


def _dot(a, b):
    return jnp.dot(a, b, preferred_element_type=F32)


def _split(a):
    hi = a.astype(BF16)
    lo = (a - hi.astype(F32)).astype(BF16)
    return hi, lo


def _dot3(a, b):
    ah, al = _split(a)
    bh, bl = _split(b)
    return _dot(ah, bh) + (_dot(al, bh) + _dot(ah, bl))


def _tiles_to_rows(t):
    c = pltpu.einshape("tjl->jtl", t)
    return jnp.concatenate([c[j] for j in range(c.shape[0])], axis=1)


def _rows_to_tiles(x):
    c = jnp.stack([x[:, j * LANES:(j + 1) * LANES] for j in range(x.shape[1] // LANES)], axis=0)
    return pltpu.einshape("jtl->tjl", c)


def _rms(t):
    return t * lax.rsqrt(jnp.mean(t * t, axis=-1, keepdims=True) + NORM_EPS)


def _params(*sem):
    return pltpu.CompilerParams(dimension_semantics=sem, vmem_limit_bytes=VMEM_LIMIT)


def _adaln_kernel(c_ref, w_ref, b_ref, o_ref):
    c = c_ref[...]
    o_ref[...] = _dot3(c * jax.nn.sigmoid(c), w_ref[...]) + b_ref[...]


def _adaln(cs, w_ada, b_ada):
    depth, d, n = w_ada.shape
    rows = cs.shape[0]
    tn = 1536
    return pl.pallas_call(
        _adaln_kernel,
        grid=(depth, n // tn),
        in_specs=[pl.BlockSpec((rows, d), lambda l, j: (0, 0)),
                  pl.BlockSpec((None, d, tn), lambda l, j: (l, 0, j)),
                  pl.BlockSpec((None, 1, tn), lambda l, j: (l, 0, j))],
        out_specs=pl.BlockSpec((None, rows, tn), lambda l, j: (l, 0, j)),
        out_shape=jax.ShapeDtypeStruct((depth, rows, n), F32),
        compiler_params=_params("parallel", "parallel"),
        name="adaln",
    )(cs, w_ada, b_ada.reshape(depth, 1, n))


def _inproj_kernel(x_ref, sh_ref, sc_ref, nw_ref, w_ref, qn_ref, kn_ref, cos_ref, sin_ref,
                   *outs, sections):
    h = _rms(x_ref[...]) * nw_ref[...]
    h = h * (1.0 + sc_ref[...]) + sh_ref[...]
    hb = h.astype(BF16)
    cos = cos_ref[...]
    sin = sin_ref[...]
    o = dict(zip(sections, outs))

    def head_cols(col0, n_cols, norm_w, scale, out_ref):
        for c in range(n_cols // MXU_COLS):
            acc = _dot(hb, w_ref[:, col0 + c * MXU_COLS:col0 + (c + 1) * MXU_COLS])
            for j in range(MXU_COLS // HEAD_DIM):
                t = _rms(acc[:, j * HEAD_DIM:(j + 1) * HEAD_DIM]) * norm_w
                t = t * cos + pltpu.roll(t, HEAD_DIM // 2, 1) * sin
                lo = c * MXU_COLS + j * HEAD_DIM
                out_ref[:, lo:lo + HEAD_DIM] = (t * scale).astype(out_ref.dtype)

    if "q" in o:
        head_cols(COL_Q, ATT_WIDTH, qn_ref[...], ATT_SCALE * LOG2_E, o["q"])
    if "k" in o:
        head_cols(COL_K, KV_WIDTH, kn_ref[...], 1.0, o["k"])
    if "v" in o:
        acc = _dot(hb, w_ref[:, COL_V:COL_V + KV_WIDTH]).astype(BF16)
        ones = jnp.ones((acc.shape[0], HEAD_DIM), BF16)
        for j in range(N_KV_HEADS):
            o["v"][:, 2 * j * HEAD_DIM:(2 * j + 1) * HEAD_DIM] = acc[:, j * HEAD_DIM:(j + 1) * HEAD_DIM]
            o["v"][:, (2 * j + 1) * HEAD_DIM:(2 * j + 2) * HEAD_DIM] = ones
    if "u" in o:
        for c in range(3):
            o["u"][:, c * HY_WIDTH:(c + 1) * HY_WIDTH] = _dot(
                hb, w_ref[:, COL_U + c * HY_WIDTH:COL_U + (c + 1) * HY_WIDTH]).astype(BF16)
    if "g" in o:
        for c in range(4):
            acc = _dot(hb, w_ref[:, COL_G + c * 512:COL_G + (c + 1) * 512])
            o["g"][:, c * 512:(c + 1) * 512] = jax.nn.sigmoid(acc).astype(BF16)


_SECTION_SHAPES = {"q": (ATT_WIDTH, BF16), "k": (KV_WIDTH, BF16), "v": (2 * KV_WIDTH, BF16),
                   "u": (3 * HY_WIDTH, BF16), "g": (2 * D_MODEL, BF16)}


def _inproj(x2d, seq_len, sh, sc, nw, w_bf16, qn, kn, cos2, sin2, sections):
    rows, d = x2d.shape
    tm = min(ROW_TILE, seq_len)
    tiles_per_seq = seq_len // tm
    nb = sh.shape[0]
    mod_idx = (lambda i: (i // tiles_per_seq, 0, 0)) if nb > 1 else (lambda i: (0, 0, 0))
    const2 = lambda i: (0, 0)
    out_shape = [jax.ShapeDtypeStruct((rows, _SECTION_SHAPES[s][0]), _SECTION_SHAPES[s][1])
                 for s in sections]
    out_specs = [pl.BlockSpec((tm, _SECTION_SHAPES[s][0]), lambda i: (i, 0)) for s in sections]
    return pl.pallas_call(
        functools.partial(_inproj_kernel, sections=tuple(sections)),
        grid=(rows // tm,),
        in_specs=[pl.BlockSpec((tm, d), lambda i: (i, 0)),
                  pl.BlockSpec((None, 1, d), mod_idx),
                  pl.BlockSpec((None, 1, d), mod_idx),
                  pl.BlockSpec((1, d), const2),
                  pl.BlockSpec((d, IN_WIDTH), const2),
                  pl.BlockSpec((1, HEAD_DIM), const2),
                  pl.BlockSpec((1, HEAD_DIM), const2),
                  pl.BlockSpec((tm, HEAD_DIM), lambda i: (i % tiles_per_seq, 0)),
                  pl.BlockSpec((tm, HEAD_DIM), lambda i: (i % tiles_per_seq, 0))],
        out_specs=out_specs,
        out_shape=out_shape,
        compiler_params=_params("parallel"),
        name="inproj",
    )(x2d, sh, sc, nw, w_bf16, qn, kn, cos2, sin2)


def _attn_kernel(q_ref, *refs, n_kv_sets):
    kv = refs[:2 * n_kv_sets]
    o_ref = refs[2 * n_kv_sets]
    nt = (((1,), (1,)), ((), ()))
    for g in range(N_HEADS // N_KV_HEADS):
        q = q_ref[:, g * HEAD_DIM:(g + 1) * HEAD_DIM]
        m = None
        acc = None
        for s_idx in range(n_kv_sets):
            k_ref, v_ref = kv[2 * s_idx], kv[2 * s_idx + 1]
            lk = k_ref.shape[0]
            for c0 in range(0, lk, ATT_KEY_CHUNK):
                c1 = min(c0 + ATT_KEY_CHUNK, lk)
                s = lax.dot_general(q, k_ref[c0:c1, :], nt, preferred_element_type=F32)
                mc = jnp.max(s, axis=1, keepdims=True)
                if m is None:
                    m = mc
                    acc = _dot(jnp.exp2(s - m).astype(BF16), v_ref[c0:c1, :])
                else:
                    m_new = jnp.maximum(m, mc)
                    acc = jnp.exp2(m - m_new) * acc + _dot(jnp.exp2(s - m_new).astype(BF16),
                                                           v_ref[c0:c1, :])
                    m = m_new
        o_ref[:, g * HEAD_DIM:(g + 1) * HEAD_DIM] = (
            acc[:, :HEAD_DIM] / acc[:, HEAD_DIM:]).astype(o_ref.dtype)


def _attention(q, kv_sets, batch, seq_len):
    tq = min(ATT_Q_TILE, seq_len)
    nq = seq_len // tq
    grp = (N_HEADS // N_KV_HEADS) * HEAD_DIM
    in_specs = [pl.BlockSpec((tq, grp), lambda b, j, i: (b * nq + i, j))]
    args = [q]
    for k, v, lk in kv_sets:
        in_specs += [pl.BlockSpec((lk, HEAD_DIM), lambda b, j, i: (b, j)),
                     pl.BlockSpec((lk, 2 * HEAD_DIM), lambda b, j, i: (b, j))]
        args += [k, v]
    return pl.pallas_call(
        functools.partial(_attn_kernel, n_kv_sets=len(kv_sets)),
        grid=(batch, N_KV_HEADS, nq),
        in_specs=in_specs,
        out_specs=pl.BlockSpec((tq, grp), lambda b, j, i: (b * nq + i, j)),
        out_shape=jax.ShapeDtypeStruct(q.shape, BF16),
        compiler_params=_params("parallel", "parallel", "arbitrary"),
        name="attention",
    )(*args)


def _hy_filter_kernel(z_ref, t_ref, w1_ref, b1_ref, f1_ref, w2_ref, b2_ref, f2_ref, w3_ref,
                      dl_ref, sp_ref, sm_ref, spa_ref, sma_ref, mid_ref):
    i = pl.program_id(0)
    tl = z_ref.shape[0]
    h = jnp.sin(f1_ref[...] * (_dot3(z_ref[...], w1_ref[...]) + b1_ref[...]))
    h = jnp.sin(f2_ref[...] * (_dot3(h, w2_ref[...]) + b2_ref[...]))
    h = _dot3(h, w3_ref[...])
    window = jnp.exp(-t_ref[...] * dl_ref[...])
    row = i * tl + lax.broadcasted_iota(jnp.int32, (tl, HY_WIDTH), 0)
    alt = (1 - 2 * (row & 1)).astype(F32)
    quarter = row & 3
    cos4 = jnp.where(quarter == 0, 1.0, jnp.where(quarter == 2, -1.0, 0.0))
    sin4 = jnp.where(quarter == 1, 1.0, jnp.where(quarter == 3, -1.0, 0.0))

    @pl.when(i == 0)
    def _():
        mid_ref[...] = jnp.zeros_like(mid_ref)

    for o in range(HY_ORDER):
        cols = slice(o * HY_WIDTH, (o + 1) * HY_WIDTH)
        hf = h[:, (2 * o) * HY_WIDTH:(2 * o + 1) * HY_WIDTH] * window
        hb = h[:, (2 * o + 1) * HY_WIDTH:(2 * o + 2) * HY_WIDTH] * window
        hb = jnp.where(row == 0, 0.0, hb)
        plus = hf + hb
        minus = hf - hb
        sp_ref[:, cols] = plus.astype(BF16)
        sm_ref[:, cols] = minus.astype(BF16)
        spa_ref[:, cols] = (alt * plus).astype(BF16)
        sma_ref[:, cols] = (alt * minus).astype(BF16)
        mid_ref[0:1, cols] += jnp.sum(cos4 * plus, axis=0, keepdims=True)
        mid_ref[1:2, cols] += jnp.sum(sin4 * minus, axis=0, keepdims=True)


def _hy_filter(seq_len, pe_w1, pe_b1, freq1, pe_w2, pe_b2, freq2, pe_w3):
    t01 = np.linspace(0.0, 1.0, seq_len)[:, None]
    pos = np.arange(seq_len, dtype=np.float64)[:, None]
    bands = np.linspace(1e-4, HY_BANDS - 1, HY_BANDS)[None, :]
    f = 2.0 * math.pi * pos * bands / seq_len
    z = np.zeros((seq_len, HY_PE_PAD), np.float32)
    z[:, :HY_PE_DIM] = np.concatenate([t01, np.cos(f), -np.sin(f)], axis=-1)
    max_decay = math.log(HY_DECAY_TARGET) / HY_FAST_DECAY
    min_decay = math.log(HY_DECAY_TARGET) / HY_SLOW_DECAY
    deltas = np.abs(np.linspace(min_decay, max_decay, HY_WIDTH))[None, :].astype(np.float32)
    w1p = jnp.zeros((HY_PE_PAD, HY_FILTER_HIDDEN), F32).at[:HY_PE_DIM].set(pe_w1)
    tl = min(512, seq_len)
    hid = HY_FILTER_HIDDEN
    n_out = HY_ORDER * HY_WIDTH
    c2 = lambda i: (0, 0)
    return pl.pallas_call(
        _hy_filter_kernel,
        grid=(seq_len // tl,),
        in_specs=[pl.BlockSpec((tl, HY_PE_PAD), lambda i: (i, 0)),
                  pl.BlockSpec((tl, 1), lambda i: (i, 0)),
                  pl.BlockSpec((HY_PE_PAD, hid), c2), pl.BlockSpec((1, hid), c2),
                  pl.BlockSpec((1, hid), c2), pl.BlockSpec((hid, hid), c2),
                  pl.BlockSpec((1, hid), c2), pl.BlockSpec((1, hid), c2),
                  pl.BlockSpec((hid, 2 * n_out), c2), pl.BlockSpec((1, HY_WIDTH), c2)],
        out_specs=[pl.BlockSpec((tl, n_out), lambda i: (i, 0))] * 4
                  + [pl.BlockSpec((SUBLANES, n_out), c2)],
        out_shape=[jax.ShapeDtypeStruct((seq_len, n_out), BF16)] * 4
                  + [jax.ShapeDtypeStruct((SUBLANES, n_out), F32)],
        compiler_params=_params("arbitrary"),
        name="hy_filter",
    )(jnp.asarray(z), jnp.asarray(t01.astype(np.float32)), w1p, pe_b1.reshape(1, hid),
      freq1.reshape(1, hid), pe_w2, pe_b2.reshape(1, hid), freq2.reshape(1, hid), pe_w3,
      jnp.asarray(deltas))


def _dft_kernel(ca_ref, sa_ref, cb_ref, sb_ref, alt_ref, wc_ref, cm_ref, sm_ref, *inverse_refs):
    i = pl.program_id(0)
    ca = ca_ref[...]
    sa = sa_ref[...]
    cb = cb_ref[...]
    sb = sb_ref[...]
    c = ca * cb - sa * sb
    s = sa * cb + ca * sb
    rows = lax.broadcasted_iota(jnp.int32, c.shape, 0)
    cols = lax.broadcasted_iota(jnp.int32, c.shape, 1)
    wc = wc_ref[...]
    cm_ref[...] = c.astype(BF16)
    sm_ref[...] = jnp.where((rows == 0) & (i == 0), alt_ref[...], s).astype(BF16)
    if inverse_refs:
        ci_ref, si_ref = inverse_refs
        alt_t = (1 - 2 * (rows & 1)).astype(F32)
        ci_ref[...] = (c * wc).astype(BF16)
        si_ref[...] = (jnp.where(cols == 0, alt_t, s) * wc).astype(BF16)


def _dft_matrices(seq_len, inverse, n_rows=None):
    n = 2 * seq_len
    n_rows = seq_len if n_rows is None else n_rows
    idx = np.arange(seq_len, dtype=np.int64)[None, :]
    r1 = np.arange(n_rows // DFT_ROWS, dtype=np.int64)[:, None] * DFT_ROWS
    r0 = np.arange(DFT_ROWS, dtype=np.int64)[:, None]
    ang_a = ((r1 * idx) % n).astype(np.float64) * (2.0 * math.pi / n)
    ang_b = ((r0 * idx) % n).astype(np.float64) * (2.0 * math.pi / n)
    tab = lambda a: jnp.asarray(a.astype(np.float32))
    ca = tab(np.cos(ang_a)).reshape(-1, 1, seq_len)
    sa = tab(np.sin(ang_a)).reshape(-1, 1, seq_len)
    alt = tab(1.0 - 2.0 * (idx % 2))
    wc = tab(np.where(idx == 0, 1.0, 2.0) / n)
    row_blk = pl.BlockSpec((None, 1, seq_len), lambda i: (i, 0, 0))
    full = lambda r: pl.BlockSpec((r, seq_len), lambda i: (0, 0))
    out_blk = pl.BlockSpec((DFT_ROWS, seq_len), lambda i: (i, 0))
    n_out = 4 if inverse else 2
    mats = pl.pallas_call(
        _dft_kernel,
        grid=(n_rows // DFT_ROWS,),
        in_specs=[row_blk, row_blk, full(DFT_ROWS), full(DFT_ROWS), full(1), full(1)],
        out_specs=[out_blk] * n_out,
        out_shape=[jax.ShapeDtypeStruct((n_rows, seq_len), BF16)] * n_out,
        compiler_params=_params("parallel"),
        name="dft_matrices",
    )(ca, sa, tab(np.cos(ang_b)), tab(np.sin(ang_b)), alt, wc)
    if not inverse:
        return tuple(mats)
    tw = np.arange(seq_len, dtype=np.float64)[:, None] * (2.0 * math.pi / (2 * n))
    rep = lambda a: tab(np.broadcast_to(a, (seq_len, LANES)))
    return tuple(mats) + (rep(np.cos(tw)), rep(np.sin(tw)))


def _hy_spec_kernel(cm_ref, sm_ref, sp_ref, smn_ref, spa_ref, sma_ref, mid_ref,
                    klr_ref, kli_ref, khr_ref, khi_ref):
    i = pl.program_id(1)
    cm = cm_ref[...]
    sm = sm_ref[...]
    klr_ref[...] = _dot(cm, sp_ref[...])
    khr_ref[...] = _dot(cm, spa_ref[...])
    lo_q = _dot(sm, smn_ref[...])
    hi_q = _dot(sm, sma_ref[...])
    rows = lax.broadcasted_iota(jnp.int32, lo_q.shape, 0)
    edge = (rows == 0) & (i == 0)
    kli_ref[...] = jnp.where(edge, mid_ref[0:1, :], -lo_q)
    khi_ref[...] = jnp.where(edge, -mid_ref[1:2, :], hi_q)


def _hy_spectrum(cm, sm, splus, sminus, splus_alt, sminus_alt, mid):
    half, seq_len = cm.shape
    tf = min(256, half)
    mat = pl.BlockSpec((tf, seq_len), lambda o, i: (i, 0))
    sig = pl.BlockSpec((seq_len, HY_WIDTH), lambda o, i: (0, o))
    out = pl.BlockSpec((tf, HY_WIDTH), lambda o, i: (i, o))
    return pl.pallas_call(
        _hy_spec_kernel,
        grid=(HY_ORDER, half // tf),
        in_specs=[mat, mat, sig, sig, sig, sig,
                  pl.BlockSpec((SUBLANES, HY_WIDTH), lambda o, i: (0, o))],
        out_specs=[out] * 4,
        out_shape=[jax.ShapeDtypeStruct((half, HY_ORDER * HY_WIDTH), F32)] * 4,
        compiler_params=_params("parallel", "arbitrary"),
        name="hy_spectrum",
    )(cm, sm, splus, sminus, splus_alt, sminus_alt, mid)


def _hy_sconv_kernel(u_ref, w_ref, b_ref, ue_ref, uo_ref):
    half = u_ref.shape[0] // 2
    words = pltpu.bitcast(u_ref[...], jnp.uint32)
    xe = lax.bitcast_convert_type(words << 16, F32)
    xo = lax.bitcast_convert_type(words & jnp.uint32(0xFFFF0000), F32)
    rows = lax.broadcasted_iota(jnp.int32, xe.shape, 0)
    xo_prev = jnp.where(rows == 0, 0.0, pltpu.roll(xo, 1, 0))
    xe_next = jnp.where(rows == half - 1, 0.0, pltpu.roll(xe, half - 1, 0))
    w0, w1, w2 = w_ref[0:1, :], w_ref[1:2, :], w_ref[2:3, :]
    ye = xo_prev * w0 + xe * w1 + xo * w2 + b_ref[...]
    yo = xe * w0 + xo * w1 + xe_next * w2 + b_ref[...]
    ue_ref[...] = ye.astype(ue_ref.dtype)
    uo_ref[...] = yo.astype(uo_ref.dtype)


def _hy_sconv(u, batch, seq_len, conv_w, conv_b):
    ch = u.shape[1]
    tc = 256
    half = seq_len // 2
    u3 = u.reshape(batch, seq_len, ch)
    plane = pl.BlockSpec((None, half, tc), lambda b, j: (b, 0, j))
    ue, uo = pl.pallas_call(
        _hy_sconv_kernel,
        grid=(batch, ch // tc),
        in_specs=[pl.BlockSpec((None, seq_len, tc), lambda b, j: (b, 0, j)),
                  pl.BlockSpec((3, tc), lambda b, j: (0, j)),
                  pl.BlockSpec((1, tc), lambda b, j: (0, j))],
        out_specs=[plane, plane],
        out_shape=[jax.ShapeDtypeStruct((batch, half, ch), BF16)] * 2,
        compiler_params=_params("parallel", "parallel"),
        name="hy_sconv",
    )(u3, conv_w, conv_b.reshape(1, ch))
    return ue, uo


def _hy_fwd_kernel(cm_ref, sm_ref, tc_ref, ts_ref, ve_ref, vo_ref, klr_ref, kli_ref, khr_ref,
                   khi_ref, zp_ref, zq_ref):
    i = pl.program_id(0)
    w = ve_ref.shape[1]
    wc = w // HY_CHAINS
    reps = wc // LANES
    tc = jnp.concatenate([tc_ref[...]] * reps, axis=1)
    ts = jnp.concatenate([ts_ref[...]] * reps, axis=1)
    rows = lax.broadcasted_iota(jnp.int32, (cm_ref.shape[0], wc), 0)
    edge = (rows == 0) & (i == 0)
    cm = cm_ref[...]
    sm = sm_ref[...]
    for c in range(HY_CHAINS):
        lo, hi = c * wc, (c + 1) * wc
        v = jnp.concatenate([ve_ref[:, lo:hi], vo_ref[:, lo:hi]], axis=1)
        p = _dot(cm, v)
        q = _dot(sm, v)
        pe, po, qe, qo = p[:, :wc], p[:, wc:], q[:, :wc], q[:, wc:]

        a = tc * po - ts * qo
        b = tc * qo + ts * po
        xlr = pe + a
        xhr = pe - a
        xli = jnp.where(edge, qe, -(qe + b))
        xhi = jnp.where(edge, -qo, qe - b)

        klr, kli, khr, khi = klr_ref[:, lo:hi], kli_ref[:, lo:hi], khr_ref[:, lo:hi], khi_ref[:, lo:hi]
        ylr = xlr * klr - jnp.where(edge, 0.0, xli * kli)
        yhr = xhr * khr - jnp.where(edge, 0.0, xhi * khi)
        yli = jnp.where(edge, xli * kli - xhi * khi, xlr * kli + xli * klr)
        yhi = jnp.where(edge, xli * khi + xhi * kli, xhr * khi + xhi * khr)

        dr = ylr - yhr
        di = yli + yhi
        zp_ref[:, lo:hi] = (0.5 * (ylr + yhr)).astype(BF16)
        zp_ref[:, w + lo:w + hi] = (0.5 * (tc * dr - ts * di)).astype(BF16)
        zq_ref[:, lo:hi] = jnp.where(edge, yli, -0.5 * (yli - yhi)).astype(BF16)
        zq_ref[:, w + lo:w + hi] = jnp.where(edge, -yhi, -0.5 * (tc * di + ts * dr)).astype(BF16)


def _hy_forward(dfth, v_planes, v_blk, kpack, order):
    cm, sm, _, _, tc, ts = dfth
    batch, half, _ = v_planes[0].shape
    w = HY_WIDTH
    tf = min(512, half)
    mat = pl.BlockSpec((tf, half), lambda i, b: (i, 0))
    tw = pl.BlockSpec((tf, LANES), lambda i, b: (i, 0))
    spec = pl.BlockSpec((tf, w), lambda i, b: (i, order))
    out = pl.BlockSpec((None, tf, 2 * w), lambda i, b: (b, i, 0))
    return pl.pallas_call(
        _hy_fwd_kernel,
        grid=(half // tf, batch),
        in_specs=[mat, mat, tw, tw,
                  pl.BlockSpec((None, half, w), lambda i, b: (b, 0, v_blk)),
                  pl.BlockSpec((None, half, w), lambda i, b: (b, 0, v_blk)),
                  spec, spec, spec, spec],
        out_specs=[out, out],
        out_shape=[jax.ShapeDtypeStruct((batch, half, 2 * w), BF16)] * 2,
        compiler_params=_params("parallel", "arbitrary"),
        name="hy_forward",
    )(cm, sm, tc, ts, v_planes[0], v_planes[1], *kpack)


def _hy_inv_kernel(ci_ref, si_ref, zp_ref, zq_ref, ae_ref, ao_ref, ce_ref, co_ref, skip_ref, *outs,
                   interleave):
    w = ae_ref.shape[1]
    tt = ae_ref.shape[0]
    y = _dot(ci_ref[...], zp_ref[...]) + _dot(si_ref[...], zq_ref[...])
    skip = skip_ref[...]
    ze = ae_ref[...].astype(F32) * (y[:, :w] + skip * ce_ref[...].astype(F32))
    zo = ao_ref[...].astype(F32) * (y[:, w:] + skip * co_ref[...].astype(F32))
    if interleave:
        (o_ref,) = outs
        bits = lambda t: lax.bitcast_convert_type(t.astype(BF16).astype(F32), jnp.uint32)
        words = (bits(ze) >> 16) | (bits(zo) & jnp.uint32(0xFFFF0000))
        o_ref[...] = pltpu.bitcast(words, BF16)
    else:
        for k in range(len(outs) // 2):
            outs[2 * k][...] = ze.astype(outs[2 * k].dtype)
            outs[2 * k + 1][...] = zo.astype(outs[2 * k + 1].dtype)


def _hy_inverse(dfth, zp, zq, a_planes, a_blk, c_planes, c_blk, skip, out_dtypes, interleave):
    _, _, ci, si, _, _ = dfth
    batch, half, w2 = zp.shape
    w = w2 // 2
    tt = min(512, half)
    mat = pl.BlockSpec((tt, half), lambda i, b: (i, 0))
    sig = pl.BlockSpec((None, half, w2), lambda i, b: (b, 0, 0))
    a_spec = pl.BlockSpec((None, tt, w), lambda i, b: (b, i, a_blk))
    c_spec = pl.BlockSpec((None, tt, w), lambda i, b: (b, i, c_blk))
    plane = pl.BlockSpec((None, tt, w), lambda i, b: (b, i, 0))
    if interleave:
        (dt,) = out_dtypes
        out_specs = [pl.BlockSpec((None, 2 * tt, w), lambda i, b: (b, i, 0))]
        out_shape = [jax.ShapeDtypeStruct((batch, 2 * half, w), dt)]
    else:
        out_specs = [plane] * (2 * len(out_dtypes))
        out_shape = [jax.ShapeDtypeStruct((batch, half, w), dt) for dt in out_dtypes for _ in (0, 1)]
    outs = pl.pallas_call(
        functools.partial(_hy_inv_kernel, interleave=interleave),
        grid=(half // tt, batch),
        in_specs=[mat, mat, sig, sig, a_spec, a_spec, c_spec, c_spec,
                  pl.BlockSpec((1, w), lambda i, b: (0, 0))],
        out_specs=out_specs,
        out_shape=out_shape,
        compiler_params=_params("parallel", "arbitrary"),
        name="hy_inverse",
    )(ci, si, zp, zq, a_planes[0], a_planes[1], c_planes[0], c_planes[1], skip)
    if interleave:
        return outs[0]
    return [(outs[2 * k], outs[2 * k + 1]) for k in range(len(out_dtypes))]


def _hyena(u, batch, seq_len, dfth, kpack, conv_w, conv_b, skip):
    uc = _hy_sconv(u, batch, seq_len, conv_w, conv_b)
    zp, zq = _hy_forward(dfth, uc, 2, kpack, 0)
    z, zb = _hy_inverse(dfth, zp, zq, uc, 0, uc, 2, skip[0:1], (F32, BF16), False)
    zp, zq = _hy_forward(dfth, zb, 0, kpack, 1)
    y = _hy_inverse(dfth, zp, zq, uc, 1, z, 0, skip[1:2], (BF16,), True)
    return y.reshape(batch * seq_len, HY_WIDTH)


def _merge_kernel(ya_ref, yh_ref, g_ref, x_ref, g1_ref, sh_ref, sc_ref, nw_ref,
                  wa_ref, wh_ref, wo_ref, rh_ref, rl_ref, *rest):
    xo_ref, hx_ref, lg_ref = rest[-3:]
    a = _dot(ya_ref[...], wa_ref[...])
    h = _dot(yh_ref[...], wh_ref[...])
    m = g_ref[:, :D_MODEL].astype(F32) * a + g_ref[:, D_MODEL:].astype(F32) * h
    xn = x_ref[...] + g1_ref[...] * _dot(m.astype(BF16), wo_ref[...])
    xo_ref[...] = xn
    hx = _rms(xn) * nw_ref[...]
    hx = hx * (1.0 + sc_ref[...]) + sh_ref[...]
    hx_ref[...] = _rows_to_tiles(hx)
    hh, hl = _split(hx)
    nt = (((1,), (1,)), ((), ()))
    dn = lambda a, b: lax.dot_general(a, b, nt, preferred_element_type=F32)
    lg_ref[...] = dn(rh_ref[...], hh) + (dn(rh_ref[...], hl) + dn(rl_ref[...], hh))


def _merge(ya, yh, g, x2d, seq_len, g1, sh2, sc2, nw, wa, wh, wo, r_hi, r_lo,
           total_rows, row_offset, prev=None):
    rows, d = x2d.shape
    tm = min(ROW_TILE, seq_len)
    tiles_per_seq = seq_len // tm
    off = row_offset // tm
    nb = g1.shape[0]
    mod_idx = (lambda i: (i // tiles_per_seq, 0, 0)) if nb > 1 else (lambda i: (0, 0, 0))
    c2 = lambda i: (0, 0)
    row = lambda w: pl.BlockSpec((tm, w), lambda i: (i, 0))
    in_specs = [row(ATT_WIDTH), row(HY_WIDTH), row(2 * d), row(d),
                pl.BlockSpec((None, 1, d), mod_idx), pl.BlockSpec((None, 1, d), mod_idx),
                pl.BlockSpec((None, 1, d), mod_idx), pl.BlockSpec((1, d), c2),
                pl.BlockSpec((ATT_WIDTH, d), c2), pl.BlockSpec((HY_WIDTH, d), c2),
                pl.BlockSpec((d, d), c2), pl.BlockSpec((N_EXPERTS, d), c2),
                pl.BlockSpec((N_EXPERTS, d), c2)]
    args = [ya, yh, g, x2d, g1, sh2, sc2, nw, wa, wh, wo, r_hi, r_lo]
    aliases = {}
    if prev is not None:
        in_specs += [pl.BlockSpec(memory_space=pl.ANY)] * 2
        aliases = {len(args): 1, len(args) + 1: 2}
        args += list(prev)
    return pl.pallas_call(
        _merge_kernel,
        grid=(rows // tm,),
        in_specs=in_specs,
        out_specs=[row(d), pl.BlockSpec((tm, d // LANES, LANES), lambda i: (i + off, 0, 0)),
                   pl.BlockSpec((N_EXPERTS, tm), lambda i: (0, i + off))],
        out_shape=[jax.ShapeDtypeStruct((rows, d), F32),
                   jax.ShapeDtypeStruct((total_rows, d // LANES, LANES), F32),
                   jax.ShapeDtypeStruct((N_EXPERTS, total_rows), F32)],
        input_output_aliases=aliases,
        compiler_params=_params("arbitrary"),
        name="merge",
    )(*args)


def _route_kernel(lg_ref, bias_ref, idx_ref, gate_ref, rank_ref, cnt_ref, run_ref):
    step = pl.program_id(0)
    tn = lg_ref.shape[2]
    shape = (N_GROUPS, EXPERTS_PER_GROUP, tn)
    s = jax.nn.sigmoid(lg_ref[...])
    b = s + bias_ref[...]
    mem = lax.broadcasted_iota(jnp.int32, shape, 1).astype(F32)
    grp = lax.broadcasted_iota(jnp.int32, (N_GROUPS, 1, tn), 0).astype(F32)
    big = float(N_EXPERTS)

    m1 = jnp.max(b, axis=1, keepdims=True)
    i1 = jnp.min(jnp.where(b == m1, mem, big), axis=1, keepdims=True)
    b2 = jnp.where(mem == i1, -jnp.inf, b)
    m2 = jnp.max(b2, axis=1, keepdims=True)
    i2 = jnp.min(jnp.where(b2 == m2, mem, big), axis=1, keepdims=True)
    gs = m1 + m2
    gmax = jnp.max(gs, axis=0, keepdims=True)
    gsel = jnp.min(jnp.where(gs == gmax, grp, big), axis=0, keepdims=True)
    selg = grp == gsel

    @pl.when(step == 0)
    def _():
        run_ref[...] = jnp.zeros_like(run_ref)

    tri = (lax.broadcasted_iota(jnp.int32, (tn, tn), 0)
           < lax.broadcasted_iota(jnp.int32, (tn, tn), 1)).astype(BF16)
    run = run_ref[...]
    ws = []
    for k, ik in enumerate((i1, i2)):
        hit = selg & (mem == ik)
        ws.append(jnp.sum(jnp.where(hit, s, 0.0), axis=(0, 1), keepdims=True))
        e_loc = jnp.sum(jnp.where(selg, ik, 0.0), axis=0, keepdims=True)
        idx_ref[k:k + 1, :] = (gsel * EXPERTS_PER_GROUP + e_loc).reshape(1, tn).astype(jnp.int32)
        oh = jnp.where(hit, 1.0, 0.0).reshape(N_EXPERTS, tn)
        before = run + _dot(oh.astype(BF16), tri)
        rank_ref[k:k + 1, :] = jnp.sum(oh * before, axis=0, keepdims=True).astype(jnp.int32)
        run = run + jnp.sum(oh, axis=1, keepdims=True)
    run_ref[...] = run
    cnt_ref[...] = run
    tot = ws[0] + ws[1]
    for k in range(TOP_K):
        gate_ref[k:k + 1, :] = (ws[k] / tot).reshape(1, tn)


def _route(logits_t, router_b):
    n_tok = logits_t.shape[1]
    tn = 1024
    lg3 = logits_t.reshape(N_GROUPS, EXPERTS_PER_GROUP, n_tok)
    row2 = pl.BlockSpec((TOP_K, tn), lambda i: (0, i))
    return pl.pallas_call(
        _route_kernel,
        grid=(n_tok // tn,),
        in_specs=[pl.BlockSpec((N_GROUPS, EXPERTS_PER_GROUP, tn), lambda i: (0, 0, i)),
                  pl.BlockSpec((N_GROUPS, EXPERTS_PER_GROUP, 1), lambda i: (0, 0, 0))],
        out_specs=[row2, row2, row2, pl.BlockSpec((N_EXPERTS, 1), lambda i: (0, 0))],
        out_shape=[jax.ShapeDtypeStruct((TOP_K, n_tok), jnp.int32),
                   jax.ShapeDtypeStruct((TOP_K, n_tok), F32),
                   jax.ShapeDtypeStruct((TOP_K, n_tok), jnp.int32),
                   jax.ShapeDtypeStruct((N_EXPERTS, 1), F32)],
        scratch_shapes=[pltpu.VMEM((N_EXPERTS, 1), F32)],
        compiler_params=_params("arbitrary"),
        name="moe_route",
    )(lg3, router_b.astype(F32).reshape(N_GROUPS, EXPERTS_PER_GROUP, 1))


def _slot_kernel(idx_ref, rank_ref, start_ref, dest_ref):
    tn = idx_ref.shape[1]
    e = lax.broadcasted_iota(jnp.int32, (N_EXPERTS, tn), 0)
    for k in range(TOP_K):
        base = jnp.sum(jnp.where(e == idx_ref[k:k + 1, :], start_ref[...], 0.0), axis=0,
                       keepdims=True)
        dest_ref[k:k + 1, :] = base.astype(jnp.int32) + rank_ref[k:k + 1, :]


def _slots(idx, rank, pad_start):
    n_tok = idx.shape[1]
    tn = min(2048, n_tok)
    row2 = pl.BlockSpec((TOP_K, tn), lambda i: (0, i))
    return pl.pallas_call(
        _slot_kernel,
        grid=(n_tok // tn,),
        in_specs=[row2, row2, pl.BlockSpec((N_EXPERTS, 1), lambda i: (0, 0))],
        out_specs=row2,
        out_shape=jax.ShapeDtypeStruct((TOP_K, n_tok), jnp.int32),
        compiler_params=_params("parallel"),
        name="moe_slots",
    )(idx, rank, pad_start.astype(F32).reshape(N_EXPERTS, 1))


def _layout(counts, n_blocks):
    counts = counts.reshape(N_EXPERTS).astype(jnp.int32)
    padded = (counts + MOE_ROWS - 1) // MOE_ROWS * MOE_ROWS
    pad_end = jnp.cumsum(padded)
    pad_start = pad_end - padded
    blk_row = jnp.arange(n_blocks, dtype=jnp.int32)[:, None] * MOE_ROWS
    block_e = jnp.minimum(jnp.sum((pad_end[None, :] <= blk_row).astype(jnp.int32), axis=1),
                          N_EXPERTS - 1)
    prev_e = jnp.concatenate([jnp.full((1,), -1, jnp.int32), block_e[:-1]])
    block_new = (block_e != prev_e).astype(jnp.int32)
    block_slot = (jnp.cumsum(block_new) - 1) & 1
    e_ids = jnp.arange(N_EXPERTS, dtype=jnp.int32)
    later = (e_ids[None, :] > e_ids[:, None]) & (counts[None, :] > 0)
    next_e = jnp.min(jnp.where(later, e_ids[None, :], N_EXPERTS), axis=1)
    next_e = jnp.where(next_e == N_EXPERTS, -1, next_e)
    block_next = jnp.sum(jnp.where(block_e[:, None] == e_ids[None, :], next_e[None, :], 0), axis=1)
    n_used = (pad_end[-1] // MOE_ROWS).astype(jnp.int32).reshape(1)
    tables = (block_e, block_new, block_slot.astype(jnp.int32), block_next.astype(jnp.int32), n_used)
    return pad_start, pad_start + counts, pad_end, tables


_PAD_CHUNKS = tuple(2 ** p for p in range(int(math.log2(MOE_ROWS)) - 1, -1, -1))


def _scatter_kernel(dest_ref, lo_ref, hi_ref, x_ref, o_hbm, zeros, sem, zsem, *, n_tok):
    i = pl.program_id(0)
    tm = x_ref.shape[0]

    def pad_copies(act):
        def per_expert(e, carry):
            off = lo_ref[e]
            n = hi_ref[e] - off
            for c in _PAD_CHUNKS:
                @pl.when((n & c) != 0)
                def _():
                    act(pltpu.make_async_copy(zeros.at[pl.ds(0, c)], o_hbm.at[pl.ds(off, c)], zsem))
                off = off + (n & c)
            return carry
        lax.fori_loop(0, N_EXPERTS, per_expert, 0)

    @pl.when(i == 0)
    def _():
        zeros[...] = jnp.zeros_like(zeros)
        pad_copies(lambda cp: cp.start())

    def issue(r, carry):
        for k in range(TOP_K):
            pltpu.make_async_copy(x_ref.at[r], o_hbm.at[dest_ref[k * n_tok + i * tm + r]],
                                  sem).start(priority=k)
        return carry
    lax.fori_loop(0, tm, issue, 0, unroll=8)
    for k in range(TOP_K):
        pltpu.make_async_copy(x_ref, o_hbm.at[pl.ds(0, tm)], sem).wait()

    @pl.when(i == 0)
    def _():
        pad_copies(lambda cp: cp.wait())


def _scatter_rows(dest_flat, pad_lo, pad_hi, h, n_blocks):
    n_tok = h.shape[0]
    tile = h.shape[1:]
    tm = 256
    return pl.pallas_call(
        functools.partial(_scatter_kernel, n_tok=n_tok),
        grid_spec=pltpu.PrefetchScalarGridSpec(
            num_scalar_prefetch=3,
            grid=(n_tok // tm,),
            in_specs=[pl.BlockSpec((tm,) + tile, lambda i, *_: (i, 0, 0))],
            out_specs=pl.BlockSpec(memory_space=pl.ANY),
            scratch_shapes=[pltpu.VMEM((MOE_ROWS // 2,) + tile, h.dtype),
                            pltpu.SemaphoreType.DMA(()), pltpu.SemaphoreType.DMA(())]),
        out_shape=jax.ShapeDtypeStruct((n_blocks * MOE_ROWS,) + tile, h.dtype),
        compiler_params=pltpu.CompilerParams(dimension_semantics=("arbitrary",),
                                             vmem_limit_bytes=VMEM_LIMIT, has_side_effects=True),
        name="moe_scatter",
    )(dest_flat, pad_lo, pad_hi, h)


def _expert_kernel(be_ref, new_ref, slot_ref, next_ref, nused_ref, x_ref, w1_hbm, w3_hbm, w2_hbm,
                   y_ref, w1f, w3f, w2f, w1b, w3b, w2b, sems, *, layer):
    i = pl.program_id(0)
    hbm = (w1_hbm, w3_hbm, w2_hbm)
    f32_bufs = (w1f, w3f, w2f)
    bf16_bufs = (w1b, w3b, w2b)

    def copies(e, s):
        return [pltpu.make_async_copy(hbm[k].at[layer, e], f32_bufs[k].at[s], sems.at[s, k])
                for k in range(3)]

    @pl.when(i < nused_ref[0])
    def _():
        @pl.when(new_ref[i] == 1)
        def _():
            s = slot_ref[i]

            @pl.when(i == 0)
            def _():
                for cp in copies(be_ref[0], 0):
                    cp.start()

            for cp in copies(be_ref[i], s):
                cp.wait()

            @pl.when(next_ref[i] >= 0)
            def _():
                for cp in copies(next_ref[i], 1 - s):
                    cp.start(priority=1)

            for k in range(3):
                bf16_bufs[k][...] = f32_bufs[k][s].astype(BF16)

        x = _tiles_to_rows(x_ref[...]).astype(BF16)
        a = _dot(x, w1b[...])
        b = _dot(x, w3b[...])
        h = (a * jax.nn.sigmoid(a)) * b
        y_ref[...] = _rows_to_tiles(_dot(h.astype(BF16), w2b[...]))


def _experts(tables, xs, w1, w3, w2, layer):
    n_slots = xs.shape[0]
    tile = xs.shape[1:]
    d = tile[0] * tile[1]
    n_blocks = n_slots // MOE_ROWS
    e_dim = w1.shape[3]
    blk = lambda i, be, bn, bs, bx, nu: (jnp.minimum(i, nu[0] - 1), 0, 0)
    hbm = pl.BlockSpec(memory_space=pl.ANY)
    return pl.pallas_call(
        functools.partial(_expert_kernel, layer=layer),
        grid_spec=pltpu.PrefetchScalarGridSpec(
            num_scalar_prefetch=5,
            grid=(n_blocks,),
            in_specs=[pl.BlockSpec((MOE_ROWS,) + tile, blk), hbm, hbm, hbm],
            out_specs=pl.BlockSpec((MOE_ROWS,) + tile, blk),
            scratch_shapes=[pltpu.VMEM((2, d, e_dim), F32), pltpu.VMEM((2, d, e_dim), F32),
                            pltpu.VMEM((2, e_dim, d), F32),
                            pltpu.VMEM((d, e_dim), BF16), pltpu.VMEM((d, e_dim), BF16),
                            pltpu.VMEM((e_dim, d), BF16),
                            pltpu.SemaphoreType.DMA((2, 3))]),
        out_shape=jax.ShapeDtypeStruct(xs.shape, F32),
        compiler_params=_params("arbitrary"),
        name="moe_experts",
    )(*tables, xs, w1, w3, w2)


def _combine_kernel(pos_ref, x_ref, gate_ref, g2_ref, fw_ref, y_hbm, o_ref, buf, sems, *,
                    tok_offset, n_tok, final_norm):
    i = pl.program_id(0)
    n = pl.num_programs(0)
    tm = x_ref.shape[0]

    def issue(step, slot):
        def body(r, carry):
            tok = tok_offset + step * tm + r
            for k in range(TOP_K):
                pltpu.make_async_copy(y_hbm.at[pos_ref[k * n_tok + tok]],
                                      buf.at[slot, k, r], sems.at[slot]).start(priority=k)
            return carry
        lax.fori_loop(0, tm, body, 0, unroll=8)

    @pl.when(i == 0)
    def _():
        issue(0, 0)

    @pl.when(i + 1 < n)
    def _():
        issue(i + 1, (i + 1) % 2)

    slot = i % 2
    pltpu.make_async_copy(buf.at[slot], buf.at[slot], sems.at[slot]).wait()
    gate = gate_ref[...]
    f = (_tiles_to_rows(buf[slot, 0]) * gate[:, 0:1] + _tiles_to_rows(buf[slot, 1]) * gate[:, 1:2])
    out = x_ref[...] + g2_ref[...] * f
    if final_norm:
        out = _rms(out) * fw_ref[...]
    o_ref[...] = out


def _combine(pos, x2d, seq_len, gate, g2, fw, y, tok_offset, final_norm):
    rows, d = x2d.shape
    n_tok = gate.shape[0]
    tm = min(256, seq_len)
    tiles_per_seq = seq_len // tm
    off = tok_offset // tm
    nb = g2.shape[0]
    mod_idx = ((lambda i, p: (i // tiles_per_seq, 0, 0)) if nb > 1 else (lambda i, p: (0, 0, 0)))
    return pl.pallas_call(
        functools.partial(_combine_kernel, tok_offset=tok_offset, n_tok=n_tok,
                          final_norm=final_norm),
        grid_spec=pltpu.PrefetchScalarGridSpec(
            num_scalar_prefetch=1,
            grid=(rows // tm,),
            in_specs=[pl.BlockSpec((tm, d), lambda i, p: (i, 0)),
                      pl.BlockSpec((tm, TOP_K), lambda i, p: (i + off, 0)),
                      pl.BlockSpec((None, 1, d), mod_idx),
                      pl.BlockSpec((1, d), lambda i, p: (0, 0)),
                      pl.BlockSpec(memory_space=pl.ANY)],
            out_specs=pl.BlockSpec((tm, d), lambda i, p: (i, 0)),
            scratch_shapes=[pltpu.VMEM((2, TOP_K, tm) + y.shape[1:], F32),
                            pltpu.SemaphoreType.DMA((2,))]),
        out_shape=jax.ShapeDtypeStruct((rows, d), F32),
        compiler_params=_params("arbitrary"),
        name="moe_combine",
    )(pos, x2d, gate, g2, fw, y)


def _rope_tables(seq_len):
    rows = seq_len // GRID_W
    row = np.repeat(np.arange(rows), GRID_W).astype(np.float64)
    col = np.tile(np.arange(GRID_W), rows).astype(np.float64)
    n = HEAD_DIM // 4
    inv = (ROPE_THETA ** (-np.arange(n, dtype=np.float32) / n)).astype(np.float64)
    ang = np.concatenate([row[:, None] * inv, col[:, None] * inv], axis=-1)
    cos = np.cos(ang.astype(np.float32).astype(np.float64))
    sin = np.sin(ang.astype(np.float32).astype(np.float64))
    cos2 = np.concatenate([cos, cos], axis=-1).astype(np.float32)
    sin2 = np.concatenate([-sin, sin], axis=-1).astype(np.float32)
    return jnp.asarray(cos2), jnp.asarray(sin2)


def kernel(x, c, ctx, c_ctx, w_ada, b_ada, norm1_w, norm2_w, w_in, q_norm_w, k_norm_w,
           hy_conv_w, hy_conv_b, hy_pe_w1, hy_pe_b1, hy_freq1, hy_pe_w2, hy_pe_b2, hy_freq2,
           hy_pe_w3, hy_skip, w_att_proj, w_hy_proj, w_out, router_w, router_b,
           exp_w1, exp_w3, exp_w2, final_norm_w):
    B, S, D = x.shape
    C = ctx.shape[1]
    depth = w_ada.shape[0]
    n_lat = B * S
    n_ctx = B * C

    cos2, sin2 = _rope_tables(S)
    cos_id = jnp.ones((C, HEAD_DIM), F32)
    sin_id = jnp.zeros((C, HEAD_DIM), F32)
    dft_xf = _dft_matrices(S, False, S // 2)
    dft_cf = _dft_matrices(C, False, C // 2)
    dft_x = _dft_matrices(S // 2, True)
    dft_c = _dft_matrices(C // 2, True)

    mod_rows = 16
    cs = jnp.zeros((mod_rows, D), F32).at[:B].set(c).at[B].set(c_ctx)
    mods = _adaln(cs, w_ada, b_ada)

    r_hi = router_w.T.astype(BF16)
    r_lo = (router_w.T - r_hi.astype(F32)).astype(BF16)
    fw = final_norm_w.reshape(1, D)

    x2d = x.reshape(n_lat, D)
    c2d = ctx.reshape(n_ctx, D)
    for i in range(depth):
        last = i == depth - 1
        m_lat = [mods[i, :B, j * D:(j + 1) * D].reshape(B, 1, D) for j in range(6)]
        m_ctx = [mods[i, B:B + 1, j * D:(j + 1) * D].reshape(1, 1, D) for j in range(6)]
        wb = w_in[i].astype(BF16)
        n1 = norm1_w[i].reshape(1, D)
        n2 = norm2_w[i].reshape(1, D)
        qn = q_norm_w[i].reshape(1, HEAD_DIM)
        kn = k_norm_w[i].reshape(1, HEAD_DIM)
        wa = w_att_proj[i].astype(BF16)
        wh = w_hy_proj[i].astype(BF16)
        wo = w_out[i].astype(BF16)

        q, k, v, u, g = _inproj(x2d, S, m_lat[0], m_lat[1], n1, wb, qn, kn, cos2, sin2,
                                ("q", "k", "v", "u", "g"))
        if last:
            kc, vc = _inproj(c2d, C, m_ctx[0], m_ctx[1], n1, wb, qn, kn, cos_id, sin_id, ("k", "v"))
        else:
            qc, kc, vc, uc, gc = _inproj(c2d, C, m_ctx[0], m_ctx[1], n1, wb, qn, kn, cos_id, sin_id,
                                         ("q", "k", "v", "u", "g"))
        ya = _attention(q, [(k, v, S), (kc, vc, C)], B, S)

        filt = (hy_pe_w1[i], hy_pe_b1[i], hy_freq1[i], hy_pe_w2[i], hy_pe_b2[i], hy_freq2[i],
                hy_pe_w3[i])
        kpack = _hy_spectrum(dft_xf[0], dft_xf[1], *_hy_filter(S, *filt))
        yh = _hyena(u, B, S, dft_x, kpack, hy_conv_w[i], hy_conv_b[i], hy_skip[i])

        n_tok = n_lat if last else n_lat + n_ctx
        x2d, hx, logits = _merge(ya, yh, g, x2d, S, m_lat[2], m_lat[3], m_lat[4], n2,
                                 wa, wh, wo, r_hi, r_lo, n_tok, 0)
        if not last:
            ya_c = _attention(qc, [(kc, vc, C)], B, C)
            kpack = _hy_spectrum(dft_cf[0], dft_cf[1], *_hy_filter(C, *filt))
            yh_c = _hyena(uc, B, C, dft_c, kpack, hy_conv_w[i], hy_conv_b[i], hy_skip[i])
            c2d, hx, logits = _merge(ya_c, yh_c, gc, c2d, C, m_ctx[2], m_ctx[3], m_ctx[4], n2,
                                     wa, wh, wo, r_hi, r_lo, n_tok, n_lat, prev=(hx, logits))

        idx, gate, rank, counts = _route(logits, router_b)
        n_blocks = -(-(n_tok * TOP_K) // MOE_ROWS) + N_EXPERTS
        pad_start, pad_lo, pad_hi, tables = _layout(counts, n_blocks)
        dest = _slots(idx, rank, pad_start).reshape(TOP_K * n_tok)
        xs = _scatter_rows(dest, pad_lo, pad_hi, hx, n_blocks)
        y = _experts(tables, xs, exp_w1, exp_w3, exp_w2, i)
        gate_t = gate.T
        x2d = _combine(dest, x2d, S, gate_t, m_lat[5], fw, y, 0, last)
        if not last:
            c2d = _combine(dest, c2d, C, gate_t, m_ctx[5], fw, y, n_lat, False)
    return x2d.reshape(B, S, D)
```
